```python
import math
import jax, jax.numpy as jnp
from jax import lax
import numpy as np

D_MODEL = 1024
BATCH = 8
SEQ = 2048
DEPTH = 1
DEC_BATCH = 128
DEC_SEQ = 1
PAST_LEN = 16384
PAGE_SIZE = 128

D_MIX = D_MODEL
D_S5 = D_MIX // 2
S5_CH = 16
S5_GROUPS = D_S5 // S5_CH
S5_STATE = 64
D_POOL = D_MIX - D_S5
POOL_WINDOWS = (2, 4, 8, 16)
N_POOL = len(POOL_WINDOWS)
POOL_CH = D_POOL // N_POOL
POOL_BUF = max(POOL_WINDOWS) - 1
D_FF = ((8 * D_MODEL // 3 + 127) // 128) * 128
CONV_W = 3
EPS = 1e-6
DT_MIN = 1e-3
DT_MAX = 1e-1

kernel_name = "hymba_s5_pool_convffn_step"


def _rmsnorm(x, g):
    xf = x.astype(jnp.float32)
    y = xf * lax.rsqrt(jnp.mean(xf * xf, axis=-1, keepdims=True) + EPS)
    return (y * g.astype(jnp.float32)).astype(x.dtype)


def _lin_rec_op(e1, e2):
    a1, b1 = e1
    a2, b2 = e2
    return a1 * a2, a2 * b1 + b2


def _s5_mixer(u, h0_re, h0_im, a_re, a_im, log_dt, b_re, b_im, c_re, c_im, d_skip, w_glu):
    f32 = jnp.float32
    n, l, _ = u.shape
    uf = u.astype(f32).reshape(n, l, S5_GROUPS, S5_CH)
    a = lax.complex(a_re.astype(f32), a_im.astype(f32))
    dt_a = jnp.exp(log_dt.astype(f32))[:, None] * a
    a_bar = jnp.exp(dt_a)
    b = lax.complex(b_re.astype(f32), b_im.astype(f32))
    b_bar = ((a_bar - 1.0) / a)[:, :, None] * b
    bu = jnp.einsum('gph,nlgh->nlgp', b_bar, uf.astype(jnp.complex64))
    a_seq = jnp.broadcast_to(a_bar, bu.shape)
    _, h = lax.associative_scan(_lin_rec_op, (a_seq, bu), axis=1)
    steps = jnp.arange(1, l + 1, dtype=f32)[:, None, None]
    h0 = lax.complex(h0_re.astype(f32), h0_im.astype(f32))
    h = h + jnp.exp(steps * dt_a)[None] * h0[:, None]
    c = lax.complex(c_re.astype(f32), c_im.astype(f32))
    y = jnp.real(jnp.einsum('ghp,nlgp->nlgh', c, h)) + d_skip.astype(f32).reshape(S5_GROUPS, S5_CH) * uf
    y = jax.nn.gelu(y.reshape(n, l, D_S5), approximate=False)
    y = y * jax.nn.sigmoid(y @ w_glu.astype(f32))
    h_last = h[:, -1]
    return y.astype(u.dtype), jnp.real(h_last), jnp.imag(h_last)


def _pool_mixer(v, buf, n_past, w_pool, pool_scale):
    f32 = jnp.float32
    n, l, _ = v.shape
    seq = jnp.concatenate([buf.astype(f32), v.astype(f32)], axis=1)
    cs = jnp.concatenate([jnp.zeros((n, 1, D_POOL), f32), jnp.cumsum(seq, axis=1)], axis=1)
    end = cs[:, POOL_BUF + 1:]
    pos = n_past + jnp.arange(l, dtype=jnp.int32) + 1
    means = []
    for gi, w in enumerate(POOL_WINDOWS):
        sl = slice(gi * POOL_CH, (gi + 1) * POOL_CH)
        start = cs[:, POOL_BUF + 1 - w:POOL_BUF + 1 - w + l, sl]
        cnt = jnp.minimum(pos, w).astype(f32)[None, :, None]
        means.append((end[..., sl] - start) / cnt)
    pooled = jnp.concatenate(means, axis=-1) - seq[:, POOL_BUF:]
    z = jnp.einsum('nlgc,gcd->nlgd', pooled.reshape(n, l, N_POOL, POOL_CH), w_pool.astype(f32))
    out = z.reshape(n, l, D_POOL) * pool_scale.astype(f32)
    new_buf = seq[:, -POOL_BUF:]
    return out.astype(v.dtype), new_buf.astype(v.dtype)


def _conv_ffn(x, buf, w_up, conv_w, conv_b, w_down):
    l = x.shape[1]
    hup = x @ w_up
    seq = jnp.concatenate([buf.astype(hup.dtype), hup], axis=1)
    conv = conv_b + sum(conv_w[k] * seq[:, k:k + l] for k in range(CONV_W))
    gate, val = conv[..., :D_FF], conv[..., D_FF:]
    out = (jax.nn.gelu(gate, approximate=False) * val) @ w_down
    return out, seq[:, -(CONV_W - 1):]


def _layer(x, h0_re, h0_im, pool_buf, conv_buf, n_past,
           norm_mix_g, w_in, s5_a_re, s5_a_im, s5_log_dt, s5_b_re, s5_b_im,
           s5_c_re, s5_c_im, s5_d, s5_w_glu, pool_w, pool_scale, w_out,
           norm_ffn_g, ffn_w_up, ffn_conv_w, ffn_conv_b, ffn_w_down):
    h = _rmsnorm(x, norm_mix_g)
    proj = h @ w_in
    u, v = proj[..., :D_S5], proj[..., D_S5:]
    y_s5, new_re, new_im = _s5_mixer(u, h0_re, h0_im, s5_a_re, s5_a_im, s5_log_dt,
                                     s5_b_re, s5_b_im, s5_c_re, s5_c_im, s5_d, s5_w_glu)
    y_pool, new_pool = _pool_mixer(v, pool_buf, n_past, pool_w, pool_scale)
    x = x + jnp.concatenate([y_s5, y_pool], axis=-1) @ w_out
    h = _rmsnorm(x, norm_ffn_g)
    y_ffn, new_conv = _conv_ffn(h, conv_buf, ffn_w_up, ffn_conv_w, ffn_conv_b, ffn_w_down)
    x = x + y_ffn
    return x, new_re, new_im, new_pool, new_conv


def setup_inputs(seed: int = 0) -> dict:
    key = jax.random.key(seed)
    ks = jax.random.split(key, 32)
    f32 = jnp.float32
    nrm = lambda k, s, sc: jax.random.normal(k, s, f32) * sc
    n_idx = jnp.arange(S5_STATE, dtype=f32)
    return {
        "x_prompt": nrm(ks[0], (BATCH, SEQ, D_MODEL), 1.0),
        "x_sample": nrm(ks[1], (DEC_BATCH, DEC_SEQ, D_MODEL), 1.0),
        "state_s5_re": nrm(ks[2], (DEPTH, DEC_BATCH, S5_GROUPS, S5_STATE), 0.3),
        "state_s5_im": nrm(ks[3], (DEPTH, DEC_BATCH, S5_GROUPS, S5_STATE), 0.3),
        "state_pool": nrm(ks[4], (DEPTH, DEC_BATCH, POOL_BUF, D_POOL), 1.0),
        "state_ffn_conv": nrm(ks[5], (DEPTH, DEC_BATCH, CONV_W - 1, 2 * D_FF), 1.0),
        "norm_mix_g": 1.0 + nrm(ks[6], (DEPTH, D_MODEL), 0.02),
        "w_in": nrm(ks[7], (DEPTH, D_MODEL, D_MIX), D_MODEL ** -0.5),
        "s5_a_re": -0.5 + nrm(ks[8], (DEPTH, S5_GROUPS, S5_STATE), 0.01),
        "s5_a_im": math.pi * n_idx + nrm(ks[9], (DEPTH, S5_GROUPS, S5_STATE), 0.01),
        "s5_log_dt": jax.random.uniform(ks[10], (DEPTH, S5_GROUPS), f32,
                                         math.log(DT_MIN), math.log(DT_MAX)),
        "s5_b_re": nrm(ks[11], (DEPTH, S5_GROUPS, S5_STATE, S5_CH), (2 * S5_CH) ** -0.5),
        "s5_b_im": nrm(ks[12], (DEPTH, S5_GROUPS, S5_STATE, S5_CH), (2 * S5_CH) ** -0.5),
        "s5_c_re": nrm(ks[13], (DEPTH, S5_GROUPS, S5_CH, S5_STATE), (2 * S5_STATE) ** -0.5),
        "s5_c_im": nrm(ks[14], (DEPTH, S5_GROUPS, S5_CH, S5_STATE), (2 * S5_STATE) ** -0.5),
        "s5_d": nrm(ks[15], (DEPTH, D_S5), 1.0),
        "s5_w_glu": nrm(ks[16], (DEPTH, D_S5, D_S5), D_S5 ** -0.5),
        "pool_w": nrm(ks[17], (DEPTH, N_POOL, POOL_CH, POOL_CH), POOL_CH ** -0.5),
        "pool_scale": 1.0 + nrm(ks[18], (DEPTH, D_POOL), 0.02),
        "w_out": nrm(ks[19], (DEPTH, D_MIX, D_MODEL), D_MIX ** -0.5),
        "norm_ffn_g": 1.0 + nrm(ks[20], (DEPTH, D_MODEL), 0.02),
        "ffn_w_up": nrm(ks[21], (DEPTH, D_MODEL, 2 * D_FF), D_MODEL ** -0.5),
        "ffn_conv_w": nrm(ks[22], (DEPTH, CONV_W, 2 * D_FF), CONV_W ** -0.5),
        "ffn_conv_b": nrm(ks[23], (DEPTH, 2 * D_FF), 0.02),
        "ffn_w_down": nrm(ks[24], (DEPTH, D_FF, D_MODEL), D_FF ** -0.5),
        "norm_final_g": 1.0 + nrm(ks[25], (D_MODEL,), 0.02),
    }


def reference(x_prompt, x_sample, state_s5_re, state_s5_im, state_pool, state_ffn_conv,
              norm_mix_g, w_in, s5_a_re, s5_a_im, s5_log_dt, s5_b_re, s5_b_im,
              s5_c_re, s5_c_im, s5_d, s5_w_glu, pool_w, pool_scale, w_out,
              norm_ffn_g, ffn_w_up, ffn_conv_w, ffn_conv_b, ffn_w_down, norm_final_g):
    f32 = jnp.float32
    xp, xs = x_prompt, x_sample
    p_re, p_im, p_pool, p_conv = [], [], [], []
    s_re, s_im, s_pool, s_conv = [], [], [], []
    for i in range(DEPTH):
        w = (norm_mix_g[i], w_in[i], s5_a_re[i], s5_a_im[i], s5_log_dt[i], s5_b_re[i],
             s5_b_im[i], s5_c_re[i], s5_c_im[i], s5_d[i], s5_w_glu[i], pool_w[i],
             pool_scale[i], w_out[i], norm_ffn_g[i], ffn_w_up[i], ffn_conv_w[i],
             ffn_conv_b[i], ffn_w_down[i])
        zs = jnp.zeros((BATCH, S5_GROUPS, S5_STATE), f32)
        zp = jnp.zeros((BATCH, POOL_BUF, D_POOL), xp.dtype)
        zc = jnp.zeros((BATCH, CONV_W - 1, 2 * D_FF), xp.dtype)
        xp, a, b, c, d = _layer(xp, zs, zs, zp, zc, 0, *w)
        p_re.append(a); p_im.append(b); p_pool.append(c); p_conv.append(d)
        xs, a, b, c, d = _layer(xs, state_s5_re[i], state_s5_im[i], state_pool[i],
                                state_ffn_conv[i], PAST_LEN, *w)
        s_re.append(a); s_im.append(b); s_pool.append(c); s_conv.append(d)
    y_prompt = _rmsnorm(xp, norm_final_g)
    y_sample = _rmsnorm(xs, norm_final_g)
    return (y_prompt, y_sample,
            jnp.stack(p_re), jnp.stack(p_im), jnp.stack(p_pool), jnp.stack(p_conv),
            jnp.stack(s_re), jnp.stack(s_im), jnp.stack(s_pool), jnp.stack(s_conv))
```

```python
import functools
import math

import numpy as np
import jax
import jax.numpy as jnp
from jax import lax
from jax.experimental import pallas as pl
from jax.experimental.pallas import tpu as pltpu

D_MODEL = 1024
D_S5 = 512
S5_CH = 16
S5_GROUPS = 32
S5_STATE = 64
D_POOL = 512
POOL_WINDOWS = (2, 4, 8, 16)
POOL_CH = 128
POOL_BUF = 15
D_FF = 2816
CONV_W = 3
EPS = 1e-6

SUBLANES = 8
S5_SPLIT = 2
GROUPS_PER_SPLIT = S5_GROUPS // S5_SPLIT
U_PER_SPLIT = GROUPS_PER_SPLIT * S5_CH
ST_PER_SPLIT = GROUPS_PER_SPLIT * S5_STATE
FF_CHUNK = 256
N_FF_CHUNKS = D_FF // FF_CHUNK
SQRT_HALF = float(np.sqrt(0.5).astype(np.float32))

BF16 = jnp.bfloat16
F32 = jnp.float32


def _dot(a, b):
    return jnp.dot(a, b, preferred_element_type=F32)


def _rms(x, g):
    ms = jnp.mean(x * x, axis=-1, keepdims=True)
    return x * lax.rsqrt(ms + EPS) * g


def _gelu(x):
    return 0.5 * x * (1.0 + lax.erf(x * SQRT_HALF))


def _s5_post(y_lin, u, dskip_ref, wglu_ref):
    y = _gelu(y_lin + dskip_ref[...] * u)
    return y * jax.nn.sigmoid(_dot(y.astype(BF16), wglu_ref[...]))


def _pool_project(pooled_cols, wpool_ref, pscale_ref):
    outs = []
    for gi in range(len(POOL_WINDOWS)):
        z = _dot(pooled_cols[gi].astype(BF16), wpool_ref[gi])
        outs.append(z * pscale_ref[:, gi * POOL_CH:(gi + 1) * POOL_CH])
    return jnp.concatenate(outs, axis=1)


def _prompt_kernel(tc,
                   x_ref, g1_ref, win_ref, are_ref, aim_ref, bbd_ref, cbd_ref, dskip_ref,
                   wglu_ref, wpool_ref, pscale_ref, wout_ref, g2_ref, wup_ref, cw_ref,
                   cb_ref, wdown_ref, gf_ref,
                   y_ref, sre_ref, sim_ref, pool_ref, conv_ref,
                   bu_ref, h_ref):
    rows = tc * SUBLANES
    step = pl.program_id(0)

    @pl.when(step == 0)
    def _():
        sre_ref[...] = jnp.zeros_like(sre_ref)
        sim_ref[...] = jnp.zeros_like(sim_ref)
        pool_ref[...] = jnp.zeros_like(pool_ref)
        conv_ref[...] = jnp.zeros_like(conv_ref)

    x = x_ref[...]
    hb = _rms(x, g1_ref[...]).astype(BF16)
    proj = _dot(hb, win_ref[...])
    u = proj[:, :D_S5]
    v = proj[:, D_S5:]
    ub = u.astype(BF16)

    y_parts = []
    for k in range(S5_SPLIT):
        bu_ref[...] = _dot(ub[:, k * U_PER_SPLIT:(k + 1) * U_PER_SPLIT], bbd_ref[k])
        lanes = slice(k * ST_PER_SPLIT, (k + 1) * ST_PER_SPLIT)
        a_re = jnp.broadcast_to(are_ref[:, lanes], (SUBLANES, ST_PER_SPLIT))
        a_im = jnp.broadcast_to(aim_ref[:, lanes], (SUBLANES, ST_PER_SPLIT))

        def two_steps(i, carry, a_re=a_re, a_im=a_im):
            h_re, h_im = carry
            r0 = pl.multiple_of(i * (2 * SUBLANES), 2 * SUBLANES)
            res, ims = [], []
            for s in range(2):
                rs = pl.ds(r0 + s * SUBLANES, SUBLANES)
                n_re = a_re * h_re - a_im * h_im + bu_ref[rs, :ST_PER_SPLIT]
                n_im = a_re * h_im + a_im * h_re + bu_ref[rs, ST_PER_SPLIT:]
                h_re, h_im = n_re, n_im
                res.append(h_re)
                ims.append(h_im)
            pair = pl.ds(r0, 2 * SUBLANES)
            h_ref[pair, :ST_PER_SPLIT] = jnp.concatenate(res, axis=0).astype(BF16)
            h_ref[pair, ST_PER_SPLIT:] = jnp.concatenate(ims, axis=0).astype(BF16)
            return h_re, h_im

        h_re, h_im = lax.fori_loop(0, tc // 2, two_steps, (sre_ref[:, lanes], sim_ref[:, lanes]))
        sre_ref[:, lanes] = h_re
        sim_ref[:, lanes] = h_im
        y_parts.append(_dot(h_ref[...], cbd_ref[k]))
    y_s5 = _s5_post(jnp.concatenate(y_parts, axis=1), u, dskip_ref, wglu_ref)

    halo = POOL_BUF * SUBLANES
    vfull = jnp.concatenate([pool_ref[...], v], axis=0)
    pool_ref[...] = vfull[rows:, :]
    t_idx = step * tc + (lax.broadcasted_iota(jnp.int32, (rows, POOL_CH), 0) >> 3)
    pooled = []
    for gi, w in enumerate(POOL_WINDOWS):
        s = vfull[:, gi * POOL_CH:(gi + 1) * POOL_CH]
        span = 1
        while span < w:
            sh = span * SUBLANES
            s = s[sh:, :] + s[:-sh, :]
            span *= 2
        first = (POOL_BUF - (w - 1)) * SUBLANES
        wsum = s[first:first + rows, :]
        cnt = jnp.minimum(t_idx + 1, w).astype(F32)
        pooled.append(wsum / cnt - v[:, gi * POOL_CH:(gi + 1) * POOL_CH])
    y_pool = _pool_project(pooled, wpool_ref, pscale_ref)

    x1 = x + _dot(y_s5.astype(BF16), wout_ref[:D_S5, :]) + _dot(y_pool.astype(BF16), wout_ref[D_S5:, :])

    h2b = _rms(x1, g2_ref[...]).astype(BF16)
    taps = (CONV_W - 1) * SUBLANES
    acc = None
    for j in range(N_FF_CHUNKS):
        convd = []
        for base in (0, D_FF):
            cols = slice(base + j * FF_CHUNK, base + (j + 1) * FF_CHUNK)
            hup = _dot(h2b, wup_ref[:, cols])
            full = jnp.concatenate([conv_ref[:, cols], hup], axis=0)
            conv_ref[:, cols] = hup[rows - taps:, :]
            c = cb_ref[:, cols]
            for kk in range(CONV_W):
                c = c + cw_ref[kk:kk + 1, cols] * full[kk * SUBLANES:kk * SUBLANES + rows, :]
            convd.append(c)
        act = (_gelu(convd[0]) * convd[1]).astype(BF16)
        part = _dot(act, wdown_ref[j * FF_CHUNK:(j + 1) * FF_CHUNK, :])
        acc = part if acc is None else acc + part
    y_ref[...] = _rms(x1 + acc, gf_ref[...])


def _sample_kernel(x_ref, sre_in, sim_in, pool_in, conv_in,
                   g1_ref, win_ref, are_ref, aim_ref, bbd_ref, cbd_ref, dskip_ref,
                   wglu_ref, wpool_ref, pscale_ref, wout_ref, g2_ref, wup_ref, cw_ref,
                   cb_ref, wdown_ref, gf_ref,
                   y_ref, sre_ref, sim_ref, pool_ref, conv_ref):
    x = x_ref[...]
    hb = _rms(x, g1_ref[...]).astype(BF16)
    proj = _dot(hb, win_ref[...])
    u = proj[:, :D_S5]
    v = proj[:, D_S5:]
    ub = u.astype(BF16)

    y_parts = []
    for k in range(S5_SPLIT):
        bu = _dot(ub[:, k * U_PER_SPLIT:(k + 1) * U_PER_SPLIT], bbd_ref[k])
        lanes = slice(k * ST_PER_SPLIT, (k + 1) * ST_PER_SPLIT)
        a_re = are_ref[:, lanes]
        a_im = aim_ref[:, lanes]
        h_re0 = sre_in[:, lanes]
        h_im0 = sim_in[:, lanes]
        h_re = a_re * h_re0 - a_im * h_im0 + bu[:, :ST_PER_SPLIT]
        h_im = a_re * h_im0 + a_im * h_re0 + bu[:, ST_PER_SPLIT:]
        sre_ref[:, lanes] = h_re
        sim_ref[:, lanes] = h_im
        hcat = jnp.concatenate([h_re, h_im], axis=1).astype(BF16)
        y_parts.append(_dot(hcat, cbd_ref[k]))
    y_s5 = _s5_post(jnp.concatenate(y_parts, axis=1), u, dskip_ref, wglu_ref)

    pooled = []
    for gi, w in enumerate(POOL_WINDOWS):
        vc = v[:, gi * POOL_CH:(gi + 1) * POOL_CH]
        wsum = vc
        for back in range(1, w):
            c0 = (POOL_BUF - back) * D_POOL + gi * POOL_CH
            wsum = wsum + pool_in[:, c0:c0 + POOL_CH]
        pooled.append(wsum / float(w) - vc)
    y_pool = _pool_project(pooled, wpool_ref, pscale_ref)
    pool_ref[:, :(POOL_BUF - 1) * D_POOL] = pool_in[:, D_POOL:]
    pool_ref[:, (POOL_BUF - 1) * D_POOL:] = v

    x1 = x + _dot(y_s5.astype(BF16), wout_ref[:D_S5, :]) + _dot(y_pool.astype(BF16), wout_ref[D_S5:, :])

    h2b = _rms(x1, g2_ref[...]).astype(BF16)
    acc = None
    for j in range(N_FF_CHUNKS):
        convd = []
        for base in (0, D_FF):
            cols = slice(base + j * FF_CHUNK, base + (j + 1) * FF_CHUNK)
            hup = _dot(h2b, wup_ref[:, cols])
            older = conv_in[:, cols]
            newer = conv_in[:, 2 * D_FF + cols.start:2 * D_FF + cols.stop]
            conv_ref[:, cols] = newer
            conv_ref[:, 2 * D_FF + cols.start:2 * D_FF + cols.stop] = hup
            convd.append(cb_ref[:, cols] + cw_ref[0:1, cols] * older
                         + cw_ref[1:2, cols] * newer + cw_ref[2:3, cols] * hup)
        act = (_gelu(convd[0]) * convd[1]).astype(BF16)
        part = _dot(act, wdown_ref[j * FF_CHUNK:(j + 1) * FF_CHUNK, :])
        acc = part if acc is None else acc + part
    y_ref[...] = _rms(x1 + acc, gf_ref[...])


def _s5_tables(a_re, a_im, log_dt, b_re, b_im, c_re, c_im):
    a = lax.complex(a_re, a_im)
    dt_a = jnp.exp(log_dt)[:, None] * a
    a_bar = jnp.exp(dt_a)
    b_bar = ((a_bar - 1.0) / a)[:, :, None] * lax.complex(b_re, b_im)
    eye = jnp.eye(GROUPS_PER_SPLIT, dtype=F32)

    def in_table(b):
        b = b.reshape(S5_SPLIT, GROUPS_PER_SPLIT, S5_STATE, S5_CH)
        t = jnp.einsum('ab,kaph->kahbp', eye, b)
        return t.reshape(S5_SPLIT, U_PER_SPLIT, ST_PER_SPLIT)

    def out_table(c):
        c = c.reshape(S5_SPLIT, GROUPS_PER_SPLIT, S5_CH, S5_STATE)
        t = jnp.einsum('ab,kahp->kapbh', eye, c)
        return t.reshape(S5_SPLIT, ST_PER_SPLIT, U_PER_SPLIT)

    bbd = jnp.concatenate([in_table(jnp.real(b_bar)), in_table(jnp.imag(b_bar))], axis=2)
    cbd = jnp.concatenate([out_table(c_re), out_table(-c_im)], axis=1)
    return (jnp.real(a_bar).reshape(1, -1), jnp.imag(a_bar).reshape(1, -1),
            bbd.astype(BF16), cbd.astype(BF16))


def _vmem_spec():
    return pl.BlockSpec(memory_space=pltpu.VMEM)


PROMPT_TC = 64
VMEM_LIMIT_BYTES = 60 * 1024 * 1024


def kernel(x_prompt, x_sample, state_s5_re, state_s5_im, state_pool, state_ffn_conv, norm_mix_g, w_in, s5_a_re, s5_a_im, s5_log_dt, s5_b_re, s5_b_im, s5_c_re, s5_c_im, s5_d, s5_w_glu, pool_w, pool_scale, w_out, norm_ffn_g, ffn_w_up, ffn_conv_w, ffn_conv_b, ffn_w_down, norm_final_g):
    nb, seq, _ = x_prompt.shape
    ns = x_sample.shape[0]
    assert nb == SUBLANES and seq % PROMPT_TC == 0 and x_sample.shape[1] == 1
    assert norm_mix_g.shape[0] == 1, "single layer"

    a_re, a_im, bbd, cbd = _s5_tables(s5_a_re[0], s5_a_im[0], s5_log_dt[0], s5_b_re[0],
                                      s5_b_im[0], s5_c_re[0], s5_c_im[0])
    weights = (
        norm_mix_g[0].reshape(1, D_MODEL), w_in[0].astype(BF16), a_re, a_im, bbd, cbd,
        s5_d[0].reshape(1, D_S5), s5_w_glu[0].astype(BF16), pool_w[0].astype(BF16),
        pool_scale[0].reshape(1, D_POOL), w_out[0].astype(BF16),
        norm_ffn_g[0].reshape(1, D_MODEL), ffn_w_up[0].astype(BF16), ffn_conv_w[0],
        ffn_conv_b[0].reshape(1, 2 * D_FF), ffn_w_down[0].astype(BF16),
        norm_final_g.reshape(1, D_MODEL),
    )
    n_states = S5_GROUPS * S5_STATE
    cparams = dict(vmem_limit_bytes=VMEM_LIMIT_BYTES)

    tc = PROMPT_TC
    rows = tc * SUBLANES
    xt = jnp.transpose(x_prompt, (1, 0, 2)).reshape(seq * nb, D_MODEL)
    const = lambda i: (0, 0)
    yt, p_re, p_im, p_pool, p_conv = pl.pallas_call(
        functools.partial(_prompt_kernel, tc),
        grid=(seq // tc,),
        in_specs=[pl.BlockSpec((rows, D_MODEL), lambda i: (i, 0))] + [_vmem_spec()] * len(weights),
        out_specs=[
            pl.BlockSpec((rows, D_MODEL), lambda i: (i, 0)),
            pl.BlockSpec((SUBLANES, n_states), const),
            pl.BlockSpec((SUBLANES, n_states), const),
            pl.BlockSpec((POOL_BUF * SUBLANES, D_POOL), const),
            pl.BlockSpec(((CONV_W - 1) * SUBLANES, 2 * D_FF), const),
        ],
        out_shape=[
            jax.ShapeDtypeStruct((seq * nb, D_MODEL), F32),
            jax.ShapeDtypeStruct((SUBLANES, n_states), F32),
            jax.ShapeDtypeStruct((SUBLANES, n_states), F32),
            jax.ShapeDtypeStruct((POOL_BUF * SUBLANES, D_POOL), F32),
            jax.ShapeDtypeStruct(((CONV_W - 1) * SUBLANES, 2 * D_FF), F32),
        ],
        scratch_shapes=[
            pltpu.VMEM((rows, 2 * ST_PER_SPLIT), F32),
            pltpu.VMEM((rows, 2 * ST_PER_SPLIT), BF16),
        ],
        compiler_params=pltpu.CompilerParams(dimension_semantics=("arbitrary",), **cparams),
        name="prompt_layer",
    )(xt, *weights)
    y_prompt = jnp.transpose(yt.reshape(seq, nb, D_MODEL), (1, 0, 2))
    new_pool_p = jnp.transpose(p_pool.reshape(POOL_BUF, nb, D_POOL), (1, 0, 2))[None]
    new_conv_p = jnp.transpose(p_conv.reshape(CONV_W - 1, nb, 2 * D_FF), (1, 0, 2))[None]
    new_re_p = p_re.reshape(1, nb, S5_GROUPS, S5_STATE)
    new_im_p = p_im.reshape(1, nb, S5_GROUPS, S5_STATE)

    ys, s_re, s_im, s_pool, s_conv = pl.pallas_call(
        _sample_kernel,
        in_specs=[_vmem_spec()] * (5 + len(weights)),
        out_specs=[_vmem_spec()] * 5,
        out_shape=[
            jax.ShapeDtypeStruct((ns, D_MODEL), F32),
            jax.ShapeDtypeStruct((ns, n_states), F32),
            jax.ShapeDtypeStruct((ns, n_states), F32),
            jax.ShapeDtypeStruct((ns, POOL_BUF * D_POOL), F32),
            jax.ShapeDtypeStruct((ns, (CONV_W - 1) * 2 * D_FF), F32),
        ],
        compiler_params=pltpu.CompilerParams(**cparams),
        name="sample_layer",
    )(x_sample.reshape(ns, D_MODEL), state_s5_re[0].reshape(ns, n_states),
      state_s5_im[0].reshape(ns, n_states), state_pool[0].reshape(ns, POOL_BUF * D_POOL),
      state_ffn_conv[0].reshape(ns, (CONV_W - 1) * 2 * D_FF), *weights)

    return (y_prompt, ys.reshape(ns, 1, D_MODEL), new_re_p, new_im_p, new_pool_p, new_conv_p,
            s_re.reshape(1, ns, S5_GROUPS, S5_STATE), s_im.reshape(1, ns, S5_GROUPS, S5_STATE),
            s_pool.reshape(1, ns, POOL_BUF, D_POOL), s_conv.reshape(1, ns, CONV_W - 1, 2 * D_FF))
```

```python
import functools
import math

import numpy as np
import jax
import jax.numpy as jnp
from jax import lax
from jax.experimental import pallas as pl
from jax.experimental.pallas import tpu as pltpu

D_MODEL = 1024
D_S5 = 512
S5_CH = 16
S5_GROUPS = 32
S5_STATE = 64
D_POOL = 512
POOL_WINDOWS = (2, 4, 8, 16)
POOL_CH = 128
POOL_BUF = 15
D_FF = 2816
CONV_W = 3
EPS = 1e-6

SUBLANES = 8
S5_SPLIT = 2
GROUPS_PER_SPLIT = S5_GROUPS // S5_SPLIT
U_PER_SPLIT = GROUPS_PER_SPLIT * S5_CH
ST_PER_SPLIT = GROUPS_PER_SPLIT * S5_STATE
FF_CHUNK = 256
N_FF_CHUNKS = D_FF // FF_CHUNK
SQRT_HALF = float(np.sqrt(0.5).astype(np.float32))

BF16 = jnp.bfloat16
F32 = jnp.float32


def _dot(a, b):
    return jnp.dot(a, b, preferred_element_type=F32)


def _rms(x, g):
    ms = jnp.mean(x * x, axis=-1, keepdims=True)
    return x * lax.rsqrt(ms + EPS) * g


def _gelu(x):
    return 0.5 * x * (1.0 + lax.erf(x * SQRT_HALF))


def _s5_post(y_lin, u, dskip_ref, wglu_ref):
    y = _gelu(y_lin + dskip_ref[...] * u)
    return y * jax.nn.sigmoid(_dot(y.astype(BF16), wglu_ref[...]))


def _pool_project(pooled_cols, wpool_ref, pscale_ref):
    outs = []
    for gi in range(len(POOL_WINDOWS)):
        z = _dot(pooled_cols[gi].astype(BF16), wpool_ref[gi])
        outs.append(z * pscale_ref[:, gi * POOL_CH:(gi + 1) * POOL_CH])
    return jnp.concatenate(outs, axis=1)


def _block_copies(hbm_ref, buf_ref, sem_ref, block, slot, tc, to_hbm):
    copies = []
    for n in range(SUBLANES):
        hbm = hbm_ref.at[n, pl.ds(block * tc, tc), :]
        vmem = buf_ref.at[slot, :, n, :]
        src, dst = (vmem, hbm) if to_hbm else (hbm, vmem)
        copies.append(pltpu.make_async_copy(src, dst, sem_ref.at[slot]))
    return copies


def _prompt_kernel(tc, n_steps,
                   x_hbm, g1_ref, win_ref, are_ref, aim_ref, bbd_ref, cbd_ref, dskip_ref,
                   wglu_ref, wpool_ref, pscale_ref, wout_ref, g2_ref, wup_ref, cw_ref,
                   cb_ref, wdown_ref, gf_ref,
                   y_hbm, sre_ref, sim_ref, pool_ref, conv_ref,
                   bu_ref, h_ref, xbuf, ybuf, sem_in, sem_out):
    rows = tc * SUBLANES
    step = pl.program_id(0)
    slot = step % 2

    @pl.when(step == 0)
    def _():
        sre_ref[...] = jnp.zeros_like(sre_ref)
        sim_ref[...] = jnp.zeros_like(sim_ref)
        pool_ref[...] = jnp.zeros_like(pool_ref)
        conv_ref[...] = jnp.zeros_like(conv_ref)
        for cp in _block_copies(x_hbm, xbuf, sem_in, 0, 0, tc, False):
            cp.start()

    @pl.when(step + 1 < n_steps)
    def _():
        for cp in _block_copies(x_hbm, xbuf, sem_in, step + 1, 1 - slot, tc, False):
            cp.start()

    for cp in _block_copies(x_hbm, xbuf, sem_in, step, slot, tc, False):
        cp.wait()

    x = xbuf[slot].reshape(rows, D_MODEL)
    hb = _rms(x, g1_ref[...]).astype(BF16)
    proj = _dot(hb, win_ref[...])
    u = proj[:, :D_S5]
    v = proj[:, D_S5:]
    ub = u.astype(BF16)

    y_parts = []
    for k in range(S5_SPLIT):
        bu_ref[...] = _dot(ub[:, k * U_PER_SPLIT:(k + 1) * U_PER_SPLIT], bbd_ref[k])
        lanes = slice(k * ST_PER_SPLIT, (k + 1) * ST_PER_SPLIT)
        a_re = jnp.broadcast_to(are_ref[:, lanes], (SUBLANES, ST_PER_SPLIT))
        a_im = jnp.broadcast_to(aim_ref[:, lanes], (SUBLANES, ST_PER_SPLIT))

        def two_steps(i, carry, a_re=a_re, a_im=a_im):
            h_re, h_im = carry
            r0 = pl.multiple_of(i * (2 * SUBLANES), 2 * SUBLANES)
            res, ims = [], []
            for s in range(2):
                rs = pl.ds(r0 + s * SUBLANES, SUBLANES)
                n_re = a_re * h_re - a_im * h_im + bu_ref[rs, :ST_PER_SPLIT]
                n_im = a_re * h_im + a_im * h_re + bu_ref[rs, ST_PER_SPLIT:]
                h_re, h_im = n_re, n_im
                res.append(h_re)
                ims.append(h_im)
            pair = pl.ds(r0, 2 * SUBLANES)
            h_ref[pair, :ST_PER_SPLIT] = jnp.concatenate(res, axis=0).astype(BF16)
            h_ref[pair, ST_PER_SPLIT:] = jnp.concatenate(ims, axis=0).astype(BF16)
            return h_re, h_im

        h_re, h_im = lax.fori_loop(0, tc // 2, two_steps, (sre_ref[:, lanes], sim_ref[:, lanes]))
        sre_ref[:, lanes] = h_re
        sim_ref[:, lanes] = h_im
        y_parts.append(_dot(h_ref[...], cbd_ref[k]))
    y_s5 = _s5_post(jnp.concatenate(y_parts, axis=1), u, dskip_ref, wglu_ref)

    halo = POOL_BUF * SUBLANES
    vfull = jnp.concatenate([pool_ref[...], v], axis=0)
    pool_ref[...] = vfull[rows:, :]
    t_idx = step * tc + (lax.broadcasted_iota(jnp.int32, (rows, POOL_CH), 0) >> 3)
    pooled = []
    for gi, w in enumerate(POOL_WINDOWS):
        s = vfull[:, gi * POOL_CH:(gi + 1) * POOL_CH]
        span = 1
        while span < w:
            sh = span * SUBLANES
            s = s[sh:, :] + s[:-sh, :]
            span *= 2
        first = (POOL_BUF - (w - 1)) * SUBLANES
        wsum = s[first:first + rows, :]
        cnt = jnp.minimum(t_idx + 1, w).astype(F32)
        pooled.append(wsum / cnt - v[:, gi * POOL_CH:(gi + 1) * POOL_CH])
    y_pool = _pool_project(pooled, wpool_ref, pscale_ref)

    x1 = x + _dot(y_s5.astype(BF16), wout_ref[:D_S5, :]) + _dot(y_pool.astype(BF16), wout_ref[D_S5:, :])

    h2b = _rms(x1, g2_ref[...]).astype(BF16)
    taps = (CONV_W - 1) * SUBLANES
    acc = None
    for j in range(N_FF_CHUNKS):
        convd = []
        for base in (0, D_FF):
            cols = slice(base + j * FF_CHUNK, base + (j + 1) * FF_CHUNK)
            hup = _dot(h2b, wup_ref[:, cols])
            full = jnp.concatenate([conv_ref[:, cols], hup], axis=0)
            conv_ref[:, cols] = hup[rows - taps:, :]
            c = cb_ref[:, cols]
            for kk in range(CONV_W):
                c = c + cw_ref[kk:kk + 1, cols] * full[kk * SUBLANES:kk * SUBLANES + rows, :]
            convd.append(c)
        act = (_gelu(convd[0]) * convd[1]).astype(BF16)
        part = _dot(act, wdown_ref[j * FF_CHUNK:(j + 1) * FF_CHUNK, :])
        acc = part if acc is None else acc + part
    y = _rms(x1 + acc, gf_ref[...])

    @pl.when(step >= 2)
    def _():
        for cp in _block_copies(y_hbm, ybuf, sem_out, step - 2, slot, tc, True):
            cp.wait()

    ybuf[slot] = y.reshape(tc, SUBLANES, D_MODEL)
    for cp in _block_copies(y_hbm, ybuf, sem_out, step, slot, tc, True):
        cp.start()

    @pl.when(step == n_steps - 1)
    def _():
        if n_steps >= 2:
            for cp in _block_copies(y_hbm, ybuf, sem_out, step - 1, 1 - slot, tc, True):
                cp.wait()
        for cp in _block_copies(y_hbm, ybuf, sem_out, step, slot, tc, True):
            cp.wait()


def _sample_kernel(x_ref, sre_in, sim_in, pool_in, conv_in,
                   g1_ref, win_ref, are_ref, aim_ref, bbd_ref, cbd_ref, dskip_ref,
                   wglu_ref, wpool_ref, pscale_ref, wout_ref, g2_ref, wup_ref, cw_ref,
                   cb_ref, wdown_ref, gf_ref,
                   y_ref, sre_ref, sim_ref, pool_ref, conv_ref):
    x = x_ref[...]
    hb = _rms(x, g1_ref[...]).astype(BF16)
    proj = _dot(hb, win_ref[...])
    u = proj[:, :D_S5]
    v = proj[:, D_S5:]
    ub = u.astype(BF16)

    y_parts = []
    for k in range(S5_SPLIT):
        bu = _dot(ub[:, k * U_PER_SPLIT:(k + 1) * U_PER_SPLIT], bbd_ref[k])
        lanes = slice(k * ST_PER_SPLIT, (k + 1) * ST_PER_SPLIT)
        a_re = are_ref[:, lanes]
        a_im = aim_ref[:, lanes]
        h_re0 = sre_in[:, lanes]
        h_im0 = sim_in[:, lanes]
        h_re = a_re * h_re0 - a_im * h_im0 + bu[:, :ST_PER_SPLIT]
        h_im = a_re * h_im0 + a_im * h_re0 + bu[:, ST_PER_SPLIT:]
        sre_ref[:, lanes] = h_re
        sim_ref[:, lanes] = h_im
        hcat = jnp.concatenate([h_re, h_im], axis=1).astype(BF16)
        y_parts.append(_dot(hcat, cbd_ref[k]))
    y_s5 = _s5_post(jnp.concatenate(y_parts, axis=1), u, dskip_ref, wglu_ref)

    pooled = []
    for gi, w in enumerate(POOL_WINDOWS):
        vc = v[:, gi * POOL_CH:(gi + 1) * POOL_CH]
        wsum = vc
        for back in range(1, w):
            c0 = (POOL_BUF - back) * D_POOL + gi * POOL_CH
            wsum = wsum + pool_in[:, c0:c0 + POOL_CH]
        pooled.append(wsum / float(w) - vc)
    y_pool = _pool_project(pooled, wpool_ref, pscale_ref)
    pool_ref[:, :(POOL_BUF - 1) * D_POOL] = pool_in[:, D_POOL:]
    pool_ref[:, (POOL_BUF - 1) * D_POOL:] = v

    x1 = x + _dot(y_s5.astype(BF16), wout_ref[:D_S5, :]) + _dot(y_pool.astype(BF16), wout_ref[D_S5:, :])

    h2b = _rms(x1, g2_ref[...]).astype(BF16)
    acc = None
    for j in range(N_FF_CHUNKS):
        convd = []
        for base in (0, D_FF):
            cols = slice(base + j * FF_CHUNK, base + (j + 1) * FF_CHUNK)
            hup = _dot(h2b, wup_ref[:, cols])
            older = conv_in[:, cols]
            newer = conv_in[:, 2 * D_FF + cols.start:2 * D_FF + cols.stop]
            conv_ref[:, cols] = newer
            conv_ref[:, 2 * D_FF + cols.start:2 * D_FF + cols.stop] = hup
            convd.append(cb_ref[:, cols] + cw_ref[0:1, cols] * older
                         + cw_ref[1:2, cols] * newer + cw_ref[2:3, cols] * hup)
        act = (_gelu(convd[0]) * convd[1]).astype(BF16)
        part = _dot(act, wdown_ref[j * FF_CHUNK:(j + 1) * FF_CHUNK, :])
        acc = part if acc is None else acc + part
    y_ref[...] = _rms(x1 + acc, gf_ref[...])


def _s5_tables(a_re, a_im, log_dt, b_re, b_im, c_re, c_im):
    dt = jnp.exp(log_dt)[:, None]
    mag = jnp.exp(dt * a_re)
    abar_re = mag * jnp.cos(dt * a_im)
    abar_im = mag * jnp.sin(dt * a_im)
    nr, ni = abar_re - 1.0, abar_im
    den = a_re * a_re + a_im * a_im
    f_re = ((nr * a_re + ni * a_im) / den)[:, :, None]
    f_im = ((ni * a_re - nr * a_im) / den)[:, :, None]
    bbar_re = f_re * b_re - f_im * b_im
    bbar_im = f_re * b_im + f_im * b_re
    eye = jnp.eye(GROUPS_PER_SPLIT, dtype=F32)

    def in_table(b):
        b = b.reshape(S5_SPLIT, GROUPS_PER_SPLIT, S5_STATE, S5_CH)
        t = jnp.einsum('ab,kaph->kahbp', eye, b)
        return t.reshape(S5_SPLIT, U_PER_SPLIT, ST_PER_SPLIT)

    def out_table(c):
        c = c.reshape(S5_SPLIT, GROUPS_PER_SPLIT, S5_CH, S5_STATE)
        t = jnp.einsum('ab,kahp->kapbh', eye, c)
        return t.reshape(S5_SPLIT, ST_PER_SPLIT, U_PER_SPLIT)

    bbd = jnp.concatenate([in_table(bbar_re), in_table(bbar_im)], axis=2)
    cbd = jnp.concatenate([out_table(c_re), out_table(-c_im)], axis=1)
    return abar_re.reshape(1, -1), abar_im.reshape(1, -1), bbd.astype(BF16), cbd.astype(BF16)


def _vmem_spec():
    return pl.BlockSpec(memory_space=pltpu.VMEM)


PROMPT_TC = 64
VMEM_LIMIT_BYTES = 60 * 1024 * 1024


def kernel(x_prompt, x_sample, state_s5_re, state_s5_im, state_pool, state_ffn_conv, norm_mix_g, w_in, s5_a_re, s5_a_im, s5_log_dt, s5_b_re, s5_b_im, s5_c_re, s5_c_im, s5_d, s5_w_glu, pool_w, pool_scale, w_out, norm_ffn_g, ffn_w_up, ffn_conv_w, ffn_conv_b, ffn_w_down, norm_final_g):
    nb, seq, _ = x_prompt.shape
    ns = x_sample.shape[0]
    assert nb == SUBLANES and seq % PROMPT_TC == 0 and x_sample.shape[1] == 1
    assert norm_mix_g.shape[0] == 1, "single layer"

    a_re, a_im, bbd, cbd = _s5_tables(s5_a_re[0], s5_a_im[0], s5_log_dt[0], s5_b_re[0],
                                      s5_b_im[0], s5_c_re[0], s5_c_im[0])
    weights = (
        norm_mix_g[0].reshape(1, D_MODEL), w_in[0].astype(BF16), a_re, a_im, bbd, cbd,
        s5_d[0].reshape(1, D_S5), s5_w_glu[0].astype(BF16), pool_w[0].astype(BF16),
        pool_scale[0].reshape(1, D_POOL), w_out[0].astype(BF16),
        norm_ffn_g[0].reshape(1, D_MODEL), ffn_w_up[0].astype(BF16), ffn_conv_w[0],
        ffn_conv_b[0].reshape(1, 2 * D_FF), ffn_w_down[0].astype(BF16),
        norm_final_g.reshape(1, D_MODEL),
    )
    n_states = S5_GROUPS * S5_STATE
    cparams = dict(vmem_limit_bytes=VMEM_LIMIT_BYTES)

    tc = PROMPT_TC
    rows = tc * SUBLANES
    const = lambda i: (0, 0)
    y_prompt, p_re, p_im, p_pool, p_conv = pl.pallas_call(
        functools.partial(_prompt_kernel, tc, seq // tc),
        grid=(seq // tc,),
        in_specs=[pl.BlockSpec(memory_space=pl.ANY)] + [_vmem_spec()] * len(weights),
        out_specs=[
            pl.BlockSpec(memory_space=pl.ANY),
            pl.BlockSpec((SUBLANES, n_states), const),
            pl.BlockSpec((SUBLANES, n_states), const),
            pl.BlockSpec((POOL_BUF * SUBLANES, D_POOL), const),
            pl.BlockSpec(((CONV_W - 1) * SUBLANES, 2 * D_FF), const),
        ],
        out_shape=[
            jax.ShapeDtypeStruct((nb, seq, D_MODEL), F32),
            jax.ShapeDtypeStruct((SUBLANES, n_states), F32),
            jax.ShapeDtypeStruct((SUBLANES, n_states), F32),
            jax.ShapeDtypeStruct((POOL_BUF * SUBLANES, D_POOL), F32),
            jax.ShapeDtypeStruct(((CONV_W - 1) * SUBLANES, 2 * D_FF), F32),
        ],
        scratch_shapes=[
            pltpu.VMEM((rows, 2 * ST_PER_SPLIT), F32),
            pltpu.VMEM((rows, 2 * ST_PER_SPLIT), BF16),
            pltpu.VMEM((2, tc, SUBLANES, D_MODEL), F32),
            pltpu.VMEM((2, tc, SUBLANES, D_MODEL), F32),
            pltpu.SemaphoreType.DMA((2,)),
            pltpu.SemaphoreType.DMA((2,)),
        ],
        compiler_params=pltpu.CompilerParams(dimension_semantics=("arbitrary",), **cparams),
        name="prompt_layer",
    )(x_prompt, *weights)
    new_pool_p = jnp.transpose(p_pool.reshape(POOL_BUF, nb, D_POOL), (1, 0, 2))[None]
    new_conv_p = jnp.transpose(p_conv.reshape(CONV_W - 1, nb, 2 * D_FF), (1, 0, 2))[None]
    new_re_p = p_re.reshape(1, nb, S5_GROUPS, S5_STATE)
    new_im_p = p_im.reshape(1, nb, S5_GROUPS, S5_STATE)

    ys, s_re, s_im, s_pool, s_conv = pl.pallas_call(
        _sample_kernel,
        in_specs=[_vmem_spec()] * (5 + len(weights)),
        out_specs=[_vmem_spec()] * 5,
        out_shape=[
            jax.ShapeDtypeStruct((ns, D_MODEL), F32),
            jax.ShapeDtypeStruct((ns, n_states), F32),
            jax.ShapeDtypeStruct((ns, n_states), F32),
            jax.ShapeDtypeStruct((ns, POOL_BUF * D_POOL), F32),
            jax.ShapeDtypeStruct((ns, (CONV_W - 1) * 2 * D_FF), F32),
        ],
        compiler_params=pltpu.CompilerParams(**cparams),
        name="sample_layer",
    )(x_sample.reshape(ns, D_MODEL), state_s5_re[0].reshape(ns, n_states),
      state_s5_im[0].reshape(ns, n_states), state_pool[0].reshape(ns, POOL_BUF * D_POOL),
      state_ffn_conv[0].reshape(ns, (CONV_W - 1) * 2 * D_FF), *weights)

    return (y_prompt, ys.reshape(ns, 1, D_MODEL), new_re_p, new_im_p, new_pool_p, new_conv_p,
            s_re.reshape(1, ns, S5_GROUPS, S5_STATE), s_im.reshape(1, ns, S5_GROUPS, S5_STATE),
            s_pool.reshape(1, ns, POOL_BUF, D_POOL), s_conv.reshape(1, ns, CONV_W - 1, 2 * D_FF))
```

```python
import functools
import math

import numpy as np
import jax
import jax.numpy as jnp
from jax import lax
from jax.experimental import pallas as pl
from jax.experimental.pallas import tpu as pltpu

D_MODEL = 1024
D_S5 = 512
S5_CH = 16
S5_GROUPS = 32
S5_STATE = 64
D_POOL = 512
POOL_WINDOWS = (2, 4, 8, 16)
POOL_CH = 128
POOL_BUF = 15
D_FF = 2816
CONV_W = 3
EPS = 1e-6

SUBLANES = 8
S5_SPLIT = 2
GROUPS_PER_SPLIT = S5_GROUPS // S5_SPLIT
U_PER_SPLIT = GROUPS_PER_SPLIT * S5_CH
ST_PER_SPLIT = GROUPS_PER_SPLIT * S5_STATE
FF_CHUNK = 256
N_FF_CHUNKS = D_FF // FF_CHUNK
SQRT_HALF = float(np.sqrt(0.5).astype(np.float32))

BF16 = jnp.bfloat16
F32 = jnp.float32


def _dot(a, b):
    return jnp.dot(a, b, preferred_element_type=F32)


def _rms(x, g):
    ms = jnp.mean(x * x, axis=-1, keepdims=True)
    return x * lax.rsqrt(ms + EPS) * g


def _gelu(x):
    return 0.5 * x * (1.0 + lax.erf(x * SQRT_HALF))


def _s5_post(y_lin, u, dskip_ref, wglu_ref):
    y = _gelu(y_lin + dskip_ref[...] * u)
    return y * jax.nn.sigmoid(_dot(y.astype(BF16), wglu_ref[...]))


def _pool_project(pooled_cols, wpool_ref, pscale_ref):
    outs = []
    for gi in range(len(POOL_WINDOWS)):
        z = _dot(pooled_cols[gi].astype(BF16), wpool_ref[gi])
        outs.append(z * pscale_ref[:, gi * POOL_CH:(gi + 1) * POOL_CH])
    return jnp.concatenate(outs, axis=1)


def _block_copies(hbm_ref, buf_ref, sem_ref, block, slot, tc, to_hbm):
    copies = []
    for n in range(SUBLANES):
        hbm = hbm_ref.at[n, pl.ds(block * tc, tc), :]
        vmem = buf_ref.at[slot, :, n, :]
        src, dst = (vmem, hbm) if to_hbm else (hbm, vmem)
        copies.append(pltpu.make_async_copy(src, dst, sem_ref.at[slot]))
    return copies


def _prompt_kernel(tc, n_steps,
                   x_hbm, g1_ref, win_ref, are_ref, aim_ref, bbd_ref, cbd_ref, dskip_ref,
                   wglu_ref, wpool_ref, pscale_ref, wout_ref, g2_ref, wup_ref, cw_ref,
                   cb_ref, wdown_ref, gf_ref,
                   y_hbm, sre_ref, sim_ref, pool_ref, conv_ref,
                   bu_ref, h_ref, xbuf, ybuf, sem_in, sem_out):
    rows = tc * SUBLANES
    step = pl.program_id(0)
    slot = step % 2

    @pl.when(step == 0)
    def _():
        sre_ref[...] = jnp.zeros_like(sre_ref)
        sim_ref[...] = jnp.zeros_like(sim_ref)
        pool_ref[...] = jnp.zeros_like(pool_ref)
        conv_ref[...] = jnp.zeros_like(conv_ref)
        for cp in _block_copies(x_hbm, xbuf, sem_in, 0, 0, tc, False):
            cp.start()

    @pl.when(step + 1 < n_steps)
    def _():
        for cp in _block_copies(x_hbm, xbuf, sem_in, step + 1, 1 - slot, tc, False):
            cp.start()

    for cp in _block_copies(x_hbm, xbuf, sem_in, step, slot, tc, False):
        cp.wait()

    x = xbuf[slot].reshape(rows, D_MODEL)
    hb = _rms(x, g1_ref[...]).astype(BF16)
    proj = _dot(hb, win_ref[...])
    u = proj[:, :D_S5]
    v = proj[:, D_S5:]
    ub = u.astype(BF16)

    y_parts = []
    for k in range(S5_SPLIT):
        bu_ref[...] = _dot(ub[:, k * U_PER_SPLIT:(k + 1) * U_PER_SPLIT], bbd_ref[k])
        lanes = slice(k * ST_PER_SPLIT, (k + 1) * ST_PER_SPLIT)
        a_re = jnp.broadcast_to(are_ref[:, lanes], (SUBLANES, ST_PER_SPLIT))
        a_im = jnp.broadcast_to(aim_ref[:, lanes], (SUBLANES, ST_PER_SPLIT))

        def two_steps(i, carry, a_re=a_re, a_im=a_im):
            h_re, h_im = carry
            r0 = pl.multiple_of(i * (2 * SUBLANES), 2 * SUBLANES)
            res, ims = [], []
            for s in range(2):
                rs = pl.ds(r0 + s * SUBLANES, SUBLANES)
                n_re = a_re * h_re - a_im * h_im + bu_ref[rs, :ST_PER_SPLIT]
                n_im = a_re * h_im + a_im * h_re + bu_ref[rs, ST_PER_SPLIT:]
                h_re, h_im = n_re, n_im
                res.append(h_re)
                ims.append(h_im)
            pair = pl.ds(r0, 2 * SUBLANES)
            h_ref[pair, :ST_PER_SPLIT] = jnp.concatenate(res, axis=0).astype(BF16)
            h_ref[pair, ST_PER_SPLIT:] = jnp.concatenate(ims, axis=0).astype(BF16)
            return h_re, h_im

        h_re, h_im = lax.fori_loop(0, tc // 2, two_steps, (sre_ref[:, lanes], sim_ref[:, lanes]),
                                   unroll=True)
        sre_ref[:, lanes] = h_re
        sim_ref[:, lanes] = h_im
        half = rows // 2
        y_parts.append(jnp.concatenate(
            [_dot(h_ref[:half, :], cbd_ref[k]), _dot(h_ref[half:, :], cbd_ref[k])], axis=0))
    y_s5 = _s5_post(jnp.concatenate(y_parts, axis=1), u, dskip_ref, wglu_ref)

    halo = POOL_BUF * SUBLANES
    vfull = jnp.concatenate([pool_ref[...], v], axis=0)
    pool_ref[...] = vfull[rows:, :]
    t_idx = step * tc + (lax.broadcasted_iota(jnp.int32, (rows, POOL_CH), 0) >> 3)
    pooled = []
    for gi, w in enumerate(POOL_WINDOWS):
        s = vfull[:, gi * POOL_CH:(gi + 1) * POOL_CH]
        span = 1
        while span < w:
            sh = span * SUBLANES
            s = s[sh:, :] + s[:-sh, :]
            span *= 2
        first = (POOL_BUF - (w - 1)) * SUBLANES
        wsum = s[first:first + rows, :]
        cnt = jnp.minimum(t_idx + 1, w).astype(F32)
        pooled.append(wsum / cnt - v[:, gi * POOL_CH:(gi + 1) * POOL_CH])
    y_pool = _pool_project(pooled, wpool_ref, pscale_ref)

    x1 = x + _dot(y_s5.astype(BF16), wout_ref[:D_S5, :]) + _dot(y_pool.astype(BF16), wout_ref[D_S5:, :])

    h2b = _rms(x1, g2_ref[...]).astype(BF16)
    taps = (CONV_W - 1) * SUBLANES
    acc = None
    for j in range(N_FF_CHUNKS):
        convd = []
        for base in (0, D_FF):
            cols = slice(base + j * FF_CHUNK, base + (j + 1) * FF_CHUNK)
            hup = _dot(h2b, wup_ref[:, cols])
            full = jnp.concatenate([conv_ref[:, cols], hup], axis=0)
            conv_ref[:, cols] = hup[rows - taps:, :]
            c = cb_ref[:, cols]
            for kk in range(CONV_W):
                c = c + cw_ref[kk:kk + 1, cols] * full[kk * SUBLANES:kk * SUBLANES + rows, :]
            convd.append(c)
        act = (_gelu(convd[0]) * convd[1]).astype(BF16)
        part = _dot(act, wdown_ref[j * FF_CHUNK:(j + 1) * FF_CHUNK, :])
        acc = part if acc is None else acc + part
    y = _rms(x1 + acc, gf_ref[...])

    @pl.when(step >= 2)
    def _():
        for cp in _block_copies(y_hbm, ybuf, sem_out, step - 2, slot, tc, True):
            cp.wait()

    ybuf[slot] = y.reshape(tc, SUBLANES, D_MODEL)
    for cp in _block_copies(y_hbm, ybuf, sem_out, step, slot, tc, True):
        cp.start()

    @pl.when(step == n_steps - 1)
    def _():
        if n_steps >= 2:
            for cp in _block_copies(y_hbm, ybuf, sem_out, step - 1, 1 - slot, tc, True):
                cp.wait()
        for cp in _block_copies(y_hbm, ybuf, sem_out, step, slot, tc, True):
            cp.wait()


def _sample_kernel(x_ref, sre_in, sim_in, pool_in, conv_in,
                   g1_ref, win_ref, are_ref, aim_ref, bbd_ref, cbd_ref, dskip_ref,
                   wglu_ref, wpool_ref, pscale_ref, wout_ref, g2_ref, wup_ref, cw_ref,
                   cb_ref, wdown_ref, gf_ref,
                   y_ref, sre_ref, sim_ref, pool_ref, conv_ref):
    x = x_ref[...]
    hb = _rms(x, g1_ref[...]).astype(BF16)
    proj = _dot(hb, win_ref[...])
    u = proj[:, :D_S5]
    v = proj[:, D_S5:]
    ub = u.astype(BF16)

    y_parts = []
    for k in range(S5_SPLIT):
        bu = _dot(ub[:, k * U_PER_SPLIT:(k + 1) * U_PER_SPLIT], bbd_ref[k])
        lanes = slice(k * ST_PER_SPLIT, (k + 1) * ST_PER_SPLIT)
        a_re = are_ref[:, lanes]
        a_im = aim_ref[:, lanes]
        h_re0 = sre_in[:, lanes]
        h_im0 = sim_in[:, lanes]
        h_re = a_re * h_re0 - a_im * h_im0 + bu[:, :ST_PER_SPLIT]
        h_im = a_re * h_im0 + a_im * h_re0 + bu[:, ST_PER_SPLIT:]
        sre_ref[:, lanes] = h_re
        sim_ref[:, lanes] = h_im
        hcat = jnp.concatenate([h_re, h_im], axis=1).astype(BF16)
        y_parts.append(_dot(hcat, cbd_ref[k]))
    y_s5 = _s5_post(jnp.concatenate(y_parts, axis=1), u, dskip_ref, wglu_ref)

    pooled = []
    for gi, w in enumerate(POOL_WINDOWS):
        vc = v[:, gi * POOL_CH:(gi + 1) * POOL_CH]
        wsum = vc
        for back in range(1, w):
            c0 = (POOL_BUF - back) * D_POOL + gi * POOL_CH
            wsum = wsum + pool_in[:, c0:c0 + POOL_CH]
        pooled.append(wsum / float(w) - vc)
    y_pool = _pool_project(pooled, wpool_ref, pscale_ref)
    pool_ref[:, :(POOL_BUF - 1) * D_POOL] = pool_in[:, D_POOL:]
    pool_ref[:, (POOL_BUF - 1) * D_POOL:] = v

    x1 = x + _dot(y_s5.astype(BF16), wout_ref[:D_S5, :]) + _dot(y_pool.astype(BF16), wout_ref[D_S5:, :])

    h2b = _rms(x1, g2_ref[...]).astype(BF16)
    acc = None
    for j in range(N_FF_CHUNKS):
        convd = []
        for base in (0, D_FF):
            cols = slice(base + j * FF_CHUNK, base + (j + 1) * FF_CHUNK)
            hup = _dot(h2b, wup_ref[:, cols])
            older = conv_in[:, cols]
            newer = conv_in[:, 2 * D_FF + cols.start:2 * D_FF + cols.stop]
            conv_ref[:, cols] = newer
            conv_ref[:, 2 * D_FF + cols.start:2 * D_FF + cols.stop] = hup
            convd.append(cb_ref[:, cols] + cw_ref[0:1, cols] * older
                         + cw_ref[1:2, cols] * newer + cw_ref[2:3, cols] * hup)
        act = (_gelu(convd[0]) * convd[1]).astype(BF16)
        part = _dot(act, wdown_ref[j * FF_CHUNK:(j + 1) * FF_CHUNK, :])
        acc = part if acc is None else acc + part
    y_ref[...] = _rms(x1 + acc, gf_ref[...])


def _s5_tables(a_re, a_im, log_dt, b_re, b_im, c_re, c_im):
    dt = jnp.exp(log_dt)[:, None]
    mag = jnp.exp(dt * a_re)
    abar_re = mag * jnp.cos(dt * a_im)
    abar_im = mag * jnp.sin(dt * a_im)
    nr, ni = abar_re - 1.0, abar_im
    den = a_re * a_re + a_im * a_im
    f_re = ((nr * a_re + ni * a_im) / den)[:, :, None]
    f_im = ((ni * a_re - nr * a_im) / den)[:, :, None]
    bbar_re = f_re * b_re - f_im * b_im
    bbar_im = f_re * b_im + f_im * b_re
    eye = jnp.eye(GROUPS_PER_SPLIT, dtype=F32)

    def in_table(b):
        b = b.reshape(S5_SPLIT, GROUPS_PER_SPLIT, S5_STATE, S5_CH)
        t = jnp.einsum('ab,kaph->kahbp', eye, b)
        return t.reshape(S5_SPLIT, U_PER_SPLIT, ST_PER_SPLIT)

    def out_table(c):
        c = c.reshape(S5_SPLIT, GROUPS_PER_SPLIT, S5_CH, S5_STATE)
        t = jnp.einsum('ab,kahp->kapbh', eye, c)
        return t.reshape(S5_SPLIT, ST_PER_SPLIT, U_PER_SPLIT)

    bbd = jnp.concatenate([in_table(bbar_re), in_table(bbar_im)], axis=2)
    cbd = jnp.concatenate([out_table(c_re), out_table(-c_im)], axis=1)
    return abar_re.reshape(1, -1), abar_im.reshape(1, -1), bbd.astype(BF16), cbd.astype(BF16)


def _vmem_spec():
    return pl.BlockSpec(memory_space=pltpu.VMEM)


PROMPT_TC = 64
VMEM_LIMIT_BYTES = 60 * 1024 * 1024


def kernel(x_prompt, x_sample, state_s5_re, state_s5_im, state_pool, state_ffn_conv, norm_mix_g, w_in, s5_a_re, s5_a_im, s5_log_dt, s5_b_re, s5_b_im, s5_c_re, s5_c_im, s5_d, s5_w_glu, pool_w, pool_scale, w_out, norm_ffn_g, ffn_w_up, ffn_conv_w, ffn_conv_b, ffn_w_down, norm_final_g):
    nb, seq, _ = x_prompt.shape
    ns = x_sample.shape[0]
    assert nb == SUBLANES and seq % PROMPT_TC == 0 and x_sample.shape[1] == 1
    assert norm_mix_g.shape[0] == 1, "single layer"

    a_re, a_im, bbd, cbd = _s5_tables(s5_a_re[0], s5_a_im[0], s5_log_dt[0], s5_b_re[0],
                                      s5_b_im[0], s5_c_re[0], s5_c_im[0])
    weights = (
        norm_mix_g[0].reshape(1, D_MODEL), w_in[0].astype(BF16), a_re, a_im, bbd, cbd,
        s5_d[0].reshape(1, D_S5), s5_w_glu[0].astype(BF16), pool_w[0].astype(BF16),
        pool_scale[0].reshape(1, D_POOL), w_out[0].astype(BF16),
        norm_ffn_g[0].reshape(1, D_MODEL), ffn_w_up[0].astype(BF16), ffn_conv_w[0],
        ffn_conv_b[0].reshape(1, 2 * D_FF), ffn_w_down[0].astype(BF16),
        norm_final_g.reshape(1, D_MODEL),
    )
    n_states = S5_GROUPS * S5_STATE
    cparams = dict(vmem_limit_bytes=VMEM_LIMIT_BYTES)

    tc = PROMPT_TC
    rows = tc * SUBLANES
    const = lambda i: (0, 0)
    y_prompt, p_re, p_im, p_pool, p_conv = pl.pallas_call(
        functools.partial(_prompt_kernel, tc, seq // tc),
        grid=(seq // tc,),
        in_specs=[pl.BlockSpec(memory_space=pl.ANY)] + [_vmem_spec()] * len(weights),
        out_specs=[
            pl.BlockSpec(memory_space=pl.ANY),
            pl.BlockSpec((SUBLANES, n_states), const),
            pl.BlockSpec((SUBLANES, n_states), const),
            pl.BlockSpec((POOL_BUF * SUBLANES, D_POOL), const),
            pl.BlockSpec(((CONV_W - 1) * SUBLANES, 2 * D_FF), const),
        ],
        out_shape=[
            jax.ShapeDtypeStruct((nb, seq, D_MODEL), F32),
            jax.ShapeDtypeStruct((SUBLANES, n_states), F32),
            jax.ShapeDtypeStruct((SUBLANES, n_states), F32),
            jax.ShapeDtypeStruct((POOL_BUF * SUBLANES, D_POOL), F32),
            jax.ShapeDtypeStruct(((CONV_W - 1) * SUBLANES, 2 * D_FF), F32),
        ],
        scratch_shapes=[
            pltpu.VMEM((rows, 2 * ST_PER_SPLIT), F32),
            pltpu.VMEM((rows, 2 * ST_PER_SPLIT), BF16),
            pltpu.VMEM((2, tc, SUBLANES, D_MODEL), F32),
            pltpu.VMEM((2, tc, SUBLANES, D_MODEL), F32),
            pltpu.SemaphoreType.DMA((2,)),
            pltpu.SemaphoreType.DMA((2,)),
        ],
        compiler_params=pltpu.CompilerParams(dimension_semantics=("arbitrary",), **cparams),
        name="prompt_layer",
    )(x_prompt, *weights)
    new_pool_p = jnp.transpose(p_pool.reshape(POOL_BUF, nb, D_POOL), (1, 0, 2))[None]
    new_conv_p = jnp.transpose(p_conv.reshape(CONV_W - 1, nb, 2 * D_FF), (1, 0, 2))[None]
    new_re_p = p_re.reshape(1, nb, S5_GROUPS, S5_STATE)
    new_im_p = p_im.reshape(1, nb, S5_GROUPS, S5_STATE)

    ys, s_re, s_im, s_pool, s_conv = pl.pallas_call(
        _sample_kernel,
        in_specs=[_vmem_spec()] * (5 + len(weights)),
        out_specs=[_vmem_spec()] * 5,
        out_shape=[
            jax.ShapeDtypeStruct((ns, D_MODEL), F32),
            jax.ShapeDtypeStruct((ns, n_states), F32),
            jax.ShapeDtypeStruct((ns, n_states), F32),
            jax.ShapeDtypeStruct((ns, POOL_BUF * D_POOL), F32),
            jax.ShapeDtypeStruct((ns, (CONV_W - 1) * 2 * D_FF), F32),
        ],
        compiler_params=pltpu.CompilerParams(**cparams),
        name="sample_layer",
    )(x_sample.reshape(ns, D_MODEL), state_s5_re[0].reshape(ns, n_states),
      state_s5_im[0].reshape(ns, n_states), state_pool[0].reshape(ns, POOL_BUF * D_POOL),
      state_ffn_conv[0].reshape(ns, (CONV_W - 1) * 2 * D_FF), *weights)

    return (y_prompt, ys.reshape(ns, 1, D_MODEL), new_re_p, new_im_p, new_pool_p, new_conv_p,
            s_re.reshape(1, ns, S5_GROUPS, S5_STATE), s_im.reshape(1, ns, S5_GROUPS, S5_STATE),
            s_pool.reshape(1, ns, POOL_BUF, D_POOL), s_conv.reshape(1, ns, CONV_W - 1, 2 * D_FF))
```

```python
import functools
import math

import numpy as np
import jax
import jax.numpy as jnp
from jax import lax
from jax.experimental import pallas as pl
from jax.experimental.pallas import tpu as pltpu

D_MODEL = 1024
D_S5 = 512
S5_CH = 16
S5_GROUPS = 32
S5_STATE = 64
D_POOL = 512
POOL_WINDOWS = (2, 4, 8, 16)
POOL_CH = 128
POOL_BUF = 15
D_FF = 2816
CONV_W = 3
EPS = 1e-6

SUBLANES = 8
S5_SPLIT = 2
GROUPS_PER_SPLIT = S5_GROUPS // S5_SPLIT
U_PER_SPLIT = GROUPS_PER_SPLIT * S5_CH
ST_PER_SPLIT = GROUPS_PER_SPLIT * S5_STATE
FF_CHUNK = 256
N_FF_CHUNKS = D_FF // FF_CHUNK
SQRT_HALF = float(np.sqrt(0.5).astype(np.float32))

BF16 = jnp.bfloat16
F32 = jnp.float32


def _dot(a, b):
    return jnp.dot(a, b, preferred_element_type=F32)


def _rms(x, g):
    ms = jnp.mean(x * x, axis=-1, keepdims=True)
    return x * lax.rsqrt(ms + EPS) * g


def _gelu(x):
    return 0.5 * x * (1.0 + lax.erf(x * SQRT_HALF))


def _s5_post(y_lin, u, dskip_ref, wglu_ref):
    y = _gelu(y_lin + dskip_ref[...] * u)
    return y * jax.nn.sigmoid(_dot(y.astype(BF16), wglu_ref[...]))


def _pool_project(pooled_cols, wpool_ref, pscale_ref):
    outs = []
    for gi in range(len(POOL_WINDOWS)):
        z = _dot(pooled_cols[gi].astype(BF16), wpool_ref[gi * POOL_CH:(gi + 1) * POOL_CH, :])
        outs.append(z * pscale_ref[:, gi * POOL_CH:(gi + 1) * POOL_CH])
    return jnp.concatenate(outs, axis=1)


def _block_copies(hbm_ref, buf_ref, sem_ref, block, slot, tc, to_hbm):
    copies = []
    for n in range(SUBLANES):
        hbm = hbm_ref.at[n, pl.ds(block * tc, tc), :]
        vmem = buf_ref.at[slot, :, n, :]
        src, dst = (vmem, hbm) if to_hbm else (hbm, vmem)
        copies.append(pltpu.make_async_copy(src, dst, sem_ref.at[slot]))
    return copies


def _prompt_kernel(tc, n_steps,
                   x_hbm, g1_ref, win_ref, are_ref, aim_ref, bbd_ref, cbd_ref, dskip_ref,
                   wglu_ref, wpool_ref, pscale_ref, wout_ref, g2_ref, wup_ref, cw_ref,
                   cb_ref, wdown_ref, gf_ref,
                   y_hbm, sre_ref, sim_ref, pool_ref, conv_ref,
                   bu_ref, h_ref, act_ref, xbuf, ybuf, sem_in, sem_out):
    rows = tc * SUBLANES
    step = pl.program_id(0)
    slot = step % 2

    @pl.when(step == 0)
    def _():
        sre_ref[...] = jnp.zeros_like(sre_ref)
        sim_ref[...] = jnp.zeros_like(sim_ref)
        pool_ref[...] = jnp.zeros_like(pool_ref)
        conv_ref[...] = jnp.zeros_like(conv_ref)
        for cp in _block_copies(x_hbm, xbuf, sem_in, 0, 0, tc, False):
            cp.start()

    @pl.when(step + 1 < n_steps)
    def _():
        for cp in _block_copies(x_hbm, xbuf, sem_in, step + 1, 1 - slot, tc, False):
            cp.start()

    for cp in _block_copies(x_hbm, xbuf, sem_in, step, slot, tc, False):
        cp.wait()

    x = xbuf[slot].reshape(rows, D_MODEL)
    hb = _rms(x, g1_ref[...]).astype(BF16)
    proj = _dot(hb, win_ref[...])
    u = proj[:, :D_S5]
    v = proj[:, D_S5:]
    ub = u.astype(BF16)

    y_parts = []
    for k in range(S5_SPLIT):
        bu_ref[...] = _dot(ub[:, k * U_PER_SPLIT:(k + 1) * U_PER_SPLIT], bbd_ref[k])
        lanes = slice(k * ST_PER_SPLIT, (k + 1) * ST_PER_SPLIT)
        a_re = jnp.broadcast_to(are_ref[:, lanes], (SUBLANES, ST_PER_SPLIT))
        a_im = jnp.broadcast_to(aim_ref[:, lanes], (SUBLANES, ST_PER_SPLIT))

        def two_steps(i, carry, a_re=a_re, a_im=a_im):
            h_re, h_im = carry
            r0 = pl.multiple_of(i * (2 * SUBLANES), 2 * SUBLANES)
            res, ims = [], []
            for s in range(2):
                rs = pl.ds(r0 + s * SUBLANES, SUBLANES)
                n_re = a_re * h_re - a_im * h_im + bu_ref[rs, :ST_PER_SPLIT]
                n_im = a_re * h_im + a_im * h_re + bu_ref[rs, ST_PER_SPLIT:]
                h_re, h_im = n_re, n_im
                res.append(h_re)
                ims.append(h_im)
            pair = pl.ds(r0, 2 * SUBLANES)
            h_ref[pair, :ST_PER_SPLIT] = jnp.concatenate(res, axis=0).astype(BF16)
            h_ref[pair, ST_PER_SPLIT:] = jnp.concatenate(ims, axis=0).astype(BF16)
            return h_re, h_im

        h_re, h_im = lax.fori_loop(0, tc // 2, two_steps, (sre_ref[:, lanes], sim_ref[:, lanes]),
                                   unroll=True)
        sre_ref[:, lanes] = h_re
        sim_ref[:, lanes] = h_im
        half = rows // 2
        y_parts.append(jnp.concatenate(
            [_dot(h_ref[:half, :], cbd_ref[k]), _dot(h_ref[half:, :], cbd_ref[k])], axis=0))
    y_s5 = _s5_post(jnp.concatenate(y_parts, axis=1), u, dskip_ref, wglu_ref)

    halo = POOL_BUF * SUBLANES
    vfull = jnp.concatenate([pool_ref[...], v], axis=0)
    pool_ref[...] = vfull[rows:, :]
    t_idx = step * tc + (lax.broadcasted_iota(jnp.int32, (rows, POOL_CH), 0) >> 3)
    pooled = []
    for gi, w in enumerate(POOL_WINDOWS):
        s = vfull[:, gi * POOL_CH:(gi + 1) * POOL_CH]
        span = 1
        while span < w:
            sh = span * SUBLANES
            s = s[sh:, :] + s[:-sh, :]
            span *= 2
        first = (POOL_BUF - (w - 1)) * SUBLANES
        wsum = s[first:first + rows, :]
        cnt = jnp.minimum(t_idx + 1, w).astype(F32)
        pooled.append(wsum / cnt - v[:, gi * POOL_CH:(gi + 1) * POOL_CH])
    y_pool = _pool_project(pooled, wpool_ref, pscale_ref)

    x1 = x + _dot(y_s5.astype(BF16), wout_ref[:D_S5, :]) + _dot(y_pool.astype(BF16), wout_ref[D_S5:, :])

    h2b = _rms(x1, g2_ref[...]).astype(BF16)
    taps = (CONV_W - 1) * SUBLANES
    def ff_cols(j):
        return [slice(base + j * FF_CHUNK, base + (j + 1) * FF_CHUNK) for base in (0, D_FF)]

    def up_project(j):
        return [_dot(h2b, wup_ref[:, cols]) for cols in ff_cols(j)]

    acc = None
    hups = up_project(0)
    for j in range(N_FF_CHUNKS):
        nxt = up_project(j + 1) if j + 1 < N_FF_CHUNKS else None
        convd = []
        for hup, cols in zip(hups, ff_cols(j)):
            full = jnp.concatenate([conv_ref[:, cols], hup], axis=0)
            conv_ref[:, cols] = hup[rows - taps:, :]
            c = cb_ref[:, cols]
            for kk in range(CONV_W):
                c = c + cw_ref[kk:kk + 1, cols] * full[kk * SUBLANES:kk * SUBLANES + rows, :]
            convd.append(c)
        act_ref[j % 2] = (_gelu(convd[0]) * convd[1]).astype(BF16)
        part = _dot(act_ref[j % 2], wdown_ref[j * FF_CHUNK:(j + 1) * FF_CHUNK, :])
        acc = part if acc is None else acc + part
        hups = nxt
    y = _rms(x1 + acc, gf_ref[...])

    @pl.when(step >= 2)
    def _():
        for cp in _block_copies(y_hbm, ybuf, sem_out, step - 2, slot, tc, True):
            cp.wait()

    ybuf[slot] = y.reshape(tc, SUBLANES, D_MODEL)
    for cp in _block_copies(y_hbm, ybuf, sem_out, step, slot, tc, True):
        cp.start()

    @pl.when(step == n_steps - 1)
    def _():
        if n_steps >= 2:
            for cp in _block_copies(y_hbm, ybuf, sem_out, step - 1, 1 - slot, tc, True):
                cp.wait()
        for cp in _block_copies(y_hbm, ybuf, sem_out, step, slot, tc, True):
            cp.wait()


N_BIG_WEIGHTS = 6
STAGE_ROWS = {D_MODEL: 256, D_S5: 256, POOL_CH: 256, 2 * D_FF: 64}


def _convert_weight(w_hbm, wb_ref, stage_ref, sem_ref):
    n_rows = w_hbm.shape[0]
    chunk = stage_ref.shape[1]
    n_chunks = n_rows // chunk

    def copy(c):
        return pltpu.make_async_copy(w_hbm.at[pl.ds(c * chunk, chunk), :], stage_ref.at[c % 2],
                                     sem_ref.at[c % 2])

    copy(0).start()
    for c in range(n_chunks):
        if c + 1 < n_chunks:
            copy(c + 1).start()
        copy(c).wait()
        wb_ref[c * chunk:(c + 1) * chunk, :] = stage_ref[c % 2].astype(BF16)


def _sample_kernel(x_ref, sre_in, sim_in, pool_hbm, conv_hbm,
                   g1_ref, are_ref, aim_ref, bbd_ref, cbd_ref, dskip_ref, pscale_ref, g2_ref,
                   cw_ref, cb_ref, gf_ref,
                   win_hbm, wglu_hbm, wpool_hbm, wout_hbm, wup_hbm, wdown_hbm,
                   y_ref, sre_ref, sim_ref, pool_out, conv_out,
                   winb_out, wglub_out, wpoolb_out, woutb_out, wupb_out, wdownb_out,
                   win_ref, wglu_ref, wpool_ref, wout_ref, wup_ref, wdown_ref,
                   stage_model, stage_s5, stage_pool, stage_ff,
                   poolbuf, convbuf, hupbuf, vbuf,
                   sem_stage, sem_state, sem_wout, sem_sout):
    def state_row(hbm_ref, r):
        return hbm_ref.at[:, pl.ds(r, 1), :]

    pool_in_copies = [pltpu.make_async_copy(state_row(pool_hbm, r), poolbuf.at[r], sem_state.at[0])
                      for r in range(POOL_BUF)]
    conv_in_copies = [pltpu.make_async_copy(state_row(conv_hbm, r), convbuf.at[r], sem_state.at[1])
                      for r in range(CONV_W - 1)]
    for cp in pool_in_copies + conv_in_copies:
        cp.start()

    conversions = (
        (win_hbm, win_ref, stage_model, winb_out), (wglu_hbm, wglu_ref, stage_s5, wglub_out),
        (wpool_hbm, wpool_ref, stage_pool, wpoolb_out), (wout_hbm, wout_ref, stage_model, woutb_out),
        (wup_hbm, wup_ref, stage_ff, wupb_out), (wdown_hbm, wdown_ref, stage_model, wdownb_out),
    )
    weight_out_copies = []
    for i, (w_hbm, wb_ref, stage_ref, wb_out) in enumerate(conversions):
        _convert_weight(w_hbm, wb_ref, stage_ref, sem_stage)
        cp = pltpu.make_async_copy(wb_ref, wb_out, sem_wout.at[i])
        cp.start()
        weight_out_copies.append(cp)

    for cp in pool_in_copies + conv_in_copies:
        cp.wait()

    x = x_ref[...]
    hb = _rms(x, g1_ref[...]).astype(BF16)
    proj = _dot(hb, win_ref[...])
    u = proj[:, :D_S5]
    v = proj[:, D_S5:]
    ub = u.astype(BF16)

    y_parts = []
    for k in range(S5_SPLIT):
        bu = _dot(ub[:, k * U_PER_SPLIT:(k + 1) * U_PER_SPLIT], bbd_ref[k])
        lanes = slice(k * ST_PER_SPLIT, (k + 1) * ST_PER_SPLIT)
        a_re = are_ref[:, lanes]
        a_im = aim_ref[:, lanes]
        h_re0 = sre_in[:, lanes]
        h_im0 = sim_in[:, lanes]
        h_re = a_re * h_re0 - a_im * h_im0 + bu[:, :ST_PER_SPLIT]
        h_im = a_re * h_im0 + a_im * h_re0 + bu[:, ST_PER_SPLIT:]
        sre_ref[:, lanes] = h_re
        sim_ref[:, lanes] = h_im
        hcat = jnp.concatenate([h_re, h_im], axis=1).astype(BF16)
        y_parts.append(_dot(hcat, cbd_ref[k]))
    y_s5 = _s5_post(jnp.concatenate(y_parts, axis=1), u, dskip_ref, wglu_ref)

    pooled = []
    for gi, w in enumerate(POOL_WINDOWS):
        lanes = slice(gi * POOL_CH, (gi + 1) * POOL_CH)
        vc = v[:, lanes]
        wsum = vc
        for back in range(1, w):
            wsum = wsum + poolbuf[POOL_BUF - back, :, 0, lanes]
        pooled.append(wsum / float(w) - vc)
    y_pool = _pool_project(pooled, wpool_ref, pscale_ref)
    vbuf[:, 0, :] = v
    state_out_copies = [pltpu.make_async_copy(poolbuf.at[r + 1], state_row(pool_out, r), sem_sout.at[0])
                        for r in range(POOL_BUF - 1)]
    state_out_copies.append(
        pltpu.make_async_copy(vbuf, state_row(pool_out, POOL_BUF - 1), sem_sout.at[0]))
    state_out_copies.append(pltpu.make_async_copy(convbuf.at[1], state_row(conv_out, 0), sem_sout.at[1]))
    for cp in state_out_copies:
        cp.start()

    x1 = x + _dot(y_s5.astype(BF16), wout_ref[:D_S5, :]) + _dot(y_pool.astype(BF16), wout_ref[D_S5:, :])

    h2b = _rms(x1, g2_ref[...]).astype(BF16)
    acc = None
    for j in range(N_FF_CHUNKS):
        convd = []
        for base in (0, D_FF):
            cols = slice(base + j * FF_CHUNK, base + (j + 1) * FF_CHUNK)
            hup = _dot(h2b, wup_ref[:, cols])
            hupbuf[:, 0, cols] = hup
            convd.append(cb_ref[:, cols] + cw_ref[0:1, cols] * convbuf[0, :, 0, cols]
                         + cw_ref[1:2, cols] * convbuf[1, :, 0, cols] + cw_ref[2:3, cols] * hup)
        act = (_gelu(convd[0]) * convd[1]).astype(BF16)
        part = _dot(act, wdown_ref[j * FF_CHUNK:(j + 1) * FF_CHUNK, :])
        acc = part if acc is None else acc + part
    y_ref[...] = _rms(x1 + acc, gf_ref[...])

    hup_copy = pltpu.make_async_copy(hupbuf, state_row(conv_out, 1), sem_sout.at[1])
    hup_copy.start()
    for cp in state_out_copies + [hup_copy] + weight_out_copies:
        cp.wait()


def _s5_tables(a_re, a_im, log_dt, b_re, b_im, c_re, c_im):
    dt = jnp.exp(log_dt)[:, None]
    mag = jnp.exp(dt * a_re)
    abar_re = mag * jnp.cos(dt * a_im)
    abar_im = mag * jnp.sin(dt * a_im)
    nr, ni = abar_re - 1.0, abar_im
    den = a_re * a_re + a_im * a_im
    f_re = ((nr * a_re + ni * a_im) / den)[:, :, None]
    f_im = ((ni * a_re - nr * a_im) / den)[:, :, None]
    bbar_re = f_re * b_re - f_im * b_im
    bbar_im = f_re * b_im + f_im * b_re
    eye = jnp.eye(GROUPS_PER_SPLIT, dtype=F32)

    def in_table(b):
        b = b.reshape(S5_SPLIT, GROUPS_PER_SPLIT, S5_STATE, S5_CH)
        t = jnp.einsum('ab,kaph->kahbp', eye, b)
        return t.reshape(S5_SPLIT, U_PER_SPLIT, ST_PER_SPLIT)

    def out_table(c):
        c = c.reshape(S5_SPLIT, GROUPS_PER_SPLIT, S5_CH, S5_STATE)
        t = jnp.einsum('ab,kahp->kapbh', eye, c)
        return t.reshape(S5_SPLIT, ST_PER_SPLIT, U_PER_SPLIT)

    bbd = jnp.concatenate([in_table(bbar_re), in_table(bbar_im)], axis=2)
    cbd = jnp.concatenate([out_table(c_re), out_table(-c_im)], axis=1)
    return abar_re.reshape(1, -1), abar_im.reshape(1, -1), bbd.astype(BF16), cbd.astype(BF16)


def _vmem_spec():
    return pl.BlockSpec(memory_space=pltpu.VMEM)


PROMPT_TC = 64
VMEM_LIMIT_BYTES = 60 * 1024 * 1024


def kernel(x_prompt, x_sample, state_s5_re, state_s5_im, state_pool, state_ffn_conv, norm_mix_g, w_in, s5_a_re, s5_a_im, s5_log_dt, s5_b_re, s5_b_im, s5_c_re, s5_c_im, s5_d, s5_w_glu, pool_w, pool_scale, w_out, norm_ffn_g, ffn_w_up, ffn_conv_w, ffn_conv_b, ffn_w_down, norm_final_g):
    nb, seq, _ = x_prompt.shape
    ns = x_sample.shape[0]
    assert nb == SUBLANES and seq % PROMPT_TC == 0 and x_sample.shape[1] == 1
    assert norm_mix_g.shape[0] == 1, "single layer"

    a_re, a_im, bbd, cbd = _s5_tables(s5_a_re[0], s5_a_im[0], s5_log_dt[0], s5_b_re[0],
                                      s5_b_im[0], s5_c_re[0], s5_c_im[0])
    n_states = S5_GROUPS * S5_STATE
    cparams = dict(vmem_limit_bytes=VMEM_LIMIT_BYTES)
    g1 = norm_mix_g[0].reshape(1, D_MODEL)
    dskip = s5_d[0].reshape(1, D_S5)
    pscale = pool_scale[0].reshape(1, D_POOL)
    g2 = norm_ffn_g[0].reshape(1, D_MODEL)
    cw = ffn_conv_w[0]
    cb = ffn_conv_b[0].reshape(1, 2 * D_FF)
    gf = norm_final_g.reshape(1, D_MODEL)

    big_f32 = (w_in[0], s5_w_glu[0], pool_w[0].reshape(len(POOL_WINDOWS) * POOL_CH, POOL_CH),
               w_out[0], ffn_w_up[0], ffn_w_down[0])
    small = (g1, a_re, a_im, bbd, cbd, dskip, pscale, g2, cw, cb, gf)
    any_spec = pl.BlockSpec(memory_space=pl.ANY)
    stage_shapes = {w.shape[1]: (2, STAGE_ROWS[w.shape[1]], w.shape[1]) for w in big_f32}
    sample_out = pl.pallas_call(
        _sample_kernel,
        in_specs=[_vmem_spec()] * 3 + [any_spec] * 2 + [_vmem_spec()] * len(small)
                 + [any_spec] * N_BIG_WEIGHTS,
        out_specs=[_vmem_spec()] * 3 + [any_spec] * (2 + N_BIG_WEIGHTS),
        out_shape=[
            jax.ShapeDtypeStruct((ns, D_MODEL), F32),
            jax.ShapeDtypeStruct((ns, n_states), F32),
            jax.ShapeDtypeStruct((ns, n_states), F32),
            jax.ShapeDtypeStruct((ns, POOL_BUF, D_POOL), F32),
            jax.ShapeDtypeStruct((ns, CONV_W - 1, 2 * D_FF), F32),
        ] + [jax.ShapeDtypeStruct(w.shape, BF16) for w in big_f32],
        scratch_shapes=[pltpu.VMEM(w.shape, BF16) for w in big_f32] + [
            pltpu.VMEM(stage_shapes[D_MODEL], F32),
            pltpu.VMEM(stage_shapes[D_S5], F32),
            pltpu.VMEM(stage_shapes[POOL_CH], F32),
            pltpu.VMEM(stage_shapes[2 * D_FF], F32),
            pltpu.VMEM((POOL_BUF, ns, 1, D_POOL), F32),
            pltpu.VMEM((CONV_W - 1, ns, 1, 2 * D_FF), F32),
            pltpu.VMEM((ns, 1, 2 * D_FF), F32),
            pltpu.VMEM((ns, 1, D_POOL), F32),
            pltpu.SemaphoreType.DMA((2,)),
            pltpu.SemaphoreType.DMA((2,)),
            pltpu.SemaphoreType.DMA((N_BIG_WEIGHTS,)),
            pltpu.SemaphoreType.DMA((2,)),
        ],
        compiler_params=pltpu.CompilerParams(**cparams),
        name="sample_layer",
    )(x_sample.reshape(ns, D_MODEL), state_s5_re[0].reshape(ns, n_states),
      state_s5_im[0].reshape(ns, n_states), state_pool[0], state_ffn_conv[0], *small, *big_f32)
    ys, s_re, s_im, s_pool, s_conv = sample_out[:5]
    win_b, wglu_b, wpool_b, wout_b, wup_b, wdown_b = sample_out[5:]
    weights = (g1, win_b, a_re, a_im, bbd, cbd, dskip, wglu_b, wpool_b, pscale, wout_b, g2,
               wup_b, cw, cb, wdown_b, gf)

    tc = PROMPT_TC
    rows = tc * SUBLANES
    const = lambda i: (0, 0)
    y_prompt, p_re, p_im, p_pool, p_conv = pl.pallas_call(
        functools.partial(_prompt_kernel, tc, seq // tc),
        grid=(seq // tc,),
        in_specs=[pl.BlockSpec(memory_space=pl.ANY)] + [_vmem_spec()] * len(weights),
        out_specs=[
            pl.BlockSpec(memory_space=pl.ANY),
            pl.BlockSpec((SUBLANES, n_states), const),
            pl.BlockSpec((SUBLANES, n_states), const),
            pl.BlockSpec((POOL_BUF * SUBLANES, D_POOL), const),
            pl.BlockSpec(((CONV_W - 1) * SUBLANES, 2 * D_FF), const),
        ],
        out_shape=[
            jax.ShapeDtypeStruct((nb, seq, D_MODEL), F32),
            jax.ShapeDtypeStruct((SUBLANES, n_states), F32),
            jax.ShapeDtypeStruct((SUBLANES, n_states), F32),
            jax.ShapeDtypeStruct((POOL_BUF * SUBLANES, D_POOL), F32),
            jax.ShapeDtypeStruct(((CONV_W - 1) * SUBLANES, 2 * D_FF), F32),
        ],
        scratch_shapes=[
            pltpu.VMEM((rows, 2 * ST_PER_SPLIT), F32),
            pltpu.VMEM((rows, 2 * ST_PER_SPLIT), BF16),
            pltpu.VMEM((2, rows, FF_CHUNK), BF16),
            pltpu.VMEM((2, tc, SUBLANES, D_MODEL), F32),
            pltpu.VMEM((2, tc, SUBLANES, D_MODEL), F32),
            pltpu.SemaphoreType.DMA((2,)),
            pltpu.SemaphoreType.DMA((2,)),
        ],
        compiler_params=pltpu.CompilerParams(dimension_semantics=("arbitrary",), **cparams),
        name="prompt_layer",
    )(x_prompt, *weights)
    new_pool_p = jnp.transpose(p_pool.reshape(POOL_BUF, nb, D_POOL), (1, 0, 2))[None]
    new_conv_p = jnp.transpose(p_conv.reshape(CONV_W - 1, nb, 2 * D_FF), (1, 0, 2))[None]
    new_re_p = p_re.reshape(1, nb, S5_GROUPS, S5_STATE)
    new_im_p = p_im.reshape(1, nb, S5_GROUPS, S5_STATE)

    return (y_prompt, ys.reshape(ns, 1, D_MODEL), new_re_p, new_im_p, new_pool_p, new_conv_p,
            s_re.reshape(1, ns, S5_GROUPS, S5_STATE), s_im.reshape(1, ns, S5_GROUPS, S5_STATE),
            s_pool[None], s_conv[None])
```

```python
import functools
import math

import numpy as np
import jax
import jax.numpy as jnp
from jax import lax
from jax.experimental import pallas as pl
from jax.experimental.pallas import tpu as pltpu

D_MODEL = 1024
D_S5 = 512
S5_CH = 16
S5_GROUPS = 32
S5_STATE = 64
D_POOL = 512
POOL_WINDOWS = (2, 4, 8, 16)
POOL_CH = 128
POOL_BUF = 15
D_FF = 2816
CONV_W = 3
EPS = 1e-6

SUBLANES = 8
S5_SPLIT = 2
GROUPS_PER_SPLIT = S5_GROUPS // S5_SPLIT
U_PER_SPLIT = GROUPS_PER_SPLIT * S5_CH
ST_PER_SPLIT = GROUPS_PER_SPLIT * S5_STATE
FF_CHUNK = 256
N_FF_CHUNKS = D_FF // FF_CHUNK
SQRT_HALF = float(np.sqrt(0.5).astype(np.float32))

BF16 = jnp.bfloat16
F32 = jnp.float32


def _dot(a, b):
    return jnp.dot(a, b, preferred_element_type=F32)


def _rms(x, g):
    ms = jnp.mean(x * x, axis=-1, keepdims=True)
    return x * lax.rsqrt(ms + EPS) * g


def _gelu(x):
    return 0.5 * x * (1.0 + lax.erf(x * SQRT_HALF))


def _s5_post(y_lin, u, dskip_ref, wglu_ref):
    y = _gelu(y_lin + dskip_ref[...] * u)
    return y * jax.nn.sigmoid(_dot(y.astype(BF16), wglu_ref[...]))


def _pool_project(pooled_cols, wpool_ref, pscale_ref):
    outs = []
    for gi in range(len(POOL_WINDOWS)):
        z = _dot(pooled_cols[gi].astype(BF16), wpool_ref[gi * POOL_CH:(gi + 1) * POOL_CH, :])
        outs.append(z * pscale_ref[:, gi * POOL_CH:(gi + 1) * POOL_CH])
    return jnp.concatenate(outs, axis=1)


def _block_copies(hbm_ref, buf_ref, sem_ref, block, slot, tc, to_hbm):
    copies = []
    for n in range(SUBLANES):
        hbm = hbm_ref.at[n, pl.ds(block * tc, tc), :]
        vmem = buf_ref.at[slot, :, n, :]
        src, dst = (vmem, hbm) if to_hbm else (hbm, vmem)
        copies.append(pltpu.make_async_copy(src, dst, sem_ref.at[slot]))
    return copies


def _prompt_kernel(tc, n_steps,
                   x_hbm, g1_ref, win_ref, are_ref, aim_ref, bbd_ref, cbd_ref, dskip_ref,
                   wglu_ref, wpool_ref, pscale_ref, wout_ref, g2_ref, wup_ref, cw_ref,
                   cb_ref, wdown_ref, gf_ref,
                   y_hbm, sre_ref, sim_ref, pool_ref, conv_ref,
                   bu_ref, h_ref, act_ref, xbuf, ybuf, sem_in, sem_out):
    rows = tc * SUBLANES
    step = pl.program_id(0)
    slot = step % 2

    @pl.when(step == 0)
    def _():
        sre_ref[...] = jnp.zeros_like(sre_ref)
        sim_ref[...] = jnp.zeros_like(sim_ref)
        pool_ref[...] = jnp.zeros_like(pool_ref)
        conv_ref[...] = jnp.zeros_like(conv_ref)
        for cp in _block_copies(x_hbm, xbuf, sem_in, 0, 0, tc, False):
            cp.start()

    @pl.when(step + 1 < n_steps)
    def _():
        for cp in _block_copies(x_hbm, xbuf, sem_in, step + 1, 1 - slot, tc, False):
            cp.start()

    for cp in _block_copies(x_hbm, xbuf, sem_in, step, slot, tc, False):
        cp.wait()

    x = xbuf[slot].reshape(rows, D_MODEL)
    hb = _rms(x, g1_ref[...]).astype(BF16)
    proj = _dot(hb, win_ref[...])
    u = proj[:, :D_S5]
    v = proj[:, D_S5:]
    ub = u.astype(BF16)

    y_parts = []
    for k in range(S5_SPLIT):
        bu_ref[...] = _dot(ub[:, k * U_PER_SPLIT:(k + 1) * U_PER_SPLIT], bbd_ref[k])
        lanes = slice(k * ST_PER_SPLIT, (k + 1) * ST_PER_SPLIT)
        a_re = jnp.broadcast_to(are_ref[:, lanes], (SUBLANES, ST_PER_SPLIT))
        a_im = jnp.broadcast_to(aim_ref[:, lanes], (SUBLANES, ST_PER_SPLIT))

        def two_steps(i, carry, a_re=a_re, a_im=a_im):
            h_re, h_im = carry
            r0 = pl.multiple_of(i * (2 * SUBLANES), 2 * SUBLANES)
            res, ims = [], []
            for s in range(2):
                rs = pl.ds(r0 + s * SUBLANES, SUBLANES)
                n_re = a_re * h_re - a_im * h_im + bu_ref[rs, :ST_PER_SPLIT]
                n_im = a_re * h_im + a_im * h_re + bu_ref[rs, ST_PER_SPLIT:]
                h_re, h_im = n_re, n_im
                res.append(h_re)
                ims.append(h_im)
            pair = pl.ds(r0, 2 * SUBLANES)
            h_ref[pair, :ST_PER_SPLIT] = jnp.concatenate(res, axis=0).astype(BF16)
            h_ref[pair, ST_PER_SPLIT:] = jnp.concatenate(ims, axis=0).astype(BF16)
            return h_re, h_im

        h_re, h_im = lax.fori_loop(0, tc // 2, two_steps, (sre_ref[:, lanes], sim_ref[:, lanes]),
                                   unroll=True)
        sre_ref[:, lanes] = h_re
        sim_ref[:, lanes] = h_im
        half = rows // 2
        y_parts.append(jnp.concatenate(
            [_dot(h_ref[:half, :], cbd_ref[k]), _dot(h_ref[half:, :], cbd_ref[k])], axis=0))
    y_s5 = _s5_post(jnp.concatenate(y_parts, axis=1), u, dskip_ref, wglu_ref)

    halo = POOL_BUF * SUBLANES
    vfull = jnp.concatenate([pool_ref[...], v], axis=0)
    pool_ref[...] = vfull[rows:, :]
    t_idx = step * tc + (lax.broadcasted_iota(jnp.int32, (rows, POOL_CH), 0) >> 3)
    pooled = []
    for gi, w in enumerate(POOL_WINDOWS):
        s = vfull[:, gi * POOL_CH:(gi + 1) * POOL_CH]
        span = 1
        while span < w:
            sh = span * SUBLANES
            s = s[sh:, :] + s[:-sh, :]
            span *= 2
        first = (POOL_BUF - (w - 1)) * SUBLANES
        wsum = s[first:first + rows, :]
        cnt = jnp.minimum(t_idx + 1, w).astype(F32)
        pooled.append(wsum / cnt - v[:, gi * POOL_CH:(gi + 1) * POOL_CH])
    y_pool = _pool_project(pooled, wpool_ref, pscale_ref)

    x1 = x + _dot(y_s5.astype(BF16), wout_ref[:D_S5, :]) + _dot(y_pool.astype(BF16), wout_ref[D_S5:, :])

    h2b = _rms(x1, g2_ref[...]).astype(BF16)
    taps = (CONV_W - 1) * SUBLANES
    def ff_cols(j):
        return [slice(base + j * FF_CHUNK, base + (j + 1) * FF_CHUNK) for base in (0, D_FF)]

    def up_project(j):
        return [_dot(h2b, wup_ref[:, cols]) for cols in ff_cols(j)]

    acc = None
    hups = up_project(0)
    for j in range(N_FF_CHUNKS):
        nxt = up_project(j + 1) if j + 1 < N_FF_CHUNKS else None
        convd = []
        for hup, cols in zip(hups, ff_cols(j)):
            full = jnp.concatenate([conv_ref[:, cols], hup], axis=0)
            conv_ref[:, cols] = hup[rows - taps:, :]
            c = cb_ref[:, cols]
            for kk in range(CONV_W):
                c = c + cw_ref[kk:kk + 1, cols] * full[kk * SUBLANES:kk * SUBLANES + rows, :]
            convd.append(c)
        act_ref[j % 2] = (_gelu(convd[0]) * convd[1]).astype(BF16)
        part = _dot(act_ref[j % 2], wdown_ref[j * FF_CHUNK:(j + 1) * FF_CHUNK, :])
        acc = part if acc is None else acc + part
        hups = nxt
    y = _rms(x1 + acc, gf_ref[...])

    @pl.when(step >= 2)
    def _():
        for cp in _block_copies(y_hbm, ybuf, sem_out, step - 2, slot, tc, True):
            cp.wait()

    ybuf[slot] = y.reshape(tc, SUBLANES, D_MODEL)
    for cp in _block_copies(y_hbm, ybuf, sem_out, step, slot, tc, True):
        cp.start()

    @pl.when(step == n_steps - 1)
    def _():
        if n_steps >= 2:
            for cp in _block_copies(y_hbm, ybuf, sem_out, step - 1, 1 - slot, tc, True):
                cp.wait()
        for cp in _block_copies(y_hbm, ybuf, sem_out, step, slot, tc, True):
            cp.wait()


N_BIG_WEIGHTS = 6
STAGE_ROWS = {D_MODEL: 256, D_S5: 256, POOL_CH: 256, 2 * D_FF: 128}
STAGE_SLOTS = 3
STAGE_LOOKAHEAD = STAGE_SLOTS - 1


def _convert_weights(conversions, stages, sem_stage, sem_wout):
    chunks = []
    ring_pos = [0] * len(stages)
    for w_idx, (w_hbm, _, sid, _) in enumerate(conversions):
        rows = stages[sid].shape[1]
        n_chunks = w_hbm.shape[0] // rows
        for c in range(n_chunks):
            chunks.append((w_idx, sid, ring_pos[sid] % STAGE_SLOTS, c * rows, rows, c == n_chunks - 1))
            ring_pos[sid] += 1

    def read(chunk):
        w_idx, sid, slot, r0, rows, _ = chunk
        return pltpu.make_async_copy(conversions[w_idx][0].at[pl.ds(r0, rows), :],
                                     stages[sid].at[slot], sem_stage.at[sid * STAGE_SLOTS + slot])

    write_backs = []
    started = 0
    for i, chunk in enumerate(chunks):
        while started < min(len(chunks), i + 1 + STAGE_LOOKAHEAD):
            read(chunks[started]).start()
            started += 1
        read(chunk).wait()
        w_idx, sid, slot, r0, rows, last = chunk
        _, wb_ref, _, wb_out = conversions[w_idx]
        wb_ref[r0:r0 + rows, :] = stages[sid][slot].astype(BF16)
        if last:
            cp = pltpu.make_async_copy(wb_ref, wb_out, sem_wout.at[w_idx])
            cp.start()
            write_backs.append(cp)
    return write_backs


def _sample_kernel(x_hbm, sre_in, sim_in, pool_hbm, conv_hbm,
                   g1_ref, are_ref, aim_ref, bbd_ref, cbd_ref, dskip_ref, pscale_ref, g2_ref,
                   cw_ref, cb_ref, gf_ref,
                   win_hbm, wglu_hbm, wpool_hbm, wout_hbm, wup_hbm, wdown_hbm,
                   y_hbm, sre_ref, sim_ref, pool_out, conv_out,
                   winb_out, wglub_out, wpoolb_out, woutb_out, wupb_out, wdownb_out,
                   win_ref, wglu_ref, wpool_ref, wout_ref, wup_ref, wdown_ref,
                   stage_model, stage_s5, stage_pool, stage_ff,
                   xbuf, ybuf, poolbuf, convbuf, hupbuf, vbuf,
                   sem_stage, sem_state, sem_wout, sem_sout):
    def state_row(hbm_ref, r):
        return hbm_ref.at[:, pl.ds(r, 1), :]

    in_copies = [pltpu.make_async_copy(x_hbm, xbuf, sem_state.at[0]),
                 pltpu.make_async_copy(pool_hbm, poolbuf, sem_state.at[1])]
    in_copies += [pltpu.make_async_copy(state_row(conv_hbm, r), convbuf.at[r], sem_state.at[2])
                  for r in range(CONV_W - 1)]
    for cp in in_copies:
        cp.start()

    stages = (stage_model, stage_s5, stage_pool, stage_ff)
    conversions = (
        (win_hbm, win_ref, 0, winb_out), (wglu_hbm, wglu_ref, 1, wglub_out),
        (wpool_hbm, wpool_ref, 2, wpoolb_out), (wout_hbm, wout_ref, 0, woutb_out),
        (wup_hbm, wup_ref, 3, wupb_out), (wdown_hbm, wdown_ref, 0, wdownb_out),
    )
    weight_out_copies = _convert_weights(conversions, stages, sem_stage, sem_wout)

    for cp in in_copies:
        cp.wait()

    x = xbuf[:, 0, :]
    hb = _rms(x, g1_ref[...]).astype(BF16)
    proj = _dot(hb, win_ref[...])
    u = proj[:, :D_S5]
    v = proj[:, D_S5:]
    ub = u.astype(BF16)

    y_parts = []
    for k in range(S5_SPLIT):
        bu = _dot(ub[:, k * U_PER_SPLIT:(k + 1) * U_PER_SPLIT], bbd_ref[k])
        lanes = slice(k * ST_PER_SPLIT, (k + 1) * ST_PER_SPLIT)
        a_re = are_ref[:, lanes]
        a_im = aim_ref[:, lanes]
        h_re0 = sre_in[:, lanes]
        h_im0 = sim_in[:, lanes]
        h_re = a_re * h_re0 - a_im * h_im0 + bu[:, :ST_PER_SPLIT]
        h_im = a_re * h_im0 + a_im * h_re0 + bu[:, ST_PER_SPLIT:]
        sre_ref[:, lanes] = h_re
        sim_ref[:, lanes] = h_im
        hcat = jnp.concatenate([h_re, h_im], axis=1).astype(BF16)
        y_parts.append(_dot(hcat, cbd_ref[k]))
    y_s5 = _s5_post(jnp.concatenate(y_parts, axis=1), u, dskip_ref, wglu_ref)

    pooled = []
    for gi, w in enumerate(POOL_WINDOWS):
        lanes = slice(gi * POOL_CH, (gi + 1) * POOL_CH)
        vc = v[:, lanes]
        wsum = vc
        for back in range(1, w):
            wsum = wsum + poolbuf[POOL_BUF - back, :, lanes]
        pooled.append(wsum / float(w) - vc)
    y_pool = _pool_project(pooled, wpool_ref, pscale_ref)
    vbuf[...] = v
    state_out_copies = [
        pltpu.make_async_copy(poolbuf.at[pl.ds(1, POOL_BUF - 1)], pool_out.at[pl.ds(0, POOL_BUF - 1)],
                              sem_sout.at[0]),
        pltpu.make_async_copy(vbuf, pool_out.at[POOL_BUF - 1], sem_sout.at[1]),
        pltpu.make_async_copy(convbuf.at[1], state_row(conv_out, 0), sem_sout.at[2]),
    ]
    for cp in state_out_copies:
        cp.start()

    x1 = x + _dot(y_s5.astype(BF16), wout_ref[:D_S5, :]) + _dot(y_pool.astype(BF16), wout_ref[D_S5:, :])

    h2b = _rms(x1, g2_ref[...]).astype(BF16)
    acc = None
    for j in range(N_FF_CHUNKS):
        convd = []
        for base in (0, D_FF):
            cols = slice(base + j * FF_CHUNK, base + (j + 1) * FF_CHUNK)
            hup = _dot(h2b, wup_ref[:, cols])
            hupbuf[:, 0, cols] = hup
            convd.append(cb_ref[:, cols] + cw_ref[0:1, cols] * convbuf[0, :, 0, cols]
                         + cw_ref[1:2, cols] * convbuf[1, :, 0, cols] + cw_ref[2:3, cols] * hup)
        act = (_gelu(convd[0]) * convd[1]).astype(BF16)
        part = _dot(act, wdown_ref[j * FF_CHUNK:(j + 1) * FF_CHUNK, :])
        acc = part if acc is None else acc + part
    ybuf[:, 0, :] = _rms(x1 + acc, gf_ref[...])

    tail_copies = [pltpu.make_async_copy(hupbuf, state_row(conv_out, 1), sem_sout.at[3]),
                   pltpu.make_async_copy(ybuf, y_hbm, sem_sout.at[4])]
    for cp in tail_copies:
        cp.start()
    for cp in state_out_copies + tail_copies + weight_out_copies:
        cp.wait()


def _s5_tables(a_re, a_im, log_dt, b_re, b_im, c_re, c_im):
    dt = jnp.exp(log_dt)[:, None]
    mag = jnp.exp(dt * a_re)
    abar_re = mag * jnp.cos(dt * a_im)
    abar_im = mag * jnp.sin(dt * a_im)
    nr, ni = abar_re - 1.0, abar_im
    den = a_re * a_re + a_im * a_im
    f_re = ((nr * a_re + ni * a_im) / den)[:, :, None]
    f_im = ((ni * a_re - nr * a_im) / den)[:, :, None]
    bbar_re = f_re * b_re - f_im * b_im
    bbar_im = f_re * b_im + f_im * b_re
    u_group = lax.broadcasted_iota(jnp.int32, (U_PER_SPLIT, ST_PER_SPLIT), 0) // S5_CH
    st_group = lax.broadcasted_iota(jnp.int32, (U_PER_SPLIT, ST_PER_SPLIT), 1) // S5_STATE
    diag = u_group == st_group

    def in_table(b):
        b = b.reshape(S5_SPLIT, GROUPS_PER_SPLIT, S5_STATE, S5_CH)
        row = jnp.transpose(b, (0, 3, 1, 2)).reshape(S5_SPLIT, 1, S5_CH, ST_PER_SPLIT)
        rep = jnp.broadcast_to(row, (S5_SPLIT, GROUPS_PER_SPLIT, S5_CH, ST_PER_SPLIT))
        return jnp.where(diag, rep.reshape(S5_SPLIT, U_PER_SPLIT, ST_PER_SPLIT), 0.0)

    def out_table(c):
        c = c.reshape(S5_SPLIT, GROUPS_PER_SPLIT, S5_CH, S5_STATE)
        col = jnp.transpose(c, (0, 1, 3, 2)).reshape(S5_SPLIT, ST_PER_SPLIT, S5_CH)
        rep = jnp.tile(col, (1, 1, GROUPS_PER_SPLIT))
        return jnp.where(diag.T, rep, 0.0)

    bbd = jnp.concatenate([in_table(bbar_re), in_table(bbar_im)], axis=2)
    cbd = jnp.concatenate([out_table(c_re), out_table(-c_im)], axis=1)
    return abar_re.reshape(1, -1), abar_im.reshape(1, -1), bbd.astype(BF16), cbd.astype(BF16)


def _vmem_spec():
    return pl.BlockSpec(memory_space=pltpu.VMEM)


PROMPT_TC = 64
VMEM_LIMIT_BYTES = 60 * 1024 * 1024


def kernel(x_prompt, x_sample, state_s5_re, state_s5_im, state_pool, state_ffn_conv, norm_mix_g, w_in, s5_a_re, s5_a_im, s5_log_dt, s5_b_re, s5_b_im, s5_c_re, s5_c_im, s5_d, s5_w_glu, pool_w, pool_scale, w_out, norm_ffn_g, ffn_w_up, ffn_conv_w, ffn_conv_b, ffn_w_down, norm_final_g):
    nb, seq, _ = x_prompt.shape
    ns = x_sample.shape[0]
    assert nb == SUBLANES and seq % PROMPT_TC == 0 and x_sample.shape[1] == 1
    assert norm_mix_g.shape[0] == 1, "single layer"

    a_re, a_im, bbd, cbd = _s5_tables(s5_a_re[0], s5_a_im[0], s5_log_dt[0], s5_b_re[0],
                                      s5_b_im[0], s5_c_re[0], s5_c_im[0])
    n_states = S5_GROUPS * S5_STATE
    cparams = dict(vmem_limit_bytes=VMEM_LIMIT_BYTES)
    g1 = norm_mix_g[0].reshape(1, D_MODEL)
    dskip = s5_d[0].reshape(1, D_S5)
    pscale = pool_scale[0].reshape(1, D_POOL)
    g2 = norm_ffn_g[0].reshape(1, D_MODEL)
    cw = ffn_conv_w[0]
    cb = ffn_conv_b[0].reshape(1, 2 * D_FF)
    gf = norm_final_g.reshape(1, D_MODEL)

    big_f32 = (w_in[0], s5_w_glu[0], pool_w[0].reshape(len(POOL_WINDOWS) * POOL_CH, POOL_CH),
               w_out[0], ffn_w_up[0], ffn_w_down[0])
    small = (g1, a_re, a_im, bbd, cbd, dskip, pscale, g2, cw, cb, gf)
    any_spec = pl.BlockSpec(memory_space=pl.ANY)
    stage_widths = (D_MODEL, D_S5, POOL_CH, 2 * D_FF)
    pool_in = jnp.transpose(state_pool[0], (1, 0, 2))
    sample_out = pl.pallas_call(
        _sample_kernel,
        in_specs=[any_spec] + [_vmem_spec()] * 2 + [any_spec] * 2 + [_vmem_spec()] * len(small)
                 + [any_spec] * N_BIG_WEIGHTS,
        out_specs=[any_spec] + [_vmem_spec()] * 2 + [any_spec] * (2 + N_BIG_WEIGHTS),
        out_shape=[
            jax.ShapeDtypeStruct((ns, 1, D_MODEL), F32),
            jax.ShapeDtypeStruct((ns, n_states), F32),
            jax.ShapeDtypeStruct((ns, n_states), F32),
            jax.ShapeDtypeStruct((POOL_BUF, ns, D_POOL), F32),
            jax.ShapeDtypeStruct((ns, CONV_W - 1, 2 * D_FF), F32),
        ] + [jax.ShapeDtypeStruct(w.shape, BF16) for w in big_f32],
        scratch_shapes=[pltpu.VMEM(w.shape, BF16) for w in big_f32]
        + [pltpu.VMEM((STAGE_SLOTS, STAGE_ROWS[width], width), F32) for width in stage_widths] + [
            pltpu.VMEM((ns, 1, D_MODEL), F32),
            pltpu.VMEM((ns, 1, D_MODEL), F32),
            pltpu.VMEM((POOL_BUF, ns, D_POOL), F32),
            pltpu.VMEM((CONV_W - 1, ns, 1, 2 * D_FF), F32),
            pltpu.VMEM((ns, 1, 2 * D_FF), F32),
            pltpu.VMEM((ns, D_POOL), F32),
            pltpu.SemaphoreType.DMA((len(stage_widths) * STAGE_SLOTS,)),
            pltpu.SemaphoreType.DMA((3,)),
            pltpu.SemaphoreType.DMA((N_BIG_WEIGHTS,)),
            pltpu.SemaphoreType.DMA((5,)),
        ],
        compiler_params=pltpu.CompilerParams(**cparams),
        name="sample_layer",
    )(x_sample, state_s5_re[0].reshape(ns, n_states), state_s5_im[0].reshape(ns, n_states),
      pool_in, state_ffn_conv[0], *small, *big_f32)
    ys, s_re, s_im, s_pool, s_conv = sample_out[:5]
    win_b, wglu_b, wpool_b, wout_b, wup_b, wdown_b = sample_out[5:]
    weights = (g1, win_b, a_re, a_im, bbd, cbd, dskip, wglu_b, wpool_b, pscale, wout_b, g2,
               wup_b, cw, cb, wdown_b, gf)

    tc = PROMPT_TC
    rows = tc * SUBLANES
    const = lambda i: (0, 0)
    y_prompt, p_re, p_im, p_pool, p_conv = pl.pallas_call(
        functools.partial(_prompt_kernel, tc, seq // tc),
        grid=(seq // tc,),
        in_specs=[pl.BlockSpec(memory_space=pl.ANY)] + [_vmem_spec()] * len(weights),
        out_specs=[
            pl.BlockSpec(memory_space=pl.ANY),
            pl.BlockSpec((SUBLANES, n_states), const),
            pl.BlockSpec((SUBLANES, n_states), const),
            pl.BlockSpec((POOL_BUF * SUBLANES, D_POOL), const),
            pl.BlockSpec(((CONV_W - 1) * SUBLANES, 2 * D_FF), const),
        ],
        out_shape=[
            jax.ShapeDtypeStruct((nb, seq, D_MODEL), F32),
            jax.ShapeDtypeStruct((SUBLANES, n_states), F32),
            jax.ShapeDtypeStruct((SUBLANES, n_states), F32),
            jax.ShapeDtypeStruct((POOL_BUF * SUBLANES, D_POOL), F32),
            jax.ShapeDtypeStruct(((CONV_W - 1) * SUBLANES, 2 * D_FF), F32),
        ],
        scratch_shapes=[
            pltpu.VMEM((rows, 2 * ST_PER_SPLIT), F32),
            pltpu.VMEM((rows, 2 * ST_PER_SPLIT), BF16),
            pltpu.VMEM((2, rows, FF_CHUNK), BF16),
            pltpu.VMEM((2, tc, SUBLANES, D_MODEL), F32),
            pltpu.VMEM((2, tc, SUBLANES, D_MODEL), F32),
            pltpu.SemaphoreType.DMA((2,)),
            pltpu.SemaphoreType.DMA((2,)),
        ],
        compiler_params=pltpu.CompilerParams(dimension_semantics=("arbitrary",), **cparams),
        name="prompt_layer",
    )(x_prompt, *weights)
    new_pool_p = jnp.transpose(p_pool.reshape(POOL_BUF, nb, D_POOL), (1, 0, 2))[None]
    new_conv_p = jnp.transpose(p_conv.reshape(CONV_W - 1, nb, 2 * D_FF), (1, 0, 2))[None]
    new_re_p = p_re.reshape(1, nb, S5_GROUPS, S5_STATE)
    new_im_p = p_im.reshape(1, nb, S5_GROUPS, S5_STATE)

    return (y_prompt, ys, new_re_p, new_im_p, new_pool_p, new_conv_p,
            s_re.reshape(1, ns, S5_GROUPS, S5_STATE), s_im.reshape(1, ns, S5_GROUPS, S5_STATE),
            jnp.transpose(s_pool, (1, 0, 2))[None], s_conv[None])
```

```python
import functools
import math

import numpy as np
import jax
import jax.numpy as jnp
from jax import lax
from jax.experimental import pallas as pl
from jax.experimental.pallas import tpu as pltpu

D_MODEL = 1024
D_S5 = 512
S5_CH = 16
S5_GROUPS = 32
S5_STATE = 64
D_POOL = 512
POOL_WINDOWS = (2, 4, 8, 16)
POOL_CH = 128
POOL_BUF = 15
D_FF = 2816
CONV_W = 3
EPS = 1e-6

SUBLANES = 8
S5_SPLIT = 2
GROUPS_PER_SPLIT = S5_GROUPS // S5_SPLIT
U_PER_SPLIT = GROUPS_PER_SPLIT * S5_CH
ST_PER_SPLIT = GROUPS_PER_SPLIT * S5_STATE
FF_CHUNK = 256
N_FF_CHUNKS = D_FF // FF_CHUNK
SQRT_HALF = float(np.sqrt(0.5).astype(np.float32))

BF16 = jnp.bfloat16
F32 = jnp.float32


def _dot(a, b):
    return jnp.dot(a, b, preferred_element_type=F32)


def _rms(x, g):
    ms = jnp.mean(x * x, axis=-1, keepdims=True)
    return x * lax.rsqrt(ms + EPS) * g


def _gelu(x):
    return 0.5 * x * (1.0 + lax.erf(x * SQRT_HALF))


def _s5_post(y_lin, u, dskip_ref, wglu_ref):
    y = _gelu(y_lin + dskip_ref[...] * u)
    return y * jax.nn.sigmoid(_dot(y.astype(BF16), wglu_ref[...]))


def _pool_project(pooled_cols, wpool_ref, pscale_ref):
    outs = []
    for gi in range(len(POOL_WINDOWS)):
        z = _dot(pooled_cols[gi].astype(BF16), wpool_ref[gi * POOL_CH:(gi + 1) * POOL_CH, :])
        outs.append(z * pscale_ref[:, gi * POOL_CH:(gi + 1) * POOL_CH])
    return jnp.concatenate(outs, axis=1)


def _block_copies(hbm_ref, buf_ref, sem_ref, block, slot, tc, to_hbm):
    copies = []
    for n in range(SUBLANES):
        hbm = hbm_ref.at[n, pl.ds(block * tc, tc), :]
        vmem = buf_ref.at[slot, :, n, :]
        src, dst = (vmem, hbm) if to_hbm else (hbm, vmem)
        copies.append(pltpu.make_async_copy(src, dst, sem_ref.at[slot]))
    return copies


def _prompt_kernel(tc, n_steps,
                   x_hbm, g1_ref, win_ref, are_ref, aim_ref, bbd_ref, cbd_ref, dskip_ref,
                   wglu_ref, wpool_ref, pscale_ref, wout_ref, g2_ref, wup_ref, cw_ref,
                   cb_ref, wdown_ref, gf_ref,
                   y_hbm, sre_ref, sim_ref, pool_ref, conv_ref,
                   bu_ref, h_ref, act_ref, xbuf, ybuf, sem_in, sem_out):
    rows = tc * SUBLANES
    step = pl.program_id(0)
    slot = step % 2

    @pl.when(step == 0)
    def _():
        sre_ref[...] = jnp.zeros_like(sre_ref)
        sim_ref[...] = jnp.zeros_like(sim_ref)
        pool_ref[...] = jnp.zeros_like(pool_ref)
        conv_ref[...] = jnp.zeros_like(conv_ref)
        for cp in _block_copies(x_hbm, xbuf, sem_in, 0, 0, tc, False):
            cp.start()

    @pl.when(step + 1 < n_steps)
    def _():
        for cp in _block_copies(x_hbm, xbuf, sem_in, step + 1, 1 - slot, tc, False):
            cp.start()

    for cp in _block_copies(x_hbm, xbuf, sem_in, step, slot, tc, False):
        cp.wait()

    x = xbuf[slot].reshape(rows, D_MODEL)
    hb = _rms(x, g1_ref[...]).astype(BF16)
    proj = _dot(hb, win_ref[...])
    u = proj[:, :D_S5]
    v = proj[:, D_S5:]
    ub = u.astype(BF16)

    y_parts = []
    for k in range(S5_SPLIT):
        bu_ref[...] = _dot(ub[:, k * U_PER_SPLIT:(k + 1) * U_PER_SPLIT], bbd_ref[k])
        lanes = slice(k * ST_PER_SPLIT, (k + 1) * ST_PER_SPLIT)
        a_re = jnp.broadcast_to(are_ref[:, lanes], (SUBLANES, ST_PER_SPLIT))
        a_im = jnp.broadcast_to(aim_ref[:, lanes], (SUBLANES, ST_PER_SPLIT))

        def two_steps(i, carry, a_re=a_re, a_im=a_im):
            h_re, h_im = carry
            r0 = pl.multiple_of(i * (2 * SUBLANES), 2 * SUBLANES)
            res, ims = [], []
            for s in range(2):
                rs = pl.ds(r0 + s * SUBLANES, SUBLANES)
                n_re = a_re * h_re - a_im * h_im + bu_ref[rs, :ST_PER_SPLIT]
                n_im = a_re * h_im + a_im * h_re + bu_ref[rs, ST_PER_SPLIT:]
                h_re, h_im = n_re, n_im
                res.append(h_re)
                ims.append(h_im)
            pair = pl.ds(r0, 2 * SUBLANES)
            h_ref[pair, :ST_PER_SPLIT] = jnp.concatenate(res, axis=0).astype(BF16)
            h_ref[pair, ST_PER_SPLIT:] = jnp.concatenate(ims, axis=0).astype(BF16)
            return h_re, h_im

        h_re, h_im = lax.fori_loop(0, tc // 2, two_steps, (sre_ref[:, lanes], sim_ref[:, lanes]),
                                   unroll=True)
        sre_ref[:, lanes] = h_re
        sim_ref[:, lanes] = h_im
        half = rows // 2
        y_parts.append(jnp.concatenate(
            [_dot(h_ref[:half, :], cbd_ref[k]), _dot(h_ref[half:, :], cbd_ref[k])], axis=0))
    y_s5 = _s5_post(jnp.concatenate(y_parts, axis=1), u, dskip_ref, wglu_ref)

    halo = POOL_BUF * SUBLANES
    vfull = jnp.concatenate([pool_ref[...], v], axis=0)
    pool_ref[...] = vfull[rows:, :]
    t_idx = step * tc + (lax.broadcasted_iota(jnp.int32, (rows, POOL_CH), 0) >> 3)
    pooled = []
    for gi, w in enumerate(POOL_WINDOWS):
        s = vfull[:, gi * POOL_CH:(gi + 1) * POOL_CH]
        span = 1
        while span < w:
            sh = span * SUBLANES
            s = s[sh:, :] + s[:-sh, :]
            span *= 2
        first = (POOL_BUF - (w - 1)) * SUBLANES
        wsum = s[first:first + rows, :]
        cnt = jnp.minimum(t_idx + 1, w).astype(F32)
        pooled.append(wsum / cnt - v[:, gi * POOL_CH:(gi + 1) * POOL_CH])
    y_pool = _pool_project(pooled, wpool_ref, pscale_ref)

    x1 = x + _dot(y_s5.astype(BF16), wout_ref[:D_S5, :]) + _dot(y_pool.astype(BF16), wout_ref[D_S5:, :])

    h2b = _rms(x1, g2_ref[...]).astype(BF16)
    taps = (CONV_W - 1) * SUBLANES
    def ff_cols(j):
        return [slice(base + j * FF_CHUNK, base + (j + 1) * FF_CHUNK) for base in (0, D_FF)]

    def up_project(j):
        return [_dot(h2b, wup_ref[:, cols]) for cols in ff_cols(j)]

    for j in range(N_FF_CHUNKS):
        convd = []
        for hup, cols in zip(up_project(j), ff_cols(j)):
            full = jnp.concatenate([conv_ref[:, cols], hup], axis=0)
            conv_ref[:, cols] = hup[rows - taps:, :]
            c = cb_ref[:, cols]
            for kk in range(CONV_W):
                c = c + cw_ref[kk:kk + 1, cols] * full[kk * SUBLANES:kk * SUBLANES + rows, :]
            convd.append(c)
        act_ref[:, j * FF_CHUNK:(j + 1) * FF_CHUNK] = (_gelu(convd[0]) * convd[1]).astype(BF16)
    y = _rms(x1 + _dot(act_ref[...], wdown_ref[...]), gf_ref[...])

    @pl.when(step >= 2)
    def _():
        for cp in _block_copies(y_hbm, ybuf, sem_out, step - 2, slot, tc, True):
            cp.wait()

    ybuf[slot] = y.reshape(tc, SUBLANES, D_MODEL)
    for cp in _block_copies(y_hbm, ybuf, sem_out, step, slot, tc, True):
        cp.start()

    @pl.when(step == n_steps - 1)
    def _():
        if n_steps >= 2:
            for cp in _block_copies(y_hbm, ybuf, sem_out, step - 1, 1 - slot, tc, True):
                cp.wait()
        for cp in _block_copies(y_hbm, ybuf, sem_out, step, slot, tc, True):
            cp.wait()


N_BIG_WEIGHTS = 6
STAGE_ROWS = {D_MODEL: 256, D_S5: 256, POOL_CH: 256, 2 * D_FF: 128}
STAGE_SLOTS = 3
STAGE_LOOKAHEAD = STAGE_SLOTS - 1


def _convert_weights(conversions, stages, sem_stage, sem_wout):
    chunks = []
    ring_pos = [0] * len(stages)
    for w_idx, (w_hbm, _, sid, _) in enumerate(conversions):
        rows = stages[sid].shape[1]
        n_chunks = w_hbm.shape[0] // rows
        for c in range(n_chunks):
            chunks.append((w_idx, sid, ring_pos[sid] % STAGE_SLOTS, c * rows, rows, c == n_chunks - 1))
            ring_pos[sid] += 1

    def read(chunk):
        w_idx, sid, slot, r0, rows, _ = chunk
        return pltpu.make_async_copy(conversions[w_idx][0].at[pl.ds(r0, rows), :],
                                     stages[sid].at[slot], sem_stage.at[sid * STAGE_SLOTS + slot])

    write_backs = []
    started = 0
    for i, chunk in enumerate(chunks):
        while started < min(len(chunks), i + 1 + STAGE_LOOKAHEAD):
            read(chunks[started]).start()
            started += 1
        read(chunk).wait()
        w_idx, sid, slot, r0, rows, last = chunk
        _, wb_ref, _, wb_out = conversions[w_idx]
        wb_ref[r0:r0 + rows, :] = stages[sid][slot].astype(BF16)
        if last:
            cp = pltpu.make_async_copy(wb_ref, wb_out, sem_wout.at[w_idx])
            cp.start()
            write_backs.append(cp)
    return write_backs


def _sample_kernel(x_hbm, sre_in, sim_in, pool_hbm, conv_hbm,
                   g1_ref, are_ref, aim_ref, bbd_ref, cbd_ref, dskip_ref, pscale_ref, g2_ref,
                   cw_ref, cb_ref, gf_ref,
                   win_hbm, wglu_hbm, wpool_hbm, wout_hbm, wup_hbm, wdown_hbm,
                   y_hbm, sre_ref, sim_ref, pool_out, conv_out,
                   winb_out, wglub_out, wpoolb_out, woutb_out, wupb_out, wdownb_out,
                   win_ref, wglu_ref, wpool_ref, wout_ref, wup_ref, wdown_ref,
                   stage_model, stage_s5, stage_pool, stage_ff,
                   xbuf, ybuf, poolbuf, convbuf, hupbuf, vbuf,
                   sem_stage, sem_state, sem_wout, sem_sout):
    def state_row(hbm_ref, r):
        return hbm_ref.at[:, pl.ds(r, 1), :]

    in_copies = [pltpu.make_async_copy(x_hbm, xbuf, sem_state.at[0]),
                 pltpu.make_async_copy(pool_hbm, poolbuf, sem_state.at[1])]
    in_copies += [pltpu.make_async_copy(state_row(conv_hbm, r), convbuf.at[r], sem_state.at[2])
                  for r in range(CONV_W - 1)]
    for cp in in_copies:
        cp.start()

    stages = (stage_model, stage_s5, stage_pool, stage_ff)
    conversions = (
        (win_hbm, win_ref, 0, winb_out), (wglu_hbm, wglu_ref, 1, wglub_out),
        (wpool_hbm, wpool_ref, 2, wpoolb_out), (wout_hbm, wout_ref, 0, woutb_out),
        (wup_hbm, wup_ref, 3, wupb_out), (wdown_hbm, wdown_ref, 0, wdownb_out),
    )
    weight_out_copies = _convert_weights(conversions, stages, sem_stage, sem_wout)

    for cp in in_copies:
        cp.wait()

    x = xbuf[:, 0, :]
    hb = _rms(x, g1_ref[...]).astype(BF16)
    proj = _dot(hb, win_ref[...])
    u = proj[:, :D_S5]
    v = proj[:, D_S5:]
    ub = u.astype(BF16)

    y_parts = []
    for k in range(S5_SPLIT):
        bu = _dot(ub[:, k * U_PER_SPLIT:(k + 1) * U_PER_SPLIT], bbd_ref[k])
        lanes = slice(k * ST_PER_SPLIT, (k + 1) * ST_PER_SPLIT)
        a_re = are_ref[:, lanes]
        a_im = aim_ref[:, lanes]
        h_re0 = sre_in[:, lanes]
        h_im0 = sim_in[:, lanes]
        h_re = a_re * h_re0 - a_im * h_im0 + bu[:, :ST_PER_SPLIT]
        h_im = a_re * h_im0 + a_im * h_re0 + bu[:, ST_PER_SPLIT:]
        sre_ref[:, lanes] = h_re
        sim_ref[:, lanes] = h_im
        hcat = jnp.concatenate([h_re, h_im], axis=1).astype(BF16)
        y_parts.append(_dot(hcat, cbd_ref[k]))
    y_s5 = _s5_post(jnp.concatenate(y_parts, axis=1), u, dskip_ref, wglu_ref)

    pooled = []
    for gi, w in enumerate(POOL_WINDOWS):
        lanes = slice(gi * POOL_CH, (gi + 1) * POOL_CH)
        vc = v[:, lanes]
        wsum = vc
        for back in range(1, w):
            wsum = wsum + poolbuf[POOL_BUF - back, :, lanes]
        pooled.append(wsum / float(w) - vc)
    y_pool = _pool_project(pooled, wpool_ref, pscale_ref)
    vbuf[...] = v
    state_out_copies = [
        pltpu.make_async_copy(poolbuf.at[pl.ds(1, POOL_BUF - 1)], pool_out.at[pl.ds(0, POOL_BUF - 1)],
                              sem_sout.at[0]),
        pltpu.make_async_copy(vbuf, pool_out.at[POOL_BUF - 1], sem_sout.at[1]),
        pltpu.make_async_copy(convbuf.at[1], state_row(conv_out, 0), sem_sout.at[2]),
    ]
    for cp in state_out_copies:
        cp.start()

    x1 = x + _dot(y_s5.astype(BF16), wout_ref[:D_S5, :]) + _dot(y_pool.astype(BF16), wout_ref[D_S5:, :])

    h2b = _rms(x1, g2_ref[...]).astype(BF16)
    acc = None
    for j in range(N_FF_CHUNKS):
        convd = []
        for base in (0, D_FF):
            cols = slice(base + j * FF_CHUNK, base + (j + 1) * FF_CHUNK)
            hup = _dot(h2b, wup_ref[:, cols])
            hupbuf[:, 0, cols] = hup
            convd.append(cb_ref[:, cols] + cw_ref[0:1, cols] * convbuf[0, :, 0, cols]
                         + cw_ref[1:2, cols] * convbuf[1, :, 0, cols] + cw_ref[2:3, cols] * hup)
        act = (_gelu(convd[0]) * convd[1]).astype(BF16)
        part = _dot(act, wdown_ref[j * FF_CHUNK:(j + 1) * FF_CHUNK, :])
        acc = part if acc is None else acc + part
    ybuf[:, 0, :] = _rms(x1 + acc, gf_ref[...])

    tail_copies = [pltpu.make_async_copy(hupbuf, state_row(conv_out, 1), sem_sout.at[3]),
                   pltpu.make_async_copy(ybuf, y_hbm, sem_sout.at[4])]
    for cp in tail_copies:
        cp.start()
    for cp in state_out_copies + tail_copies + weight_out_copies:
        cp.wait()


def _s5_tables(a_re, a_im, log_dt, b_re, b_im, c_re, c_im):
    dt = jnp.exp(log_dt)[:, None]
    mag = jnp.exp(dt * a_re)
    abar_re = mag * jnp.cos(dt * a_im)
    abar_im = mag * jnp.sin(dt * a_im)
    nr, ni = abar_re - 1.0, abar_im
    den = a_re * a_re + a_im * a_im
    f_re = ((nr * a_re + ni * a_im) / den)[:, :, None]
    f_im = ((ni * a_re - nr * a_im) / den)[:, :, None]
    bbar_re = f_re * b_re - f_im * b_im
    bbar_im = f_re * b_im + f_im * b_re
    u_group = lax.broadcasted_iota(jnp.int32, (U_PER_SPLIT, ST_PER_SPLIT), 0) // S5_CH
    st_group = lax.broadcasted_iota(jnp.int32, (U_PER_SPLIT, ST_PER_SPLIT), 1) // S5_STATE
    diag = u_group == st_group

    def in_table(b):
        b = b.reshape(S5_SPLIT, GROUPS_PER_SPLIT, S5_STATE, S5_CH)
        row = jnp.transpose(b, (0, 3, 1, 2)).reshape(S5_SPLIT, 1, S5_CH, ST_PER_SPLIT)
        rep = jnp.broadcast_to(row, (S5_SPLIT, GROUPS_PER_SPLIT, S5_CH, ST_PER_SPLIT))
        return jnp.where(diag, rep.reshape(S5_SPLIT, U_PER_SPLIT, ST_PER_SPLIT), 0.0)

    def out_table(c):
        c = c.reshape(S5_SPLIT, GROUPS_PER_SPLIT, S5_CH, S5_STATE)
        col = jnp.transpose(c, (0, 1, 3, 2)).reshape(S5_SPLIT, ST_PER_SPLIT, S5_CH)
        rep = jnp.tile(col, (1, 1, GROUPS_PER_SPLIT))
        return jnp.where(diag.T, rep, 0.0)

    bbd = jnp.concatenate([in_table(bbar_re), in_table(bbar_im)], axis=2)
    cbd = jnp.concatenate([out_table(c_re), out_table(-c_im)], axis=1)
    return abar_re.reshape(1, -1), abar_im.reshape(1, -1), bbd.astype(BF16), cbd.astype(BF16)


def _vmem_spec():
    return pl.BlockSpec(memory_space=pltpu.VMEM)


PROMPT_TC = 64
VMEM_LIMIT_BYTES = 60 * 1024 * 1024


def kernel(x_prompt, x_sample, state_s5_re, state_s5_im, state_pool, state_ffn_conv, norm_mix_g, w_in, s5_a_re, s5_a_im, s5_log_dt, s5_b_re, s5_b_im, s5_c_re, s5_c_im, s5_d, s5_w_glu, pool_w, pool_scale, w_out, norm_ffn_g, ffn_w_up, ffn_conv_w, ffn_conv_b, ffn_w_down, norm_final_g):
    nb, seq, _ = x_prompt.shape
    ns = x_sample.shape[0]
    assert nb == SUBLANES and seq % PROMPT_TC == 0 and x_sample.shape[1] == 1
    assert norm_mix_g.shape[0] == 1, "single layer"

    a_re, a_im, bbd, cbd = _s5_tables(s5_a_re[0], s5_a_im[0], s5_log_dt[0], s5_b_re[0],
                                      s5_b_im[0], s5_c_re[0], s5_c_im[0])
    n_states = S5_GROUPS * S5_STATE
    cparams = dict(vmem_limit_bytes=VMEM_LIMIT_BYTES)
    g1 = norm_mix_g[0].reshape(1, D_MODEL)
    dskip = s5_d[0].reshape(1, D_S5)
    pscale = pool_scale[0].reshape(1, D_POOL)
    g2 = norm_ffn_g[0].reshape(1, D_MODEL)
    cw = ffn_conv_w[0]
    cb = ffn_conv_b[0].reshape(1, 2 * D_FF)
    gf = norm_final_g.reshape(1, D_MODEL)

    big_f32 = (w_in[0], s5_w_glu[0], pool_w[0].reshape(len(POOL_WINDOWS) * POOL_CH, POOL_CH),
               w_out[0], ffn_w_up[0], ffn_w_down[0])
    small = (g1, a_re, a_im, bbd, cbd, dskip, pscale, g2, cw, cb, gf)
    any_spec = pl.BlockSpec(memory_space=pl.ANY)
    stage_widths = (D_MODEL, D_S5, POOL_CH, 2 * D_FF)
    pool_in = jnp.transpose(state_pool[0], (1, 0, 2))
    sample_out = pl.pallas_call(
        _sample_kernel,
        in_specs=[any_spec] + [_vmem_spec()] * 2 + [any_spec] * 2 + [_vmem_spec()] * len(small)
                 + [any_spec] * N_BIG_WEIGHTS,
        out_specs=[any_spec] + [_vmem_spec()] * 2 + [any_spec] * (2 + N_BIG_WEIGHTS),
        out_shape=[
            jax.ShapeDtypeStruct((ns, 1, D_MODEL), F32),
            jax.ShapeDtypeStruct((ns, n_states), F32),
            jax.ShapeDtypeStruct((ns, n_states), F32),
            jax.ShapeDtypeStruct((POOL_BUF, ns, D_POOL), F32),
            jax.ShapeDtypeStruct((ns, CONV_W - 1, 2 * D_FF), F32),
        ] + [jax.ShapeDtypeStruct(w.shape, BF16) for w in big_f32],
        scratch_shapes=[pltpu.VMEM(w.shape, BF16) for w in big_f32]
        + [pltpu.VMEM((STAGE_SLOTS, STAGE_ROWS[width], width), F32) for width in stage_widths] + [
            pltpu.VMEM((ns, 1, D_MODEL), F32),
            pltpu.VMEM((ns, 1, D_MODEL), F32),
            pltpu.VMEM((POOL_BUF, ns, D_POOL), F32),
            pltpu.VMEM((CONV_W - 1, ns, 1, 2 * D_FF), F32),
            pltpu.VMEM((ns, 1, 2 * D_FF), F32),
            pltpu.VMEM((ns, D_POOL), F32),
            pltpu.SemaphoreType.DMA((len(stage_widths) * STAGE_SLOTS,)),
            pltpu.SemaphoreType.DMA((3,)),
            pltpu.SemaphoreType.DMA((N_BIG_WEIGHTS,)),
            pltpu.SemaphoreType.DMA((5,)),
        ],
        compiler_params=pltpu.CompilerParams(**cparams),
        name="sample_layer",
    )(x_sample, state_s5_re[0].reshape(ns, n_states), state_s5_im[0].reshape(ns, n_states),
      pool_in, state_ffn_conv[0], *small, *big_f32)
    ys, s_re, s_im, s_pool, s_conv = sample_out[:5]
    win_b, wglu_b, wpool_b, wout_b, wup_b, wdown_b = sample_out[5:]
    weights = (g1, win_b, a_re, a_im, bbd, cbd, dskip, wglu_b, wpool_b, pscale, wout_b, g2,
               wup_b, cw, cb, wdown_b, gf)

    tc = PROMPT_TC
    rows = tc * SUBLANES
    const = lambda i: (0, 0)
    y_prompt, p_re, p_im, p_pool, p_conv = pl.pallas_call(
        functools.partial(_prompt_kernel, tc, seq // tc),
        grid=(seq // tc,),
        in_specs=[pl.BlockSpec(memory_space=pl.ANY)] + [_vmem_spec()] * len(weights),
        out_specs=[
            pl.BlockSpec(memory_space=pl.ANY),
            pl.BlockSpec((SUBLANES, n_states), const),
            pl.BlockSpec((SUBLANES, n_states), const),
            pl.BlockSpec((POOL_BUF * SUBLANES, D_POOL), const),
            pl.BlockSpec(((CONV_W - 1) * SUBLANES, 2 * D_FF), const),
        ],
        out_shape=[
            jax.ShapeDtypeStruct((nb, seq, D_MODEL), F32),
            jax.ShapeDtypeStruct((SUBLANES, n_states), F32),
            jax.ShapeDtypeStruct((SUBLANES, n_states), F32),
            jax.ShapeDtypeStruct((POOL_BUF * SUBLANES, D_POOL), F32),
            jax.ShapeDtypeStruct(((CONV_W - 1) * SUBLANES, 2 * D_FF), F32),
        ],
        scratch_shapes=[
            pltpu.VMEM((rows, 2 * ST_PER_SPLIT), F32),
            pltpu.VMEM((rows, 2 * ST_PER_SPLIT), BF16),
            pltpu.VMEM((rows, D_FF), BF16),
            pltpu.VMEM((2, tc, SUBLANES, D_MODEL), F32),
            pltpu.VMEM((2, tc, SUBLANES, D_MODEL), F32),
            pltpu.SemaphoreType.DMA((2,)),
            pltpu.SemaphoreType.DMA((2,)),
        ],
        compiler_params=pltpu.CompilerParams(dimension_semantics=("arbitrary",), **cparams),
        name="prompt_layer",
    )(x_prompt, *weights)
    new_pool_p = jnp.transpose(p_pool.reshape(POOL_BUF, nb, D_POOL), (1, 0, 2))[None]
    new_conv_p = jnp.transpose(p_conv.reshape(CONV_W - 1, nb, 2 * D_FF), (1, 0, 2))[None]
    new_re_p = p_re.reshape(1, nb, S5_GROUPS, S5_STATE)
    new_im_p = p_im.reshape(1, nb, S5_GROUPS, S5_STATE)

    return (y_prompt, ys, new_re_p, new_im_p, new_pool_p, new_conv_p,
            s_re.reshape(1, ns, S5_GROUPS, S5_STATE), s_im.reshape(1, ns, S5_GROUPS, S5_STATE),
            jnp.transpose(s_pool, (1, 0, 2))[None], s_conv[None])
```

```python
import functools
import math

import numpy as np
import jax
import jax.numpy as jnp
from jax import lax
from jax.experimental import pallas as pl
from jax.experimental.pallas import tpu as pltpu

D_MODEL = 1024
D_S5 = 512
S5_CH = 16
S5_GROUPS = 32
S5_STATE = 64
D_POOL = 512
POOL_WINDOWS = (2, 4, 8, 16)
POOL_CH = 128
POOL_BUF = 15
D_FF = 2816
CONV_W = 3
EPS = 1e-6

SUBLANES = 8
S5_SPLIT = 2
GROUPS_PER_SPLIT = S5_GROUPS // S5_SPLIT
U_PER_SPLIT = GROUPS_PER_SPLIT * S5_CH
ST_PER_SPLIT = GROUPS_PER_SPLIT * S5_STATE
FF_CHUNK = 256
N_FF_CHUNKS = D_FF // FF_CHUNK
SQRT_HALF = float(np.sqrt(0.5).astype(np.float32))

BF16 = jnp.bfloat16
F32 = jnp.float32


def _dot(a, b):
    return jnp.dot(a, b, preferred_element_type=F32)


def _rms(x, g):
    ms = jnp.mean(x * x, axis=-1, keepdims=True)
    return x * lax.rsqrt(ms + EPS) * g


def _gelu(x):
    return 0.5 * x * (1.0 + lax.erf(x * SQRT_HALF))


def _s5_post(y_lin, u, dskip_ref, wglu_ref):
    y = _gelu(y_lin + dskip_ref[...] * u)
    return y * jax.nn.sigmoid(_dot(y.astype(BF16), wglu_ref[...]))


def _pool_project(pooled_cols, wpool_ref, pscale_ref):
    outs = []
    for gi in range(len(POOL_WINDOWS)):
        z = _dot(pooled_cols[gi].astype(BF16), wpool_ref[gi * POOL_CH:(gi + 1) * POOL_CH, :])
        outs.append(z * pscale_ref[:, gi * POOL_CH:(gi + 1) * POOL_CH])
    return jnp.concatenate(outs, axis=1)


def _block_copies(hbm_ref, buf_ref, sem_ref, block, slot, tc, to_hbm):
    copies = []
    for n in range(SUBLANES):
        hbm = hbm_ref.at[n, pl.ds(block * tc, tc), :]
        vmem = buf_ref.at[slot, :, n, :]
        src, dst = (vmem, hbm) if to_hbm else (hbm, vmem)
        copies.append(pltpu.make_async_copy(src, dst, sem_ref.at[slot]))
    return copies


def _prompt_kernel(tc, n_steps,
                   x_hbm, g1_ref, win_ref, are_ref, aim_ref, bbd_ref, cbd_ref, dskip_ref,
                   wglu_ref, wpool_ref, pscale_ref, wout_ref, g2_ref, wup_ref, cw_ref,
                   cb_ref, wdown_ref, gf_ref,
                   y_hbm, sre_ref, sim_ref, pool_ref, conv_ref,
                   h_ref, act_ref, xbuf, ybuf, sem_in, sem_out):
    rows = tc * SUBLANES
    step = pl.program_id(0)
    slot = step % 2

    @pl.when(step == 0)
    def _():
        sre_ref[...] = jnp.zeros_like(sre_ref)
        sim_ref[...] = jnp.zeros_like(sim_ref)
        pool_ref[...] = jnp.zeros_like(pool_ref)
        conv_ref[...] = jnp.zeros_like(conv_ref)
        for cp in _block_copies(x_hbm, xbuf, sem_in, 0, 0, tc, False):
            cp.start()

    @pl.when(step + 1 < n_steps)
    def _():
        for cp in _block_copies(x_hbm, xbuf, sem_in, step + 1, 1 - slot, tc, False):
            cp.start()

    @pl.when(step >= 2)
    def _():
        for cp in _block_copies(y_hbm, ybuf, sem_out, step - 2, slot, tc, True):
            cp.wait()

    for cp in _block_copies(x_hbm, xbuf, sem_in, step, slot, tc, False):
        cp.wait()

    x = xbuf[slot].reshape(rows, D_MODEL)
    hb = _rms(x, g1_ref[...]).astype(BF16)
    proj = _dot(hb, win_ref[...])
    u = proj[:, :D_S5]
    v = proj[:, D_S5:]
    ub = u.astype(BF16)

    bus = [_dot(ub[:, k * U_PER_SPLIT:(k + 1) * U_PER_SPLIT], bbd_ref[k]) for k in range(S5_SPLIT)]
    y_parts = []
    for k in range(S5_SPLIT):
        bu = bus[k]
        lanes = slice(k * ST_PER_SPLIT, (k + 1) * ST_PER_SPLIT)
        a_re = jnp.broadcast_to(are_ref[:, lanes], (SUBLANES, ST_PER_SPLIT))
        a_im = jnp.broadcast_to(aim_ref[:, lanes], (SUBLANES, ST_PER_SPLIT))
        h_re, h_im = sre_ref[:, lanes], sim_ref[:, lanes]
        for i in range(tc // 2):
            r0 = i * 2 * SUBLANES
            res, ims = [], []
            for s in range(2):
                rs = slice(r0 + s * SUBLANES, r0 + (s + 1) * SUBLANES)
                n_re = a_re * h_re - a_im * h_im + bu[rs, :ST_PER_SPLIT]
                n_im = a_re * h_im + a_im * h_re + bu[rs, ST_PER_SPLIT:]
                h_re, h_im = n_re, n_im
                res.append(h_re)
                ims.append(h_im)
            pair = slice(r0, r0 + 2 * SUBLANES)
            h_ref[k, pair, :ST_PER_SPLIT] = jnp.concatenate(res, axis=0).astype(BF16)
            h_ref[k, pair, ST_PER_SPLIT:] = jnp.concatenate(ims, axis=0).astype(BF16)
        sre_ref[:, lanes] = h_re
        sim_ref[:, lanes] = h_im
        half = rows // 2
        y_parts.append(jnp.concatenate(
            [_dot(h_ref[k, :half, :], cbd_ref[k]), _dot(h_ref[k, half:, :], cbd_ref[k])], axis=0))
    y_s5 = _s5_post(jnp.concatenate(y_parts, axis=1), u, dskip_ref, wglu_ref)

    halo = POOL_BUF * SUBLANES
    vfull = jnp.concatenate([pool_ref[...], v], axis=0)
    pool_ref[...] = vfull[rows:, :]
    t_idx = step * tc + (lax.broadcasted_iota(jnp.int32, (rows, POOL_CH), 0) >> 3)
    pooled = []
    for gi, w in enumerate(POOL_WINDOWS):
        s = vfull[:, gi * POOL_CH:(gi + 1) * POOL_CH]
        span = 1
        while span < w:
            sh = span * SUBLANES
            s = s[sh:, :] + s[:-sh, :]
            span *= 2
        first = (POOL_BUF - (w - 1)) * SUBLANES
        wsum = s[first:first + rows, :]
        cnt = jnp.minimum(t_idx + 1, w).astype(F32)
        pooled.append(wsum / cnt - v[:, gi * POOL_CH:(gi + 1) * POOL_CH])
    y_pool = _pool_project(pooled, wpool_ref, pscale_ref)

    x1 = x + _dot(y_s5.astype(BF16), wout_ref[:D_S5, :]) + _dot(y_pool.astype(BF16), wout_ref[D_S5:, :])

    h2b = _rms(x1, g2_ref[...]).astype(BF16)
    taps = (CONV_W - 1) * SUBLANES
    for j in range(N_FF_CHUNKS):
        convd = []
        for base in (0, D_FF):
            cols = slice(base + j * FF_CHUNK, base + (j + 1) * FF_CHUNK)
            hup = _dot(h2b, wup_ref[:, cols])
            full = jnp.concatenate([conv_ref[:, cols], hup], axis=0)
            conv_ref[:, cols] = hup[rows - taps:, :]
            c = cb_ref[:, cols]
            for kk in range(CONV_W):
                c = c + cw_ref[kk:kk + 1, cols] * full[kk * SUBLANES:kk * SUBLANES + rows, :]
            convd.append(c)
        act_ref[:, j * FF_CHUNK:(j + 1) * FF_CHUNK] = (_gelu(convd[0]) * convd[1]).astype(BF16)
    y = _rms(x1 + _dot(act_ref[...], wdown_ref[...]), gf_ref[...])

    ybuf[slot] = y.reshape(tc, SUBLANES, D_MODEL)
    for cp in _block_copies(y_hbm, ybuf, sem_out, step, slot, tc, True):
        cp.start()

    @pl.when(step == n_steps - 1)
    def _():
        if n_steps >= 2:
            for cp in _block_copies(y_hbm, ybuf, sem_out, step - 1, 1 - slot, tc, True):
                cp.wait()
        for cp in _block_copies(y_hbm, ybuf, sem_out, step, slot, tc, True):
            cp.wait()


N_BIG_WEIGHTS = 6
STAGE_ROWS = {D_MODEL: 256, D_S5: 256, POOL_CH: 256, 2 * D_FF: 128}
STAGE_SLOTS = 3
STAGE_LOOKAHEAD = STAGE_SLOTS - 1


def _convert_weights(conversions, stages, sem_stage, sem_wout):
    chunks = []
    ring_pos = [0] * len(stages)
    for w_idx, (w_hbm, _, sid, _) in enumerate(conversions):
        rows = stages[sid].shape[1]
        n_chunks = w_hbm.shape[0] // rows
        for c in range(n_chunks):
            chunks.append((w_idx, sid, ring_pos[sid] % STAGE_SLOTS, c * rows, rows, c == n_chunks - 1))
            ring_pos[sid] += 1

    def read(chunk):
        w_idx, sid, slot, r0, rows, _ = chunk
        return pltpu.make_async_copy(conversions[w_idx][0].at[pl.ds(r0, rows), :],
                                     stages[sid].at[slot], sem_stage.at[sid * STAGE_SLOTS + slot])

    write_backs = []
    started = 0
    for i, chunk in enumerate(chunks):
        while started < min(len(chunks), i + 1 + STAGE_LOOKAHEAD):
            read(chunks[started]).start()
            started += 1
        read(chunk).wait()
        w_idx, sid, slot, r0, rows, last = chunk
        _, wb_ref, _, wb_out = conversions[w_idx]
        wb_ref[r0:r0 + rows, :] = stages[sid][slot].astype(BF16)
        if last:
            cp = pltpu.make_async_copy(wb_ref, wb_out, sem_wout.at[w_idx])
            cp.start()
            write_backs.append(cp)
    return write_backs


def _sample_kernel(x_hbm, sre_in, sim_in, pool_hbm, conv_hbm,
                   g1_ref, are_ref, aim_ref, bbd_ref, cbd_ref, dskip_ref, pscale_ref, g2_ref,
                   cw_ref, cb_ref, gf_ref,
                   win_hbm, wglu_hbm, wpool_hbm, wout_hbm, wup_hbm, wdown_hbm,
                   y_hbm, sre_ref, sim_ref, pool_out, conv_out,
                   winb_out, wglub_out, wpoolb_out, woutb_out, wupb_out, wdownb_out,
                   win_ref, wglu_ref, wpool_ref, wout_ref, wup_ref, wdown_ref,
                   stage_model, stage_s5, stage_pool, stage_ff,
                   xbuf, ybuf, poolbuf, convbuf, hupbuf, vbuf,
                   sem_stage, sem_state, sem_wout, sem_sout):
    def state_row(hbm_ref, r):
        return hbm_ref.at[:, pl.ds(r, 1), :]

    in_copies = [pltpu.make_async_copy(x_hbm, xbuf, sem_state.at[0]),
                 pltpu.make_async_copy(pool_hbm, poolbuf, sem_state.at[1])]
    in_copies += [pltpu.make_async_copy(state_row(conv_hbm, r), convbuf.at[r], sem_state.at[2])
                  for r in range(CONV_W - 1)]
    for cp in in_copies:
        cp.start()

    stages = (stage_model, stage_s5, stage_pool, stage_ff)
    conversions = (
        (win_hbm, win_ref, 0, winb_out), (wglu_hbm, wglu_ref, 1, wglub_out),
        (wpool_hbm, wpool_ref, 2, wpoolb_out), (wout_hbm, wout_ref, 0, woutb_out),
        (wup_hbm, wup_ref, 3, wupb_out), (wdown_hbm, wdown_ref, 0, wdownb_out),
    )
    weight_out_copies = _convert_weights(conversions, stages, sem_stage, sem_wout)

    for cp in in_copies:
        cp.wait()

    x = xbuf[:, 0, :]
    hb = _rms(x, g1_ref[...]).astype(BF16)
    proj = _dot(hb, win_ref[...])
    u = proj[:, :D_S5]
    v = proj[:, D_S5:]
    ub = u.astype(BF16)

    y_parts = []
    for k in range(S5_SPLIT):
        bu = _dot(ub[:, k * U_PER_SPLIT:(k + 1) * U_PER_SPLIT], bbd_ref[k])
        lanes = slice(k * ST_PER_SPLIT, (k + 1) * ST_PER_SPLIT)
        a_re = are_ref[:, lanes]
        a_im = aim_ref[:, lanes]
        h_re0 = sre_in[:, lanes]
        h_im0 = sim_in[:, lanes]
        h_re = a_re * h_re0 - a_im * h_im0 + bu[:, :ST_PER_SPLIT]
        h_im = a_re * h_im0 + a_im * h_re0 + bu[:, ST_PER_SPLIT:]
        sre_ref[:, lanes] = h_re
        sim_ref[:, lanes] = h_im
        hcat = jnp.concatenate([h_re, h_im], axis=1).astype(BF16)
        y_parts.append(_dot(hcat, cbd_ref[k]))
    y_s5 = _s5_post(jnp.concatenate(y_parts, axis=1), u, dskip_ref, wglu_ref)

    pooled = []
    for gi, w in enumerate(POOL_WINDOWS):
        lanes = slice(gi * POOL_CH, (gi + 1) * POOL_CH)
        vc = v[:, lanes]
        wsum = vc
        for back in range(1, w):
            wsum = wsum + poolbuf[POOL_BUF - back, :, lanes]
        pooled.append(wsum / float(w) - vc)
    y_pool = _pool_project(pooled, wpool_ref, pscale_ref)
    vbuf[...] = v
    state_out_copies = [
        pltpu.make_async_copy(poolbuf.at[pl.ds(1, POOL_BUF - 1)], pool_out.at[pl.ds(0, POOL_BUF - 1)],
                              sem_sout.at[0]),
        pltpu.make_async_copy(vbuf, pool_out.at[POOL_BUF - 1], sem_sout.at[1]),
        pltpu.make_async_copy(convbuf.at[1], state_row(conv_out, 0), sem_sout.at[2]),
    ]
    for cp in state_out_copies:
        cp.start()

    x1 = x + _dot(y_s5.astype(BF16), wout_ref[:D_S5, :]) + _dot(y_pool.astype(BF16), wout_ref[D_S5:, :])

    h2b = _rms(x1, g2_ref[...]).astype(BF16)
    acc = None
    for j in range(N_FF_CHUNKS):
        convd = []
        for base in (0, D_FF):
            cols = slice(base + j * FF_CHUNK, base + (j + 1) * FF_CHUNK)
            hup = _dot(h2b, wup_ref[:, cols])
            hupbuf[:, 0, cols] = hup
            convd.append(cb_ref[:, cols] + cw_ref[0:1, cols] * convbuf[0, :, 0, cols]
                         + cw_ref[1:2, cols] * convbuf[1, :, 0, cols] + cw_ref[2:3, cols] * hup)
        act = (_gelu(convd[0]) * convd[1]).astype(BF16)
        part = _dot(act, wdown_ref[j * FF_CHUNK:(j + 1) * FF_CHUNK, :])
        acc = part if acc is None else acc + part
    ybuf[:, 0, :] = _rms(x1 + acc, gf_ref[...])

    tail_copies = [pltpu.make_async_copy(hupbuf, state_row(conv_out, 1), sem_sout.at[3]),
                   pltpu.make_async_copy(ybuf, y_hbm, sem_sout.at[4])]
    for cp in tail_copies:
        cp.start()
    for cp in state_out_copies + tail_copies + weight_out_copies:
        cp.wait()


def _s5_tables(a_re, a_im, log_dt, b_re, b_im, c_re, c_im):
    dt = jnp.exp(log_dt)[:, None]
    mag = jnp.exp(dt * a_re)
    abar_re = mag * jnp.cos(dt * a_im)
    abar_im = mag * jnp.sin(dt * a_im)
    nr, ni = abar_re - 1.0, abar_im
    den = a_re * a_re + a_im * a_im
    f_re = ((nr * a_re + ni * a_im) / den)[:, :, None]
    f_im = ((ni * a_re - nr * a_im) / den)[:, :, None]
    bbar_re = f_re * b_re - f_im * b_im
    bbar_im = f_re * b_im + f_im * b_re
    u_group = lax.broadcasted_iota(jnp.int32, (U_PER_SPLIT, ST_PER_SPLIT), 0) // S5_CH
    st_group = lax.broadcasted_iota(jnp.int32, (U_PER_SPLIT, ST_PER_SPLIT), 1) // S5_STATE
    diag = u_group == st_group

    def in_table(b):
        b = b.reshape(S5_SPLIT, GROUPS_PER_SPLIT, S5_STATE, S5_CH)
        row = jnp.transpose(b, (0, 3, 1, 2)).reshape(S5_SPLIT, 1, S5_CH, ST_PER_SPLIT)
        rep = jnp.broadcast_to(row, (S5_SPLIT, GROUPS_PER_SPLIT, S5_CH, ST_PER_SPLIT))
        return jnp.where(diag, rep.reshape(S5_SPLIT, U_PER_SPLIT, ST_PER_SPLIT), 0.0)

    def out_table(c):
        c = c.reshape(S5_SPLIT, GROUPS_PER_SPLIT, S5_CH, S5_STATE)
        col = jnp.transpose(c, (0, 1, 3, 2)).reshape(S5_SPLIT, ST_PER_SPLIT, S5_CH)
        rep = jnp.tile(col, (1, 1, GROUPS_PER_SPLIT))
        return jnp.where(diag.T, rep, 0.0)

    bbd = jnp.concatenate([in_table(bbar_re), in_table(bbar_im)], axis=2)
    cbd = jnp.concatenate([out_table(c_re), out_table(-c_im)], axis=1)
    return abar_re.reshape(1, -1), abar_im.reshape(1, -1), bbd.astype(BF16), cbd.astype(BF16)


def _vmem_spec():
    return pl.BlockSpec(memory_space=pltpu.VMEM)


PROMPT_TC = 64
VMEM_LIMIT_BYTES = 60 * 1024 * 1024


def kernel(x_prompt, x_sample, state_s5_re, state_s5_im, state_pool, state_ffn_conv, norm_mix_g, w_in, s5_a_re, s5_a_im, s5_log_dt, s5_b_re, s5_b_im, s5_c_re, s5_c_im, s5_d, s5_w_glu, pool_w, pool_scale, w_out, norm_ffn_g, ffn_w_up, ffn_conv_w, ffn_conv_b, ffn_w_down, norm_final_g):
    nb, seq, _ = x_prompt.shape
    ns = x_sample.shape[0]
    assert nb == SUBLANES and seq % PROMPT_TC == 0 and x_sample.shape[1] == 1
    assert norm_mix_g.shape[0] == 1, "single layer"

    a_re, a_im, bbd, cbd = _s5_tables(s5_a_re[0], s5_a_im[0], s5_log_dt[0], s5_b_re[0],
                                      s5_b_im[0], s5_c_re[0], s5_c_im[0])
    n_states = S5_GROUPS * S5_STATE
    cparams = dict(vmem_limit_bytes=VMEM_LIMIT_BYTES)
    g1 = norm_mix_g[0].reshape(1, D_MODEL)
    dskip = s5_d[0].reshape(1, D_S5)
    pscale = pool_scale[0].reshape(1, D_POOL)
    g2 = norm_ffn_g[0].reshape(1, D_MODEL)
    cw = ffn_conv_w[0]
    cb = ffn_conv_b[0].reshape(1, 2 * D_FF)
    gf = norm_final_g.reshape(1, D_MODEL)

    big_f32 = (w_in[0], s5_w_glu[0], pool_w[0].reshape(len(POOL_WINDOWS) * POOL_CH, POOL_CH),
               w_out[0], ffn_w_up[0], ffn_w_down[0])
    small = (g1, a_re, a_im, bbd, cbd, dskip, pscale, g2, cw, cb, gf)
    any_spec = pl.BlockSpec(memory_space=pl.ANY)
    stage_widths = (D_MODEL, D_S5, POOL_CH, 2 * D_FF)
    pool_in = jnp.transpose(state_pool[0], (1, 0, 2))
    sample_out = pl.pallas_call(
        _sample_kernel,
        in_specs=[any_spec] + [_vmem_spec()] * 2 + [any_spec] * 2 + [_vmem_spec()] * len(small)
                 + [any_spec] * N_BIG_WEIGHTS,
        out_specs=[any_spec] + [_vmem_spec()] * 2 + [any_spec] * (2 + N_BIG_WEIGHTS),
        out_shape=[
            jax.ShapeDtypeStruct((ns, 1, D_MODEL), F32),
            jax.ShapeDtypeStruct((ns, n_states), F32),
            jax.ShapeDtypeStruct((ns, n_states), F32),
            jax.ShapeDtypeStruct((POOL_BUF, ns, D_POOL), F32),
            jax.ShapeDtypeStruct((ns, CONV_W - 1, 2 * D_FF), F32),
        ] + [jax.ShapeDtypeStruct(w.shape, BF16) for w in big_f32],
        scratch_shapes=[pltpu.VMEM(w.shape, BF16) for w in big_f32]
        + [pltpu.VMEM((STAGE_SLOTS, STAGE_ROWS[width], width), F32) for width in stage_widths] + [
            pltpu.VMEM((ns, 1, D_MODEL), F32),
            pltpu.VMEM((ns, 1, D_MODEL), F32),
            pltpu.VMEM((POOL_BUF, ns, D_POOL), F32),
            pltpu.VMEM((CONV_W - 1, ns, 1, 2 * D_FF), F32),
            pltpu.VMEM((ns, 1, 2 * D_FF), F32),
            pltpu.VMEM((ns, D_POOL), F32),
            pltpu.SemaphoreType.DMA((len(stage_widths) * STAGE_SLOTS,)),
            pltpu.SemaphoreType.DMA((3,)),
            pltpu.SemaphoreType.DMA((N_BIG_WEIGHTS,)),
            pltpu.SemaphoreType.DMA((5,)),
        ],
        compiler_params=pltpu.CompilerParams(**cparams),
        name="sample_layer",
    )(x_sample, state_s5_re[0].reshape(ns, n_states), state_s5_im[0].reshape(ns, n_states),
      pool_in, state_ffn_conv[0], *small, *big_f32)
    ys, s_re, s_im, s_pool, s_conv = sample_out[:5]
    win_b, wglu_b, wpool_b, wout_b, wup_b, wdown_b = sample_out[5:]
    weights = (g1, win_b, a_re, a_im, bbd, cbd, dskip, wglu_b, wpool_b, pscale, wout_b, g2,
               wup_b, cw, cb, wdown_b, gf)

    tc = PROMPT_TC
    rows = tc * SUBLANES
    const = lambda i: (0, 0)
    y_prompt, p_re, p_im, p_pool, p_conv = pl.pallas_call(
        functools.partial(_prompt_kernel, tc, seq // tc),
        grid=(seq // tc,),
        in_specs=[pl.BlockSpec(memory_space=pl.ANY)] + [_vmem_spec()] * len(weights),
        out_specs=[
            pl.BlockSpec(memory_space=pl.ANY),
            pl.BlockSpec((SUBLANES, n_states), const),
            pl.BlockSpec((SUBLANES, n_states), const),
            pl.BlockSpec((POOL_BUF * SUBLANES, D_POOL), const),
            pl.BlockSpec(((CONV_W - 1) * SUBLANES, 2 * D_FF), const),
        ],
        out_shape=[
            jax.ShapeDtypeStruct((nb, seq, D_MODEL), F32),
            jax.ShapeDtypeStruct((SUBLANES, n_states), F32),
            jax.ShapeDtypeStruct((SUBLANES, n_states), F32),
            jax.ShapeDtypeStruct((POOL_BUF * SUBLANES, D_POOL), F32),
            jax.ShapeDtypeStruct(((CONV_W - 1) * SUBLANES, 2 * D_FF), F32),
        ],
        scratch_shapes=[
            pltpu.VMEM((S5_SPLIT, rows, 2 * ST_PER_SPLIT), BF16),
            pltpu.VMEM((rows, D_FF), BF16),
            pltpu.VMEM((2, tc, SUBLANES, D_MODEL), F32),
            pltpu.VMEM((2, tc, SUBLANES, D_MODEL), F32),
            pltpu.SemaphoreType.DMA((2,)),
            pltpu.SemaphoreType.DMA((2,)),
        ],
        compiler_params=pltpu.CompilerParams(dimension_semantics=("arbitrary",), **cparams),
        name="prompt_layer",
    )(x_prompt, *weights)
    new_pool_p = jnp.transpose(p_pool.reshape(POOL_BUF, nb, D_POOL), (1, 0, 2))[None]
    new_conv_p = jnp.transpose(p_conv.reshape(CONV_W - 1, nb, 2 * D_FF), (1, 0, 2))[None]
    new_re_p = p_re.reshape(1, nb, S5_GROUPS, S5_STATE)
    new_im_p = p_im.reshape(1, nb, S5_GROUPS, S5_STATE)

    return (y_prompt, ys, new_re_p, new_im_p, new_pool_p, new_conv_p,
            s_re.reshape(1, ns, S5_GROUPS, S5_STATE), s_im.reshape(1, ns, S5_GROUPS, S5_STATE),
            jnp.transpose(s_pool, (1, 0, 2))[None], s_conv[None])
```

```python
import functools
import math

import numpy as np
import jax
import jax.numpy as jnp
from jax import lax
from jax.experimental import pallas as pl
from jax.experimental.pallas import tpu as pltpu

D_MODEL = 1024
D_S5 = 512
S5_CH = 16
S5_GROUPS = 32
S5_STATE = 64
D_POOL = 512
POOL_WINDOWS = (2, 4, 8, 16)
POOL_CH = 128
POOL_BUF = 15
D_FF = 2816
CONV_W = 3
EPS = 1e-6

SUBLANES = 8
S5_SPLIT = 2
GROUPS_PER_SPLIT = S5_GROUPS // S5_SPLIT
U_PER_SPLIT = GROUPS_PER_SPLIT * S5_CH
ST_PER_SPLIT = GROUPS_PER_SPLIT * S5_STATE
FF_CHUNK = 256
N_FF_CHUNKS = D_FF // FF_CHUNK
SQRT_HALF = float(np.sqrt(0.5).astype(np.float32))

BF16 = jnp.bfloat16
F32 = jnp.float32


def _dot(a, b):
    return jnp.dot(a, b, preferred_element_type=F32)


def _rms(x, g):
    ms = jnp.mean(x * x, axis=-1, keepdims=True)
    return x * lax.rsqrt(ms + EPS) * g


def _gelu(x):
    return 0.5 * x * (1.0 + lax.erf(x * SQRT_HALF))


def _s5_post(y_lin, u, dskip_ref, wglu_ref):
    y = _gelu(y_lin + dskip_ref[...] * u)
    return y * jax.nn.sigmoid(_dot(y.astype(BF16), wglu_ref[...]))


def _pool_project(pooled_cols, wpool_ref, pscale_ref):
    outs = []
    for gi in range(len(POOL_WINDOWS)):
        z = _dot(pooled_cols[gi].astype(BF16), wpool_ref[gi * POOL_CH:(gi + 1) * POOL_CH, :])
        outs.append(z * pscale_ref[:, gi * POOL_CH:(gi + 1) * POOL_CH])
    return jnp.concatenate(outs, axis=1)


def _block_copies(hbm_ref, buf_ref, sem_ref, block, slot, tc, to_hbm):
    copies = []
    for n in range(SUBLANES):
        hbm = hbm_ref.at[n, pl.ds(block * tc, tc), :]
        vmem = buf_ref.at[slot, :, n, :]
        src, dst = (vmem, hbm) if to_hbm else (hbm, vmem)
        copies.append(pltpu.make_async_copy(src, dst, sem_ref.at[slot]))
    return copies


def _prompt_kernel(tc, n_steps,
                   x_hbm, g1_ref, win_ref, are_ref, aim_ref, bbd_ref, cbd_ref, dskip_ref,
                   wglu_ref, wpool_ref, pscale_ref, wout_ref, g2_ref, wup_ref, cw_ref,
                   cb_ref, wdown_ref, gf_ref,
                   y_hbm, sre_ref, sim_ref, pool_ref, conv_ref,
                   h_ref, act_ref, xbuf, ybuf, sem_in, sem_out):
    rows = tc * SUBLANES
    step = pl.program_id(0)
    slot = step % 2

    @pl.when(step == 0)
    def _():
        sre_ref[...] = jnp.zeros_like(sre_ref)
        sim_ref[...] = jnp.zeros_like(sim_ref)
        pool_ref[...] = jnp.zeros_like(pool_ref)
        conv_ref[...] = jnp.zeros_like(conv_ref)
        for cp in _block_copies(x_hbm, xbuf, sem_in, 0, 0, tc, False):
            cp.start()

    @pl.when(step + 1 < n_steps)
    def _():
        for cp in _block_copies(x_hbm, xbuf, sem_in, step + 1, 1 - slot, tc, False):
            cp.start()

    @pl.when(step >= 2)
    def _():
        for cp in _block_copies(y_hbm, ybuf, sem_out, step - 2, slot, tc, True):
            cp.wait()

    for cp in _block_copies(x_hbm, xbuf, sem_in, step, slot, tc, False):
        cp.wait()

    x = xbuf[slot].reshape(rows, D_MODEL)
    hb = _rms(x, g1_ref[...]).astype(BF16)
    proj = _dot(hb, win_ref[...])
    u = proj[:, :D_S5]
    v = proj[:, D_S5:]
    ub = u.astype(BF16)

    bus = [_dot(ub[:, k * U_PER_SPLIT:(k + 1) * U_PER_SPLIT], bbd_ref[k]) for k in range(S5_SPLIT)]
    y_parts = []
    for k in range(S5_SPLIT):
        bu = bus[k]
        lanes = slice(k * ST_PER_SPLIT, (k + 1) * ST_PER_SPLIT)
        a_re = jnp.broadcast_to(are_ref[:, lanes], (SUBLANES, ST_PER_SPLIT))
        a_im = jnp.broadcast_to(aim_ref[:, lanes], (SUBLANES, ST_PER_SPLIT))
        h_re, h_im = sre_ref[:, lanes], sim_ref[:, lanes]
        for i in range(tc // 2):
            r0 = i * 2 * SUBLANES
            res, ims = [], []
            for s in range(2):
                rs = slice(r0 + s * SUBLANES, r0 + (s + 1) * SUBLANES)
                n_re = a_re * h_re - a_im * h_im + bu[rs, :ST_PER_SPLIT]
                n_im = a_re * h_im + a_im * h_re + bu[rs, ST_PER_SPLIT:]
                h_re, h_im = n_re, n_im
                res.append(h_re)
                ims.append(h_im)
            pair = slice(r0, r0 + 2 * SUBLANES)
            h_ref[k, pair, :ST_PER_SPLIT] = jnp.concatenate(res, axis=0).astype(BF16)
            h_ref[k, pair, ST_PER_SPLIT:] = jnp.concatenate(ims, axis=0).astype(BF16)
        sre_ref[:, lanes] = h_re
        sim_ref[:, lanes] = h_im
        half = rows // 2
        y_parts.append(jnp.concatenate(
            [_dot(h_ref[k, :half, :], cbd_ref[k]), _dot(h_ref[k, half:, :], cbd_ref[k])], axis=0))
    y_s5 = _s5_post(jnp.concatenate(y_parts, axis=1), u, dskip_ref, wglu_ref)

    halo = POOL_BUF * SUBLANES
    vfull = jnp.concatenate([pool_ref[...], v], axis=0)
    pool_ref[...] = vfull[rows:, :]
    t_idx = step * tc + (lax.broadcasted_iota(jnp.int32, (rows, POOL_CH), 0) >> 3)
    pooled = []
    for gi, w in enumerate(POOL_WINDOWS):
        s = vfull[:, gi * POOL_CH:(gi + 1) * POOL_CH]
        span = 1
        while span < w:
            sh = span * SUBLANES
            s = s[sh:, :] + s[:-sh, :]
            span *= 2
        first = (POOL_BUF - (w - 1)) * SUBLANES
        wsum = s[first:first + rows, :]
        cnt = jnp.minimum(t_idx + 1, w).astype(F32)
        pooled.append(wsum / cnt - v[:, gi * POOL_CH:(gi + 1) * POOL_CH])
    y_pool = _pool_project(pooled, wpool_ref, pscale_ref)

    x1 = x + _dot(y_s5.astype(BF16), wout_ref[:D_S5, :]) + _dot(y_pool.astype(BF16), wout_ref[D_S5:, :])

    h2b = _rms(x1, g2_ref[...]).astype(BF16)
    taps = (CONV_W - 1) * SUBLANES
    for j in range(N_FF_CHUNKS):
        convd = []
        for base in (0, D_FF):
            cols = slice(base + j * FF_CHUNK, base + (j + 1) * FF_CHUNK)
            hup = _dot(h2b, wup_ref[:, cols])
            full = jnp.concatenate([conv_ref[:, cols], hup], axis=0)
            conv_ref[:, cols] = hup[rows - taps:, :]
            c = cb_ref[:, cols]
            for kk in range(CONV_W):
                c = c + cw_ref[kk:kk + 1, cols] * full[kk * SUBLANES:kk * SUBLANES + rows, :]
            convd.append(c)
        act_ref[:, j * FF_CHUNK:(j + 1) * FF_CHUNK] = (_gelu(convd[0]) * convd[1]).astype(BF16)
    y = _rms(x1 + _dot(act_ref[...], wdown_ref[...]), gf_ref[...])

    ybuf[slot] = y.reshape(tc, SUBLANES, D_MODEL)
    for cp in _block_copies(y_hbm, ybuf, sem_out, step, slot, tc, True):
        cp.start()

    @pl.when(step == n_steps - 1)
    def _():
        if n_steps >= 2:
            for cp in _block_copies(y_hbm, ybuf, sem_out, step - 1, 1 - slot, tc, True):
                cp.wait()
        for cp in _block_copies(y_hbm, ybuf, sem_out, step, slot, tc, True):
            cp.wait()


N_BIG_WEIGHTS = 6
STAGE_SHAPES = ((4, 256, D_MODEL), (2, 256, D_S5), (2, 256, POOL_CH), (4, D_MODEL, FF_CHUNK))
STAGE_LOOKAHEAD = 3


def _sample_kernel(x_hbm, sre_in, sim_in, pool_hbm, conv_hbm,
                   g1_ref, are_ref, aim_ref, bbd_ref, cbd_ref, dskip_ref, pscale_ref, g2_ref,
                   cw_ref, cb_ref, gf_ref,
                   win_hbm, wglu_hbm, wpool_hbm, wout_hbm, wup_hbm, wdown_hbm,
                   y_hbm, sre_ref, sim_ref, pool_out, conv_out,
                   winb_out, wglub_out, wpoolb_out, woutb_out, wupb_out, wdownb_out,
                   win_ref, wglu_ref, wpool_ref, wout_ref, wup_ref, wdown_ref,
                   stage_model, stage_s5, stage_pool, stage_col,
                   xbuf, ybuf, poolbuf, convbuf, hupbuf, vbuf,
                   sem_stage, sem_state, sem_wout, sem_sout):
    def state_row(hbm_ref, r):
        return hbm_ref.at[:, pl.ds(r, 1), :]

    in_copies = [pltpu.make_async_copy(x_hbm, xbuf, sem_state.at[0]),
                 pltpu.make_async_copy(pool_hbm, poolbuf, sem_state.at[1])]
    in_copies += [pltpu.make_async_copy(state_row(conv_hbm, r), convbuf.at[r], sem_state.at[2])
                  for r in range(CONV_W - 1)]
    for cp in in_copies:
        cp.start()

    stages = (stage_model, stage_s5, stage_pool, stage_col)
    sem_base = [sum(s.shape[0] for s in stages[:i]) for i in range(len(stages))]
    ring_pos = [0] * len(stages)
    queue = []
    out_copies = []
    live = {}

    def enqueue(src, sid, dst, after=None):
        queue.append((src, sid, ring_pos[sid] % stages[sid].shape[0], dst, after))
        ring_pos[sid] += 1

    def start_out(src, dst, sem):
        cp = pltpu.make_async_copy(src, dst, sem)
        cp.start()
        out_copies.append(cp)

    def enqueue_rows(w_hbm, wb_ref, sid, after_last):
        rows = stages[sid].shape[1]
        n_chunks = w_hbm.shape[0] // rows
        for c in range(n_chunks):
            part = pl.ds(c * rows, rows)
            enqueue(w_hbm.at[part, :], sid, wb_ref.at[part, :], after_last if c == n_chunks - 1 else None)

    def mixer():
        for cp in in_copies:
            cp.wait()
        start_out(wout_ref, woutb_out, sem_wout.at[3])
        x = xbuf[:, 0, :]
        hb = _rms(x, g1_ref[...]).astype(BF16)
        proj = _dot(hb, win_ref[...])
        u = proj[:, :D_S5]
        v = proj[:, D_S5:]
        ub = u.astype(BF16)

        y_parts = []
        for k in range(S5_SPLIT):
            bu = _dot(ub[:, k * U_PER_SPLIT:(k + 1) * U_PER_SPLIT], bbd_ref[k])
            lanes = slice(k * ST_PER_SPLIT, (k + 1) * ST_PER_SPLIT)
            a_re = are_ref[:, lanes]
            a_im = aim_ref[:, lanes]
            h_re0 = sre_in[:, lanes]
            h_im0 = sim_in[:, lanes]
            h_re = a_re * h_re0 - a_im * h_im0 + bu[:, :ST_PER_SPLIT]
            h_im = a_re * h_im0 + a_im * h_re0 + bu[:, ST_PER_SPLIT:]
            sre_ref[:, lanes] = h_re
            sim_ref[:, lanes] = h_im
            hcat = jnp.concatenate([h_re, h_im], axis=1).astype(BF16)
            y_parts.append(_dot(hcat, cbd_ref[k]))
        y_s5 = _s5_post(jnp.concatenate(y_parts, axis=1), u, dskip_ref, wglu_ref)

        pooled = []
        for gi, w in enumerate(POOL_WINDOWS):
            lanes = slice(gi * POOL_CH, (gi + 1) * POOL_CH)
            vc = v[:, lanes]
            wsum = vc
            for back in range(1, w):
                wsum = wsum + poolbuf[POOL_BUF - back, :, lanes]
            pooled.append(wsum / float(w) - vc)
        y_pool = _pool_project(pooled, wpool_ref, pscale_ref)
        vbuf[...] = v
        start_out(poolbuf.at[pl.ds(1, POOL_BUF - 1)], pool_out.at[pl.ds(0, POOL_BUF - 1)], sem_sout.at[0])
        start_out(vbuf, pool_out.at[POOL_BUF - 1], sem_sout.at[1])
        start_out(convbuf.at[1], state_row(conv_out, 0), sem_sout.at[2])

        x1 = (x + _dot(y_s5.astype(BF16), wout_ref[:D_S5, :])
              + _dot(y_pool.astype(BF16), wout_ref[D_S5:, :]))
        live["x1"] = x1
        live["h2b"] = _rms(x1, g2_ref[...]).astype(BF16)
        live["acc"] = None

    def ffn_chunk(j):
        rows_j = pl.ds(j * FF_CHUNK, FF_CHUNK)
        convd = []
        for base in (0, D_FF):
            cols = slice(base + j * FF_CHUNK, base + (j + 1) * FF_CHUNK)
            start_out(wup_ref.at[:, cols], wupb_out.at[:, cols], sem_wout.at[4])
            hup = _dot(live["h2b"], wup_ref[:, cols])
            hupbuf[:, 0, cols] = hup
            convd.append(cb_ref[:, cols] + cw_ref[0:1, cols] * convbuf[0, :, 0, cols]
                         + cw_ref[1:2, cols] * convbuf[1, :, 0, cols] + cw_ref[2:3, cols] * hup)
        start_out(wdown_ref.at[rows_j, :], wdownb_out.at[rows_j, :], sem_wout.at[5])
        act = (_gelu(convd[0]) * convd[1]).astype(BF16)
        part = _dot(act, wdown_ref[j * FF_CHUNK:(j + 1) * FF_CHUNK, :])
        live["acc"] = part if live["acc"] is None else live["acc"] + part

    enqueue_rows(win_hbm, win_ref, 0, lambda: start_out(win_ref, winb_out, sem_wout.at[0]))
    enqueue_rows(wglu_hbm, wglu_ref, 1, lambda: start_out(wglu_ref, wglub_out, sem_wout.at[1]))
    enqueue_rows(wpool_hbm, wpool_ref, 2, lambda: start_out(wpool_ref, wpoolb_out, sem_wout.at[2]))
    enqueue_rows(wout_hbm, wout_ref, 0, mixer)
    for j in range(N_FF_CHUNKS):
        for base in (0, D_FF):
            cols = pl.ds(base + j * FF_CHUNK, FF_CHUNK)
            enqueue(wup_hbm.at[:, cols], 3, wup_ref.at[:, cols])
        rows_j = pl.ds(j * FF_CHUNK, FF_CHUNK)
        enqueue(wdown_hbm.at[rows_j, :], 0, wdown_ref.at[rows_j, :], functools.partial(ffn_chunk, j))

    def read(entry):
        src, sid, slot, _, _ = entry
        return pltpu.make_async_copy(src, stages[sid].at[slot], sem_stage.at[sem_base[sid] + slot])

    started = 0
    for i, entry in enumerate(queue):
        while started < min(len(queue), i + 1 + STAGE_LOOKAHEAD):
            read(queue[started]).start()
            started += 1
        read(entry).wait()
        _, sid, slot, dst, after = entry
        dst[...] = stages[sid][slot].astype(BF16)
        if after is not None:
            after()

    ybuf[:, 0, :] = _rms(live["x1"] + live["acc"], gf_ref[...])
    start_out(hupbuf, state_row(conv_out, 1), sem_sout.at[3])
    start_out(ybuf, y_hbm, sem_sout.at[4])
    for cp in out_copies:
        cp.wait()


def _s5_tables(a_re, a_im, log_dt, b_re, b_im, c_re, c_im):
    dt = jnp.exp(log_dt)[:, None]
    mag = jnp.exp(dt * a_re)
    abar_re = mag * jnp.cos(dt * a_im)
    abar_im = mag * jnp.sin(dt * a_im)
    nr, ni = abar_re - 1.0, abar_im
    den = a_re * a_re + a_im * a_im
    f_re = ((nr * a_re + ni * a_im) / den)[:, :, None]
    f_im = ((ni * a_re - nr * a_im) / den)[:, :, None]
    bbar_re = f_re * b_re - f_im * b_im
    bbar_im = f_re * b_im + f_im * b_re
    u_group = lax.broadcasted_iota(jnp.int32, (U_PER_SPLIT, ST_PER_SPLIT), 0) // S5_CH
    st_group = lax.broadcasted_iota(jnp.int32, (U_PER_SPLIT, ST_PER_SPLIT), 1) // S5_STATE
    diag = u_group == st_group

    def in_table(b):
        b = b.reshape(S5_SPLIT, GROUPS_PER_SPLIT, S5_STATE, S5_CH)
        row = jnp.transpose(b, (0, 3, 1, 2)).reshape(S5_SPLIT, 1, S5_CH, ST_PER_SPLIT)
        rep = jnp.broadcast_to(row, (S5_SPLIT, GROUPS_PER_SPLIT, S5_CH, ST_PER_SPLIT))
        return jnp.where(diag, rep.reshape(S5_SPLIT, U_PER_SPLIT, ST_PER_SPLIT), 0.0)

    def out_table(c):
        c = c.reshape(S5_SPLIT, GROUPS_PER_SPLIT, S5_CH, S5_STATE)
        col = jnp.transpose(c, (0, 1, 3, 2)).reshape(S5_SPLIT, ST_PER_SPLIT, S5_CH)
        rep = jnp.tile(col, (1, 1, GROUPS_PER_SPLIT))
        return jnp.where(diag.T, rep, 0.0)

    bbd = jnp.concatenate([in_table(bbar_re), in_table(bbar_im)], axis=2)
    cbd = jnp.concatenate([out_table(c_re), out_table(-c_im)], axis=1)
    return abar_re.reshape(1, -1), abar_im.reshape(1, -1), bbd.astype(BF16), cbd.astype(BF16)


def _vmem_spec():
    return pl.BlockSpec(memory_space=pltpu.VMEM)


PROMPT_TC = 64
VMEM_LIMIT_BYTES = 60 * 1024 * 1024


def kernel(x_prompt, x_sample, state_s5_re, state_s5_im, state_pool, state_ffn_conv, norm_mix_g, w_in, s5_a_re, s5_a_im, s5_log_dt, s5_b_re, s5_b_im, s5_c_re, s5_c_im, s5_d, s5_w_glu, pool_w, pool_scale, w_out, norm_ffn_g, ffn_w_up, ffn_conv_w, ffn_conv_b, ffn_w_down, norm_final_g):
    nb, seq, _ = x_prompt.shape
    ns = x_sample.shape[0]
    assert nb == SUBLANES and seq % PROMPT_TC == 0 and x_sample.shape[1] == 1
    assert norm_mix_g.shape[0] == 1, "single layer"

    a_re, a_im, bbd, cbd = _s5_tables(s5_a_re[0], s5_a_im[0], s5_log_dt[0], s5_b_re[0],
                                      s5_b_im[0], s5_c_re[0], s5_c_im[0])
    n_states = S5_GROUPS * S5_STATE
    cparams = dict(vmem_limit_bytes=VMEM_LIMIT_BYTES)
    g1 = norm_mix_g[0].reshape(1, D_MODEL)
    dskip = s5_d[0].reshape(1, D_S5)
    pscale = pool_scale[0].reshape(1, D_POOL)
    g2 = norm_ffn_g[0].reshape(1, D_MODEL)
    cw = ffn_conv_w[0]
    cb = ffn_conv_b[0].reshape(1, 2 * D_FF)
    gf = norm_final_g.reshape(1, D_MODEL)

    big_f32 = (w_in[0], s5_w_glu[0], pool_w[0].reshape(len(POOL_WINDOWS) * POOL_CH, POOL_CH),
               w_out[0], ffn_w_up[0], ffn_w_down[0])
    small = (g1, a_re, a_im, bbd, cbd, dskip, pscale, g2, cw, cb, gf)
    any_spec = pl.BlockSpec(memory_space=pl.ANY)
    pool_in = jnp.transpose(state_pool[0], (1, 0, 2))
    sample_out = pl.pallas_call(
        _sample_kernel,
        in_specs=[any_spec] + [_vmem_spec()] * 2 + [any_spec] * 2 + [_vmem_spec()] * len(small)
                 + [any_spec] * N_BIG_WEIGHTS,
        out_specs=[any_spec] + [_vmem_spec()] * 2 + [any_spec] * (2 + N_BIG_WEIGHTS),
        out_shape=[
            jax.ShapeDtypeStruct((ns, 1, D_MODEL), F32),
            jax.ShapeDtypeStruct((ns, n_states), F32),
            jax.ShapeDtypeStruct((ns, n_states), F32),
            jax.ShapeDtypeStruct((POOL_BUF, ns, D_POOL), F32),
            jax.ShapeDtypeStruct((ns, CONV_W - 1, 2 * D_FF), F32),
        ] + [jax.ShapeDtypeStruct(w.shape, BF16) for w in big_f32],
        scratch_shapes=[pltpu.VMEM(w.shape, BF16) for w in big_f32]
        + [pltpu.VMEM(shape, F32) for shape in STAGE_SHAPES] + [
            pltpu.VMEM((ns, 1, D_MODEL), F32),
            pltpu.VMEM((ns, 1, D_MODEL), F32),
            pltpu.VMEM((POOL_BUF, ns, D_POOL), F32),
            pltpu.VMEM((CONV_W - 1, ns, 1, 2 * D_FF), F32),
            pltpu.VMEM((ns, 1, 2 * D_FF), F32),
            pltpu.VMEM((ns, D_POOL), F32),
            pltpu.SemaphoreType.DMA((sum(shape[0] for shape in STAGE_SHAPES),)),
            pltpu.SemaphoreType.DMA((3,)),
            pltpu.SemaphoreType.DMA((N_BIG_WEIGHTS,)),
            pltpu.SemaphoreType.DMA((5,)),
        ],
        compiler_params=pltpu.CompilerParams(**cparams),
        name="sample_layer",
    )(x_sample, state_s5_re[0].reshape(ns, n_states), state_s5_im[0].reshape(ns, n_states),
      pool_in, state_ffn_conv[0], *small, *big_f32)
    ys, s_re, s_im, s_pool, s_conv = sample_out[:5]
    win_b, wglu_b, wpool_b, wout_b, wup_b, wdown_b = sample_out[5:]
    weights = (g1, win_b, a_re, a_im, bbd, cbd, dskip, wglu_b, wpool_b, pscale, wout_b, g2,
               wup_b, cw, cb, wdown_b, gf)

    tc = PROMPT_TC
    rows = tc * SUBLANES
    const = lambda i: (0, 0)
    y_prompt, p_re, p_im, p_pool, p_conv = pl.pallas_call(
        functools.partial(_prompt_kernel, tc, seq // tc),
        grid=(seq // tc,),
        in_specs=[pl.BlockSpec(memory_space=pl.ANY)] + [_vmem_spec()] * len(weights),
        out_specs=[
            pl.BlockSpec(memory_space=pl.ANY),
            pl.BlockSpec((SUBLANES, n_states), const),
            pl.BlockSpec((SUBLANES, n_states), const),
            pl.BlockSpec((POOL_BUF * SUBLANES, D_POOL), const),
            pl.BlockSpec(((CONV_W - 1) * SUBLANES, 2 * D_FF), const),
        ],
        out_shape=[
            jax.ShapeDtypeStruct((nb, seq, D_MODEL), F32),
            jax.ShapeDtypeStruct((SUBLANES, n_states), F32),
            jax.ShapeDtypeStruct((SUBLANES, n_states), F32),
            jax.ShapeDtypeStruct((POOL_BUF * SUBLANES, D_POOL), F32),
            jax.ShapeDtypeStruct(((CONV_W - 1) * SUBLANES, 2 * D_FF), F32),
        ],
        scratch_shapes=[
            pltpu.VMEM((S5_SPLIT, rows, 2 * ST_PER_SPLIT), BF16),
            pltpu.VMEM((rows, D_FF), BF16),
            pltpu.VMEM((2, tc, SUBLANES, D_MODEL), F32),
            pltpu.VMEM((2, tc, SUBLANES, D_MODEL), F32),
            pltpu.SemaphoreType.DMA((2,)),
            pltpu.SemaphoreType.DMA((2,)),
        ],
        compiler_params=pltpu.CompilerParams(dimension_semantics=("arbitrary",), **cparams),
        name="prompt_layer",
    )(x_prompt, *weights)
    new_pool_p = jnp.transpose(p_pool.reshape(POOL_BUF, nb, D_POOL), (1, 0, 2))[None]
    new_conv_p = jnp.transpose(p_conv.reshape(CONV_W - 1, nb, 2 * D_FF), (1, 0, 2))[None]
    new_re_p = p_re.reshape(1, nb, S5_GROUPS, S5_STATE)
    new_im_p = p_im.reshape(1, nb, S5_GROUPS, S5_STATE)

    return (y_prompt, ys, new_re_p, new_im_p, new_pool_p, new_conv_p,
            s_re.reshape(1, ns, S5_GROUPS, S5_STATE), s_im.reshape(1, ns, S5_GROUPS, S5_STATE),
            jnp.transpose(s_pool, (1, 0, 2))[None], s_conv[None])
```

```python
import functools
import math

import numpy as np
import jax
import jax.numpy as jnp
from jax import lax
from jax.experimental import pallas as pl
from jax.experimental.pallas import tpu as pltpu

D_MODEL = 1024
D_S5 = 512
S5_CH = 16
S5_GROUPS = 32
S5_STATE = 64
D_POOL = 512
POOL_WINDOWS = (2, 4, 8, 16)
POOL_CH = 128
POOL_BUF = 15
D_FF = 2816
CONV_W = 3
EPS = 1e-6

SUBLANES = 8
S5_SPLIT = 2
GROUPS_PER_SPLIT = S5_GROUPS // S5_SPLIT
U_PER_SPLIT = GROUPS_PER_SPLIT * S5_CH
ST_PER_SPLIT = GROUPS_PER_SPLIT * S5_STATE
FF_CHUNK = 256
N_FF_CHUNKS = D_FF // FF_CHUNK
SQRT_HALF = float(np.sqrt(0.5).astype(np.float32))

BF16 = jnp.bfloat16
F32 = jnp.float32


def _dot(a, b):
    return jnp.dot(a, b, preferred_element_type=F32)


def _rms(x, g):
    ms = jnp.mean(x * x, axis=-1, keepdims=True)
    return x * lax.rsqrt(ms + EPS) * g


def _gelu(x):
    return 0.5 * x * (1.0 + lax.erf(x * SQRT_HALF))


def _s5_post(y_lin, u, dskip_ref, wglu_ref):
    y = _gelu(y_lin + dskip_ref[...] * u)
    return y * jax.nn.sigmoid(_dot(y.astype(BF16), wglu_ref[...]))


def _pool_project(pooled_cols, wpool_ref, pscale_ref):
    outs = []
    for gi in range(len(POOL_WINDOWS)):
        z = _dot(pooled_cols[gi].astype(BF16), wpool_ref[gi * POOL_CH:(gi + 1) * POOL_CH, :])
        outs.append(z * pscale_ref[:, gi * POOL_CH:(gi + 1) * POOL_CH])
    return jnp.concatenate(outs, axis=1)


def _block_copies(hbm_ref, buf_ref, sem_ref, block, slot, tc, to_hbm):
    copies = []
    for n in range(SUBLANES):
        hbm = hbm_ref.at[n, pl.ds(block * tc, tc), :]
        vmem = buf_ref.at[slot, :, n, :]
        src, dst = (vmem, hbm) if to_hbm else (hbm, vmem)
        copies.append(pltpu.make_async_copy(src, dst, sem_ref.at[slot]))
    return copies


def _prompt_kernel(tc, n_steps,
                   x_hbm, g1_ref, win_ref, are_ref, aim_ref, bbd_ref, cbd_ref, dskip_ref,
                   wglu_ref, wpool_ref, pscale_ref, wout_ref, g2_ref, wup_ref, cw_ref,
                   cb_ref, wdown_ref, gf_ref,
                   y_hbm, sre_ref, sim_ref, pool_ref, conv_ref,
                   h_ref, act_ref, hb_ref, x2_ref, xbuf, ybuf, sem_in, sem_out):
    rows = tc * SUBLANES
    step = pl.program_id(0)
    par = step % 2
    n_in = xbuf.shape[0]

    def x_copies(block):
        return _block_copies(x_hbm, xbuf, sem_in, block, block % n_in, tc, False)

    def y_copies(block):
        return _block_copies(y_hbm, ybuf, sem_out, block, block % 2, tc, True)

    @pl.when(step == 0)
    def _():
        sre_ref[...] = jnp.zeros_like(sre_ref)
        sim_ref[...] = jnp.zeros_like(sim_ref)
        pool_ref[...] = jnp.zeros_like(pool_ref)
        conv_ref[...] = jnp.zeros_like(conv_ref)
        x2_ref[...] = jnp.zeros_like(x2_ref)
        for cp in x_copies(0):
            cp.start()
        if n_steps > 1:
            for cp in x_copies(1):
                cp.start()
        for cp in x_copies(0):
            cp.wait()
        hb_ref[0] = _rms(xbuf[0].reshape(rows, D_MODEL), g1_ref[...]).astype(BF16)

    @pl.when(step + 2 < n_steps)
    def _():
        for cp in x_copies(step + 2):
            cp.start()

    @pl.when(step + 1 < n_steps)
    def _():
        for cp in x_copies(step + 1):
            cp.wait()

    @pl.when(step >= 3)
    def _():
        for cp in y_copies(step - 3):
            cp.wait()

    ybuf[1 - par] = _rms(x2_ref[1 - par], gf_ref[...]).reshape(tc, SUBLANES, D_MODEL)

    x = xbuf[step % n_in].reshape(rows, D_MODEL)
    proj = _dot(hb_ref[par], win_ref[...])
    u = proj[:, :D_S5]
    v = proj[:, D_S5:]
    ub = u.astype(BF16)

    bus = [_dot(ub[:, k * U_PER_SPLIT:(k + 1) * U_PER_SPLIT], bbd_ref[k]) for k in range(S5_SPLIT)]
    y_parts = []
    for k in range(S5_SPLIT):
        bu = bus[k]
        lanes = slice(k * ST_PER_SPLIT, (k + 1) * ST_PER_SPLIT)
        a_re = jnp.broadcast_to(are_ref[:, lanes], (SUBLANES, ST_PER_SPLIT))
        a_im = jnp.broadcast_to(aim_ref[:, lanes], (SUBLANES, ST_PER_SPLIT))
        h_re, h_im = sre_ref[:, lanes], sim_ref[:, lanes]
        for i in range(tc // 2):
            r0 = i * 2 * SUBLANES
            res, ims = [], []
            for s in range(2):
                rs = slice(r0 + s * SUBLANES, r0 + (s + 1) * SUBLANES)
                n_re = a_re * h_re - a_im * h_im + bu[rs, :ST_PER_SPLIT]
                n_im = a_re * h_im + a_im * h_re + bu[rs, ST_PER_SPLIT:]
                h_re, h_im = n_re, n_im
                res.append(h_re)
                ims.append(h_im)
            pair = slice(r0, r0 + 2 * SUBLANES)
            h_ref[k, pair, :ST_PER_SPLIT] = jnp.concatenate(res, axis=0).astype(BF16)
            h_ref[k, pair, ST_PER_SPLIT:] = jnp.concatenate(ims, axis=0).astype(BF16)
        sre_ref[:, lanes] = h_re
        sim_ref[:, lanes] = h_im
        half = rows // 2
        y_parts.append(jnp.concatenate(
            [_dot(h_ref[k, :half, :], cbd_ref[k]), _dot(h_ref[k, half:, :], cbd_ref[k])], axis=0))
    y_s5 = _s5_post(jnp.concatenate(y_parts, axis=1), u, dskip_ref, wglu_ref)

    halo = POOL_BUF * SUBLANES
    vfull = jnp.concatenate([pool_ref[...], v], axis=0)
    pool_ref[...] = vfull[rows:, :]
    t_idx = step * tc + (lax.broadcasted_iota(jnp.int32, (rows, POOL_CH), 0) >> 3)
    pooled = []
    for gi, w in enumerate(POOL_WINDOWS):
        s = vfull[:, gi * POOL_CH:(gi + 1) * POOL_CH]
        span = 1
        while span < w:
            sh = span * SUBLANES
            s = s[sh:, :] + s[:-sh, :]
            span *= 2
        first = (POOL_BUF - (w - 1)) * SUBLANES
        wsum = s[first:first + rows, :]
        cnt = jnp.minimum(t_idx + 1, w).astype(F32)
        pooled.append(wsum / cnt - v[:, gi * POOL_CH:(gi + 1) * POOL_CH])
    y_pool = _pool_project(pooled, wpool_ref, pscale_ref)

    x1 = x + _dot(y_s5.astype(BF16), wout_ref[:D_S5, :]) + _dot(y_pool.astype(BF16), wout_ref[D_S5:, :])

    h2b = _rms(x1, g2_ref[...]).astype(BF16)
    taps = (CONV_W - 1) * SUBLANES
    for j in range(N_FF_CHUNKS):
        convd = []
        for base in (0, D_FF):
            cols = slice(base + j * FF_CHUNK, base + (j + 1) * FF_CHUNK)
            hup = _dot(h2b, wup_ref[:, cols])
            full = jnp.concatenate([conv_ref[:, cols], hup], axis=0)
            conv_ref[:, cols] = hup[rows - taps:, :]
            c = cb_ref[:, cols]
            for kk in range(CONV_W):
                c = c + cw_ref[kk:kk + 1, cols] * full[kk * SUBLANES:kk * SUBLANES + rows, :]
            convd.append(c)
        act_ref[:, j * FF_CHUNK:(j + 1) * FF_CHUNK] = (_gelu(convd[0]) * convd[1]).astype(BF16)
    x_next = xbuf[(step + 1) % n_in].reshape(rows, D_MODEL)
    hb_ref[1 - par] = _rms(x_next, g1_ref[...]).astype(BF16)

    x2_ref[par] = x1 + _dot(act_ref[...], wdown_ref[...])

    @pl.when(step >= 1)
    def _():
        for cp in y_copies(step - 1):
            cp.start()

    @pl.when(step == n_steps - 1)
    def _():
        if n_steps >= 3:
            for cp in y_copies(step - 2):
                cp.wait()
        ybuf[par] = _rms(x2_ref[par], gf_ref[...]).reshape(tc, SUBLANES, D_MODEL)
        for cp in y_copies(step):
            cp.start()
        if n_steps >= 2:
            for cp in y_copies(step - 1):
                cp.wait()
        for cp in y_copies(step):
            cp.wait()


N_BIG_WEIGHTS = 6
STAGE_SHAPES = ((4, 256, D_MODEL), (2, 256, D_S5), (2, 256, POOL_CH), (4, D_MODEL, FF_CHUNK))
STAGE_LOOKAHEAD = 3


def _sample_kernel(x_hbm, sre_in, sim_in, pool_hbm, conv_hbm,
                   g1_ref, are_ref, aim_ref, bbd_ref, cbd_ref, dskip_ref, pscale_ref, g2_ref,
                   cw_ref, cb_ref, gf_ref,
                   win_hbm, wglu_hbm, wpool_hbm, wout_hbm, wup_hbm, wdown_hbm,
                   y_hbm, sre_ref, sim_ref, pool_out, conv_out,
                   winb_out, wglub_out, wpoolb_out, woutb_out, wupb_out, wdownb_out,
                   win_ref, wglu_ref, wpool_ref, wout_ref, wup_ref, wdown_ref,
                   stage_model, stage_s5, stage_pool, stage_col,
                   xbuf, ybuf, poolbuf, convbuf, hupbuf, vbuf,
                   sem_stage, sem_state, sem_wout, sem_sout):
    def state_row(hbm_ref, r):
        return hbm_ref.at[:, pl.ds(r, 1), :]

    in_copies = [pltpu.make_async_copy(x_hbm, xbuf, sem_state.at[0]),
                 pltpu.make_async_copy(pool_hbm, poolbuf, sem_state.at[1])]
    in_copies += [pltpu.make_async_copy(state_row(conv_hbm, r), convbuf.at[r], sem_state.at[2])
                  for r in range(CONV_W - 1)]
    for cp in in_copies:
        cp.start()

    stages = (stage_model, stage_s5, stage_pool, stage_col)
    sem_base = [sum(s.shape[0] for s in stages[:i]) for i in range(len(stages))]
    ring_pos = [0] * len(stages)
    queue = []
    out_copies = []
    live = {}

    def enqueue(src, sid, dst, after=None):
        queue.append((src, sid, ring_pos[sid] % stages[sid].shape[0], dst, after))
        ring_pos[sid] += 1

    def start_out(src, dst, sem):
        cp = pltpu.make_async_copy(src, dst, sem)
        cp.start()
        out_copies.append(cp)

    def enqueue_rows(w_hbm, wb_ref, sid, after_last):
        rows = stages[sid].shape[1]
        n_chunks = w_hbm.shape[0] // rows
        for c in range(n_chunks):
            part = pl.ds(c * rows, rows)
            enqueue(w_hbm.at[part, :], sid, wb_ref.at[part, :], after_last if c == n_chunks - 1 else None)

    def mixer():
        for cp in in_copies:
            cp.wait()
        start_out(wout_ref, woutb_out, sem_wout.at[3])
        x = xbuf[:, 0, :]
        hb = _rms(x, g1_ref[...]).astype(BF16)
        proj = _dot(hb, win_ref[...])
        u = proj[:, :D_S5]
        v = proj[:, D_S5:]
        ub = u.astype(BF16)

        y_parts = []
        for k in range(S5_SPLIT):
            bu = _dot(ub[:, k * U_PER_SPLIT:(k + 1) * U_PER_SPLIT], bbd_ref[k])
            lanes = slice(k * ST_PER_SPLIT, (k + 1) * ST_PER_SPLIT)
            a_re = are_ref[:, lanes]
            a_im = aim_ref[:, lanes]
            h_re0 = sre_in[:, lanes]
            h_im0 = sim_in[:, lanes]
            h_re = a_re * h_re0 - a_im * h_im0 + bu[:, :ST_PER_SPLIT]
            h_im = a_re * h_im0 + a_im * h_re0 + bu[:, ST_PER_SPLIT:]
            sre_ref[:, lanes] = h_re
            sim_ref[:, lanes] = h_im
            hcat = jnp.concatenate([h_re, h_im], axis=1).astype(BF16)
            y_parts.append(_dot(hcat, cbd_ref[k]))
        y_s5 = _s5_post(jnp.concatenate(y_parts, axis=1), u, dskip_ref, wglu_ref)

        pooled = []
        for gi, w in enumerate(POOL_WINDOWS):
            lanes = slice(gi * POOL_CH, (gi + 1) * POOL_CH)
            vc = v[:, lanes]
            wsum = vc
            for back in range(1, w):
                wsum = wsum + poolbuf[POOL_BUF - back, :, lanes]
            pooled.append(wsum / float(w) - vc)
        y_pool = _pool_project(pooled, wpool_ref, pscale_ref)
        vbuf[...] = v
        start_out(poolbuf.at[pl.ds(1, POOL_BUF - 1)], pool_out.at[pl.ds(0, POOL_BUF - 1)], sem_sout.at[0])
        start_out(vbuf, pool_out.at[POOL_BUF - 1], sem_sout.at[1])
        start_out(convbuf.at[1], state_row(conv_out, 0), sem_sout.at[2])

        x1 = (x + _dot(y_s5.astype(BF16), wout_ref[:D_S5, :])
              + _dot(y_pool.astype(BF16), wout_ref[D_S5:, :]))
        live["x1"] = x1
        live["h2b"] = _rms(x1, g2_ref[...]).astype(BF16)
        live["acc"] = None

    def ffn_chunk(j):
        rows_j = pl.ds(j * FF_CHUNK, FF_CHUNK)
        convd = []
        for base in (0, D_FF):
            cols = slice(base + j * FF_CHUNK, base + (j + 1) * FF_CHUNK)
            start_out(wup_ref.at[:, cols], wupb_out.at[:, cols], sem_wout.at[4])
            hup = _dot(live["h2b"], wup_ref[:, cols])
            hupbuf[:, 0, cols] = hup
            convd.append(cb_ref[:, cols] + cw_ref[0:1, cols] * convbuf[0, :, 0, cols]
                         + cw_ref[1:2, cols] * convbuf[1, :, 0, cols] + cw_ref[2:3, cols] * hup)
        start_out(wdown_ref.at[rows_j, :], wdownb_out.at[rows_j, :], sem_wout.at[5])
        act = (_gelu(convd[0]) * convd[1]).astype(BF16)
        part = _dot(act, wdown_ref[j * FF_CHUNK:(j + 1) * FF_CHUNK, :])
        live["acc"] = part if live["acc"] is None else live["acc"] + part

    enqueue_rows(win_hbm, win_ref, 0, lambda: start_out(win_ref, winb_out, sem_wout.at[0]))
    enqueue_rows(wglu_hbm, wglu_ref, 1, lambda: start_out(wglu_ref, wglub_out, sem_wout.at[1]))
    enqueue_rows(wpool_hbm, wpool_ref, 2, lambda: start_out(wpool_ref, wpoolb_out, sem_wout.at[2]))
    enqueue_rows(wout_hbm, wout_ref, 0, mixer)
    for j in range(N_FF_CHUNKS):
        for base in (0, D_FF):
            cols = pl.ds(base + j * FF_CHUNK, FF_CHUNK)
            enqueue(wup_hbm.at[:, cols], 3, wup_ref.at[:, cols])
        rows_j = pl.ds(j * FF_CHUNK, FF_CHUNK)
        enqueue(wdown_hbm.at[rows_j, :], 0, wdown_ref.at[rows_j, :], functools.partial(ffn_chunk, j))

    def read(entry):
        src, sid, slot, _, _ = entry
        return pltpu.make_async_copy(src, stages[sid].at[slot], sem_stage.at[sem_base[sid] + slot])

    started = 0
    for i, entry in enumerate(queue):
        while started < min(len(queue), i + 1 + STAGE_LOOKAHEAD):
            read(queue[started]).start()
            started += 1
        read(entry).wait()
        _, sid, slot, dst, after = entry
        dst[...] = stages[sid][slot].astype(BF16)
        if after is not None:
            after()

    ybuf[:, 0, :] = _rms(live["x1"] + live["acc"], gf_ref[...])
    start_out(hupbuf, state_row(conv_out, 1), sem_sout.at[3])
    start_out(ybuf, y_hbm, sem_sout.at[4])
    for cp in out_copies:
        cp.wait()


def _s5_tables(a_re, a_im, log_dt, b_re, b_im, c_re, c_im):
    dt = jnp.exp(log_dt)[:, None]
    mag = jnp.exp(dt * a_re)
    abar_re = mag * jnp.cos(dt * a_im)
    abar_im = mag * jnp.sin(dt * a_im)
    nr, ni = abar_re - 1.0, abar_im
    den = a_re * a_re + a_im * a_im
    f_re = ((nr * a_re + ni * a_im) / den)[:, :, None]
    f_im = ((ni * a_re - nr * a_im) / den)[:, :, None]
    bbar_re = f_re * b_re - f_im * b_im
    bbar_im = f_re * b_im + f_im * b_re
    u_group = lax.broadcasted_iota(jnp.int32, (U_PER_SPLIT, ST_PER_SPLIT), 0) // S5_CH
    st_group = lax.broadcasted_iota(jnp.int32, (U_PER_SPLIT, ST_PER_SPLIT), 1) // S5_STATE
    diag = u_group == st_group

    def in_table(b):
        b = b.reshape(S5_SPLIT, GROUPS_PER_SPLIT, S5_STATE, S5_CH)
        row = jnp.transpose(b, (0, 3, 1, 2)).reshape(S5_SPLIT, 1, S5_CH, ST_PER_SPLIT)
        rep = jnp.broadcast_to(row, (S5_SPLIT, GROUPS_PER_SPLIT, S5_CH, ST_PER_SPLIT))
        return jnp.where(diag, rep.reshape(S5_SPLIT, U_PER_SPLIT, ST_PER_SPLIT), 0.0)

    def out_table(c):
        c = c.reshape(S5_SPLIT, GROUPS_PER_SPLIT, S5_CH, S5_STATE)
        col = jnp.transpose(c, (0, 1, 3, 2)).reshape(S5_SPLIT, ST_PER_SPLIT, S5_CH)
        rep = jnp.tile(col, (1, 1, GROUPS_PER_SPLIT))
        return jnp.where(diag.T, rep, 0.0)

    bbd = jnp.concatenate([in_table(bbar_re), in_table(bbar_im)], axis=2)
    cbd = jnp.concatenate([out_table(c_re), out_table(-c_im)], axis=1)
    return abar_re.reshape(1, -1), abar_im.reshape(1, -1), bbd.astype(BF16), cbd.astype(BF16)


def _vmem_spec():
    return pl.BlockSpec(memory_space=pltpu.VMEM)


PROMPT_TC = 64
PROMPT_IN_SLOTS = 3
VMEM_LIMIT_BYTES = 60 * 1024 * 1024


def kernel(x_prompt, x_sample, state_s5_re, state_s5_im, state_pool, state_ffn_conv, norm_mix_g, w_in, s5_a_re, s5_a_im, s5_log_dt, s5_b_re, s5_b_im, s5_c_re, s5_c_im, s5_d, s5_w_glu, pool_w, pool_scale, w_out, norm_ffn_g, ffn_w_up, ffn_conv_w, ffn_conv_b, ffn_w_down, norm_final_g):
    nb, seq, _ = x_prompt.shape
    ns = x_sample.shape[0]
    assert nb == SUBLANES and seq % PROMPT_TC == 0 and x_sample.shape[1] == 1
    assert norm_mix_g.shape[0] == 1, "single layer"

    a_re, a_im, bbd, cbd = _s5_tables(s5_a_re[0], s5_a_im[0], s5_log_dt[0], s5_b_re[0],
                                      s5_b_im[0], s5_c_re[0], s5_c_im[0])
    n_states = S5_GROUPS * S5_STATE
    cparams = dict(vmem_limit_bytes=VMEM_LIMIT_BYTES)
    g1 = norm_mix_g[0].reshape(1, D_MODEL)
    dskip = s5_d[0].reshape(1, D_S5)
    pscale = pool_scale[0].reshape(1, D_POOL)
    g2 = norm_ffn_g[0].reshape(1, D_MODEL)
    cw = ffn_conv_w[0]
    cb = ffn_conv_b[0].reshape(1, 2 * D_FF)
    gf = norm_final_g.reshape(1, D_MODEL)

    big_f32 = (w_in[0], s5_w_glu[0], pool_w[0].reshape(len(POOL_WINDOWS) * POOL_CH, POOL_CH),
               w_out[0], ffn_w_up[0], ffn_w_down[0])
    small = (g1, a_re, a_im, bbd, cbd, dskip, pscale, g2, cw, cb, gf)
    any_spec = pl.BlockSpec(memory_space=pl.ANY)
    pool_in = jnp.transpose(state_pool[0], (1, 0, 2))
    sample_out = pl.pallas_call(
        _sample_kernel,
        in_specs=[any_spec] + [_vmem_spec()] * 2 + [any_spec] * 2 + [_vmem_spec()] * len(small)
                 + [any_spec] * N_BIG_WEIGHTS,
        out_specs=[any_spec] + [_vmem_spec()] * 2 + [any_spec] * (2 + N_BIG_WEIGHTS),
        out_shape=[
            jax.ShapeDtypeStruct((ns, 1, D_MODEL), F32),
            jax.ShapeDtypeStruct((ns, n_states), F32),
            jax.ShapeDtypeStruct((ns, n_states), F32),
            jax.ShapeDtypeStruct((POOL_BUF, ns, D_POOL), F32),
            jax.ShapeDtypeStruct((ns, CONV_W - 1, 2 * D_FF), F32),
        ] + [jax.ShapeDtypeStruct(w.shape, BF16) for w in big_f32],
        scratch_shapes=[pltpu.VMEM(w.shape, BF16) for w in big_f32]
        + [pltpu.VMEM(shape, F32) for shape in STAGE_SHAPES] + [
            pltpu.VMEM((ns, 1, D_MODEL), F32),
            pltpu.VMEM((ns, 1, D_MODEL), F32),
            pltpu.VMEM((POOL_BUF, ns, D_POOL), F32),
            pltpu.VMEM((CONV_W - 1, ns, 1, 2 * D_FF), F32),
            pltpu.VMEM((ns, 1, 2 * D_FF), F32),
            pltpu.VMEM((ns, D_POOL), F32),
            pltpu.SemaphoreType.DMA((sum(shape[0] for shape in STAGE_SHAPES),)),
            pltpu.SemaphoreType.DMA((3,)),
            pltpu.SemaphoreType.DMA((N_BIG_WEIGHTS,)),
            pltpu.SemaphoreType.DMA((5,)),
        ],
        compiler_params=pltpu.CompilerParams(**cparams),
        name="sample_layer",
    )(x_sample, state_s5_re[0].reshape(ns, n_states), state_s5_im[0].reshape(ns, n_states),
      pool_in, state_ffn_conv[0], *small, *big_f32)
    ys, s_re, s_im, s_pool, s_conv = sample_out[:5]
    win_b, wglu_b, wpool_b, wout_b, wup_b, wdown_b = sample_out[5:]
    weights = (g1, win_b, a_re, a_im, bbd, cbd, dskip, wglu_b, wpool_b, pscale, wout_b, g2,
               wup_b, cw, cb, wdown_b, gf)

    tc = PROMPT_TC
    rows = tc * SUBLANES
    const = lambda i: (0, 0)
    y_prompt, p_re, p_im, p_pool, p_conv = pl.pallas_call(
        functools.partial(_prompt_kernel, tc, seq // tc),
        grid=(seq // tc,),
        in_specs=[pl.BlockSpec(memory_space=pl.ANY)] + [_vmem_spec()] * len(weights),
        out_specs=[
            pl.BlockSpec(memory_space=pl.ANY),
            pl.BlockSpec((SUBLANES, n_states), const),
            pl.BlockSpec((SUBLANES, n_states), const),
            pl.BlockSpec((POOL_BUF * SUBLANES, D_POOL), const),
            pl.BlockSpec(((CONV_W - 1) * SUBLANES, 2 * D_FF), const),
        ],
        out_shape=[
            jax.ShapeDtypeStruct((nb, seq, D_MODEL), F32),
            jax.ShapeDtypeStruct((SUBLANES, n_states), F32),
            jax.ShapeDtypeStruct((SUBLANES, n_states), F32),
            jax.ShapeDtypeStruct((POOL_BUF * SUBLANES, D_POOL), F32),
            jax.ShapeDtypeStruct(((CONV_W - 1) * SUBLANES, 2 * D_FF), F32),
        ],
        scratch_shapes=[
            pltpu.VMEM((S5_SPLIT, rows, 2 * ST_PER_SPLIT), BF16),
            pltpu.VMEM((rows, D_FF), BF16),
            pltpu.VMEM((2, rows, D_MODEL), BF16),
            pltpu.VMEM((2, rows, D_MODEL), F32),
            pltpu.VMEM((PROMPT_IN_SLOTS, tc, SUBLANES, D_MODEL), F32),
            pltpu.VMEM((2, tc, SUBLANES, D_MODEL), F32),
            pltpu.SemaphoreType.DMA((PROMPT_IN_SLOTS,)),
            pltpu.SemaphoreType.DMA((2,)),
        ],
        compiler_params=pltpu.CompilerParams(dimension_semantics=("arbitrary",), **cparams),
        name="prompt_layer",
    )(x_prompt, *weights)
    new_pool_p = jnp.transpose(p_pool.reshape(POOL_BUF, nb, D_POOL), (1, 0, 2))[None]
    new_conv_p = jnp.transpose(p_conv.reshape(CONV_W - 1, nb, 2 * D_FF), (1, 0, 2))[None]
    new_re_p = p_re.reshape(1, nb, S5_GROUPS, S5_STATE)
    new_im_p = p_im.reshape(1, nb, S5_GROUPS, S5_STATE)

    return (y_prompt, ys, new_re_p, new_im_p, new_pool_p, new_conv_p,
            s_re.reshape(1, ns, S5_GROUPS, S5_STATE), s_im.reshape(1, ns, S5_GROUPS, S5_STATE),
            jnp.transpose(s_pool, (1, 0, 2))[None], s_conv[None])
```

```python
import functools
import math

import numpy as np
import jax
import jax.numpy as jnp
from jax import lax
from jax.experimental import pallas as pl
from jax.experimental.pallas import tpu as pltpu

D_MODEL = 1024
D_S5 = 512
S5_CH = 16
S5_GROUPS = 32
S5_STATE = 64
D_POOL = 512
POOL_WINDOWS = (2, 4, 8, 16)
POOL_CH = 128
POOL_BUF = 15
D_FF = 2816
CONV_W = 3
EPS = 1e-6

SUBLANES = 8
S5_SPLIT = 2
GROUPS_PER_SPLIT = S5_GROUPS // S5_SPLIT
U_PER_SPLIT = GROUPS_PER_SPLIT * S5_CH
ST_PER_SPLIT = GROUPS_PER_SPLIT * S5_STATE
FF_CHUNK = 256
N_FF_CHUNKS = D_FF // FF_CHUNK
SQRT_HALF = float(np.sqrt(0.5).astype(np.float32))

BF16 = jnp.bfloat16
F32 = jnp.float32


def _dot(a, b):
    return jnp.dot(a, b, preferred_element_type=F32)


def _rms(x, g):
    ms = jnp.mean(x * x, axis=-1, keepdims=True)
    return x * lax.rsqrt(ms + EPS) * g


def _gelu(x):
    return 0.5 * x * (1.0 + lax.erf(x * SQRT_HALF))


def _s5_post(y_lin, u, dskip_ref, wglu_ref):
    y = _gelu(y_lin + dskip_ref[...] * u)
    return y * jax.nn.sigmoid(_dot(y.astype(BF16), wglu_ref[...]))


def _pool_project(pooled_cols, wpool_ref, pscale_ref):
    zero = jnp.zeros((POOL_CH, POOL_CH), BF16)
    outs = []
    for pair in range(len(POOL_WINDOWS) // 2):
        w_a = wpool_ref[(2 * pair) * POOL_CH:(2 * pair + 1) * POOL_CH, :]
        w_b = wpool_ref[(2 * pair + 1) * POOL_CH:(2 * pair + 2) * POOL_CH, :]
        w_pair = jnp.concatenate([jnp.concatenate([w_a, zero], axis=1),
                                  jnp.concatenate([zero, w_b], axis=1)], axis=0)
        lhs = jnp.concatenate([pooled_cols[2 * pair], pooled_cols[2 * pair + 1]], axis=1)
        outs.append(_dot(lhs.astype(BF16), w_pair))
    return jnp.concatenate(outs, axis=1) * pscale_ref[...]


def _block_copies(hbm_ref, buf_ref, sem_ref, block, slot, tc, to_hbm):
    copies = []
    for n in range(SUBLANES):
        hbm = hbm_ref.at[n, pl.ds(block * tc, tc), :]
        vmem = buf_ref.at[slot, :, n, :]
        src, dst = (vmem, hbm) if to_hbm else (hbm, vmem)
        copies.append(pltpu.make_async_copy(src, dst, sem_ref.at[slot]))
    return copies


def _prompt_kernel(tc, n_steps,
                   x_hbm, g1_ref, win_ref, are_ref, aim_ref, bbd_ref, cbd_ref, dskip_ref,
                   wglu_ref, wpool_ref, pscale_ref, wout_ref, g2_ref, wup_ref, cw_ref,
                   cb_ref, wdown_ref, gf_ref,
                   y_hbm, sre_ref, sim_ref, pool_ref, conv_ref,
                   h_ref, act_ref, xbuf, ybuf, sem_in, sem_out):
    rows = tc * SUBLANES
    step = pl.program_id(0)
    slot = step % 2

    @pl.when(step == 0)
    def _():
        sre_ref[...] = jnp.zeros_like(sre_ref)
        sim_ref[...] = jnp.zeros_like(sim_ref)
        pool_ref[...] = jnp.zeros_like(pool_ref)
        conv_ref[...] = jnp.zeros_like(conv_ref)
        for cp in _block_copies(x_hbm, xbuf, sem_in, 0, 0, tc, False):
            cp.start()

    @pl.when(step + 1 < n_steps)
    def _():
        for cp in _block_copies(x_hbm, xbuf, sem_in, step + 1, 1 - slot, tc, False):
            cp.start()

    @pl.when(step >= 2)
    def _():
        for cp in _block_copies(y_hbm, ybuf, sem_out, step - 2, slot, tc, True):
            cp.wait()

    for cp in _block_copies(x_hbm, xbuf, sem_in, step, slot, tc, False):
        cp.wait()

    x = xbuf[slot].reshape(rows, D_MODEL)
    hb = _rms(x, g1_ref[...]).astype(BF16)
    proj = _dot(hb, win_ref[...])
    u = proj[:, :D_S5]
    v = proj[:, D_S5:]
    ub = u.astype(BF16)

    bus = [_dot(ub[:, k * U_PER_SPLIT:(k + 1) * U_PER_SPLIT], bbd_ref[k]) for k in range(S5_SPLIT)]
    y_parts = []
    for k in range(S5_SPLIT):
        bu = bus[k]
        lanes = slice(k * ST_PER_SPLIT, (k + 1) * ST_PER_SPLIT)
        a_re = jnp.broadcast_to(are_ref[:, lanes], (SUBLANES, ST_PER_SPLIT))
        a_im = jnp.broadcast_to(aim_ref[:, lanes], (SUBLANES, ST_PER_SPLIT))
        h_re, h_im = sre_ref[:, lanes], sim_ref[:, lanes]
        for i in range(tc // 2):
            r0 = i * 2 * SUBLANES
            res, ims = [], []
            for s in range(2):
                rs = slice(r0 + s * SUBLANES, r0 + (s + 1) * SUBLANES)
                n_re = a_re * h_re - a_im * h_im + bu[rs, :ST_PER_SPLIT]
                n_im = a_re * h_im + a_im * h_re + bu[rs, ST_PER_SPLIT:]
                h_re, h_im = n_re, n_im
                res.append(h_re)
                ims.append(h_im)
            pair = slice(r0, r0 + 2 * SUBLANES)
            h_ref[k, pair, :ST_PER_SPLIT] = jnp.concatenate(res, axis=0).astype(BF16)
            h_ref[k, pair, ST_PER_SPLIT:] = jnp.concatenate(ims, axis=0).astype(BF16)
        sre_ref[:, lanes] = h_re
        sim_ref[:, lanes] = h_im
        half = rows // 2
        y_parts.append(jnp.concatenate(
            [_dot(h_ref[k, :half, :], cbd_ref[k]), _dot(h_ref[k, half:, :], cbd_ref[k])], axis=0))
    y_s5 = _s5_post(jnp.concatenate(y_parts, axis=1), u, dskip_ref, wglu_ref)

    halo = POOL_BUF * SUBLANES
    vfull = jnp.concatenate([pool_ref[...], v], axis=0)
    pool_ref[...] = vfull[rows:, :]
    t_idx = step * tc + (lax.broadcasted_iota(jnp.int32, (rows, POOL_CH), 0) >> 3)
    pooled = []
    for gi, w in enumerate(POOL_WINDOWS):
        s = vfull[:, gi * POOL_CH:(gi + 1) * POOL_CH]
        span = 1
        while span < w:
            sh = span * SUBLANES
            s = s[sh:, :] + s[:-sh, :]
            span *= 2
        first = (POOL_BUF - (w - 1)) * SUBLANES
        wsum = s[first:first + rows, :]
        cnt = jnp.minimum(t_idx + 1, w).astype(F32)
        pooled.append(wsum / cnt - v[:, gi * POOL_CH:(gi + 1) * POOL_CH])
    y_pool = _pool_project(pooled, wpool_ref, pscale_ref)

    x1 = x + _dot(y_s5.astype(BF16), wout_ref[:D_S5, :]) + _dot(y_pool.astype(BF16), wout_ref[D_S5:, :])

    h2b = _rms(x1, g2_ref[...]).astype(BF16)
    taps = (CONV_W - 1) * SUBLANES
    for j in range(N_FF_CHUNKS):
        convd = []
        for base in (0, D_FF):
            cols = slice(base + j * FF_CHUNK, base + (j + 1) * FF_CHUNK)
            hup = _dot(h2b, wup_ref[:, cols])
            full = jnp.concatenate([conv_ref[:, cols], hup], axis=0)
            conv_ref[:, cols] = hup[rows - taps:, :]
            c = cb_ref[:, cols]
            for kk in range(CONV_W):
                c = c + cw_ref[kk:kk + 1, cols] * full[kk * SUBLANES:kk * SUBLANES + rows, :]
            convd.append(c)
        act_ref[:, j * FF_CHUNK:(j + 1) * FF_CHUNK] = (_gelu(convd[0]) * convd[1]).astype(BF16)
    y = _rms(x1 + _dot(act_ref[...], wdown_ref[...]), gf_ref[...])

    ybuf[slot] = y.reshape(tc, SUBLANES, D_MODEL)
    for cp in _block_copies(y_hbm, ybuf, sem_out, step, slot, tc, True):
        cp.start()

    @pl.when(step == n_steps - 1)
    def _():
        if n_steps >= 2:
            for cp in _block_copies(y_hbm, ybuf, sem_out, step - 1, 1 - slot, tc, True):
                cp.wait()
        for cp in _block_copies(y_hbm, ybuf, sem_out, step, slot, tc, True):
            cp.wait()


N_BIG_WEIGHTS = 6
STAGE_SHAPES = ((4, 256, D_MODEL), (2, 256, D_S5), (2, 256, POOL_CH), (4, D_MODEL, FF_CHUNK))
STAGE_LOOKAHEAD = 3


def _sample_kernel(x_hbm, sre_in, sim_in, pool_hbm, conv_hbm,
                   g1_ref, are_ref, aim_ref, bbd_ref, cbd_ref, dskip_ref, pscale_ref, g2_ref,
                   cw_ref, cb_ref, gf_ref,
                   win_hbm, wglu_hbm, wpool_hbm, wout_hbm, wup_hbm, wdown_hbm,
                   y_hbm, sre_ref, sim_ref, pool_out, conv_out,
                   winb_out, wglub_out, wpoolb_out, woutb_out, wupb_out, wdownb_out,
                   win_ref, wglu_ref, wpool_ref, wout_ref, wup_ref, wdown_ref,
                   stage_model, stage_s5, stage_pool, stage_col,
                   xbuf, ybuf, poolbuf, convbuf, hupbuf, vbuf,
                   sem_stage, sem_state, sem_wout, sem_sout):
    def state_row(hbm_ref, r):
        return hbm_ref.at[:, pl.ds(r, 1), :]

    in_copies = [pltpu.make_async_copy(x_hbm, xbuf, sem_state.at[0]),
                 pltpu.make_async_copy(pool_hbm, poolbuf, sem_state.at[1])]
    in_copies += [pltpu.make_async_copy(state_row(conv_hbm, r), convbuf.at[r], sem_state.at[2])
                  for r in range(CONV_W - 1)]
    for cp in in_copies:
        cp.start()

    stages = (stage_model, stage_s5, stage_pool, stage_col)
    sem_base = [sum(s.shape[0] for s in stages[:i]) for i in range(len(stages))]
    ring_pos = [0] * len(stages)
    queue = []
    out_copies = []
    live = {}

    def enqueue(src, sid, dst, after=None):
        queue.append((src, sid, ring_pos[sid] % stages[sid].shape[0], dst, after))
        ring_pos[sid] += 1

    def start_out(src, dst, sem):
        cp = pltpu.make_async_copy(src, dst, sem)
        cp.start()
        out_copies.append(cp)

    def enqueue_rows(w_hbm, wb_ref, sid, after_last):
        rows = stages[sid].shape[1]
        n_chunks = w_hbm.shape[0] // rows
        for c in range(n_chunks):
            part = pl.ds(c * rows, rows)
            enqueue(w_hbm.at[part, :], sid, wb_ref.at[part, :], after_last if c == n_chunks - 1 else None)

    def mixer():
        for cp in in_copies:
            cp.wait()
        start_out(wout_ref, woutb_out, sem_wout.at[3])
        x = xbuf[:, 0, :]
        hb = _rms(x, g1_ref[...]).astype(BF16)
        proj = _dot(hb, win_ref[...])
        u = proj[:, :D_S5]
        v = proj[:, D_S5:]
        ub = u.astype(BF16)

        y_parts = []
        for k in range(S5_SPLIT):
            bu = _dot(ub[:, k * U_PER_SPLIT:(k + 1) * U_PER_SPLIT], bbd_ref[k])
            lanes = slice(k * ST_PER_SPLIT, (k + 1) * ST_PER_SPLIT)
            a_re = are_ref[:, lanes]
            a_im = aim_ref[:, lanes]
            h_re0 = sre_in[lanes, :].T
            h_im0 = sim_in[lanes, :].T
            h_re = a_re * h_re0 - a_im * h_im0 + bu[:, :ST_PER_SPLIT]
            h_im = a_re * h_im0 + a_im * h_re0 + bu[:, ST_PER_SPLIT:]
            sre_ref[lanes, :] = h_re.T
            sim_ref[lanes, :] = h_im.T
            hcat = jnp.concatenate([h_re, h_im], axis=1).astype(BF16)
            y_parts.append(_dot(hcat, cbd_ref[k]))
        y_s5 = _s5_post(jnp.concatenate(y_parts, axis=1), u, dskip_ref, wglu_ref)

        pooled = []
        for gi, w in enumerate(POOL_WINDOWS):
            lanes = slice(gi * POOL_CH, (gi + 1) * POOL_CH)
            vc = v[:, lanes]
            wsum = vc
            for back in range(1, w):
                wsum = wsum + poolbuf[POOL_BUF - back, :, lanes]
            pooled.append(wsum / float(w) - vc)
        y_pool = _pool_project(pooled, wpool_ref, pscale_ref)
        vbuf[...] = v
        start_out(poolbuf.at[pl.ds(1, POOL_BUF - 1)], pool_out.at[pl.ds(0, POOL_BUF - 1)], sem_sout.at[0])
        start_out(vbuf, pool_out.at[POOL_BUF - 1], sem_sout.at[1])
        start_out(convbuf.at[1], state_row(conv_out, 0), sem_sout.at[2])

        x1 = (x + _dot(y_s5.astype(BF16), wout_ref[:D_S5, :])
              + _dot(y_pool.astype(BF16), wout_ref[D_S5:, :]))
        live["x1"] = x1
        live["h2b"] = _rms(x1, g2_ref[...]).astype(BF16)
        live["acc"] = None

    def ffn_chunk(j):
        rows_j = pl.ds(j * FF_CHUNK, FF_CHUNK)
        convd = []
        for base in (0, D_FF):
            cols = slice(base + j * FF_CHUNK, base + (j + 1) * FF_CHUNK)
            start_out(wup_ref.at[:, cols], wupb_out.at[:, cols], sem_wout.at[4])
            hup = _dot(live["h2b"], wup_ref[:, cols])
            hupbuf[:, 0, cols] = hup
            convd.append(cb_ref[:, cols] + cw_ref[0:1, cols] * convbuf[0, :, 0, cols]
                         + cw_ref[1:2, cols] * convbuf[1, :, 0, cols] + cw_ref[2:3, cols] * hup)
        start_out(wdown_ref.at[rows_j, :], wdownb_out.at[rows_j, :], sem_wout.at[5])
        act = (_gelu(convd[0]) * convd[1]).astype(BF16)
        part = _dot(act, wdown_ref[j * FF_CHUNK:(j + 1) * FF_CHUNK, :])
        live["acc"] = part if live["acc"] is None else live["acc"] + part

    enqueue_rows(win_hbm, win_ref, 0, lambda: start_out(win_ref, winb_out, sem_wout.at[0]))
    enqueue_rows(wglu_hbm, wglu_ref, 1, lambda: start_out(wglu_ref, wglub_out, sem_wout.at[1]))
    enqueue_rows(wpool_hbm, wpool_ref, 2, lambda: start_out(wpool_ref, wpoolb_out, sem_wout.at[2]))
    enqueue_rows(wout_hbm, wout_ref, 0, mixer)
    for j in range(N_FF_CHUNKS):
        for base in (0, D_FF):
            cols = pl.ds(base + j * FF_CHUNK, FF_CHUNK)
            enqueue(wup_hbm.at[:, cols], 3, wup_ref.at[:, cols])
        rows_j = pl.ds(j * FF_CHUNK, FF_CHUNK)
        enqueue(wdown_hbm.at[rows_j, :], 0, wdown_ref.at[rows_j, :], functools.partial(ffn_chunk, j))

    def read(entry):
        src, sid, slot, _, _ = entry
        return pltpu.make_async_copy(src, stages[sid].at[slot], sem_stage.at[sem_base[sid] + slot])

    started = 0
    for i, entry in enumerate(queue):
        while started < min(len(queue), i + 1 + STAGE_LOOKAHEAD):
            read(queue[started]).start()
            started += 1
        read(entry).wait()
        _, sid, slot, dst, after = entry
        dst[...] = stages[sid][slot].astype(BF16)
        if after is not None:
            after()

    ybuf[:, 0, :] = _rms(live["x1"] + live["acc"], gf_ref[...])
    start_out(hupbuf, state_row(conv_out, 1), sem_sout.at[3])
    start_out(ybuf, y_hbm, sem_sout.at[4])
    for cp in out_copies:
        cp.wait()


def _s5_tables(a_re, a_im, log_dt, b_re, b_im, c_re, c_im):
    dt = jnp.exp(log_dt)[:, None]
    mag = jnp.exp(dt * a_re)
    abar_re = mag * jnp.cos(dt * a_im)
    abar_im = mag * jnp.sin(dt * a_im)
    nr, ni = abar_re - 1.0, abar_im
    den = a_re * a_re + a_im * a_im
    f_re = ((nr * a_re + ni * a_im) / den)[:, :, None]
    f_im = ((ni * a_re - nr * a_im) / den)[:, :, None]
    bbar_re = f_re * b_re - f_im * b_im
    bbar_im = f_re * b_im + f_im * b_re
    u_group = lax.broadcasted_iota(jnp.int32, (U_PER_SPLIT, ST_PER_SPLIT), 0) // S5_CH
    st_group = lax.broadcasted_iota(jnp.int32, (U_PER_SPLIT, ST_PER_SPLIT), 1) // S5_STATE
    diag = u_group == st_group

    def in_table(b):
        b = b.reshape(S5_SPLIT, GROUPS_PER_SPLIT, S5_STATE, S5_CH)
        row = jnp.transpose(b, (0, 3, 1, 2)).reshape(S5_SPLIT, 1, S5_CH, ST_PER_SPLIT)
        rep = jnp.broadcast_to(row, (S5_SPLIT, GROUPS_PER_SPLIT, S5_CH, ST_PER_SPLIT))
        return jnp.where(diag, rep.reshape(S5_SPLIT, U_PER_SPLIT, ST_PER_SPLIT), 0.0)

    def out_table(c):
        c = c.reshape(S5_SPLIT, GROUPS_PER_SPLIT, S5_CH, S5_STATE)
        col = jnp.transpose(c, (0, 1, 3, 2)).reshape(S5_SPLIT, ST_PER_SPLIT, S5_CH)
        rep = jnp.tile(col, (1, 1, GROUPS_PER_SPLIT))
        return jnp.where(diag.T, rep, 0.0)

    bbd = jnp.concatenate([in_table(bbar_re), in_table(bbar_im)], axis=2)
    cbd = jnp.concatenate([out_table(c_re), out_table(-c_im)], axis=1)
    return abar_re.reshape(1, -1), abar_im.reshape(1, -1), bbd.astype(BF16), cbd.astype(BF16)


def _vmem_spec():
    return pl.BlockSpec(memory_space=pltpu.VMEM)


PROMPT_TC = 64
VMEM_LIMIT_BYTES = 60 * 1024 * 1024


def kernel(x_prompt, x_sample, state_s5_re, state_s5_im, state_pool, state_ffn_conv, norm_mix_g, w_in, s5_a_re, s5_a_im, s5_log_dt, s5_b_re, s5_b_im, s5_c_re, s5_c_im, s5_d, s5_w_glu, pool_w, pool_scale, w_out, norm_ffn_g, ffn_w_up, ffn_conv_w, ffn_conv_b, ffn_w_down, norm_final_g):
    nb, seq, _ = x_prompt.shape
    ns = x_sample.shape[0]
    assert nb == SUBLANES and seq % PROMPT_TC == 0 and x_sample.shape[1] == 1
    assert norm_mix_g.shape[0] == 1, "single layer"

    a_re, a_im, bbd, cbd = _s5_tables(s5_a_re[0], s5_a_im[0], s5_log_dt[0], s5_b_re[0],
                                      s5_b_im[0], s5_c_re[0], s5_c_im[0])
    n_states = S5_GROUPS * S5_STATE
    cparams = dict(vmem_limit_bytes=VMEM_LIMIT_BYTES)
    g1 = norm_mix_g[0].reshape(1, D_MODEL)
    dskip = s5_d[0].reshape(1, D_S5)
    pscale = pool_scale[0].reshape(1, D_POOL)
    g2 = norm_ffn_g[0].reshape(1, D_MODEL)
    cw = ffn_conv_w[0]
    cb = ffn_conv_b[0].reshape(1, 2 * D_FF)
    gf = norm_final_g.reshape(1, D_MODEL)

    big_f32 = (w_in[0], s5_w_glu[0], pool_w[0].reshape(len(POOL_WINDOWS) * POOL_CH, POOL_CH),
               w_out[0], ffn_w_up[0], ffn_w_down[0])
    small = (g1, a_re, a_im, bbd, cbd, dskip, pscale, g2, cw, cb, gf)
    any_spec = pl.BlockSpec(memory_space=pl.ANY)
    pool_in = jnp.transpose(state_pool[0], (1, 0, 2))

    def state_major(s):
        return jnp.transpose(s, (1, 2, 0)).reshape(n_states, ns)

    def sequence_major(s):
        return jnp.transpose(s.reshape(S5_GROUPS, S5_STATE, ns), (2, 0, 1))[None]
    sample_out = pl.pallas_call(
        _sample_kernel,
        in_specs=[any_spec] + [_vmem_spec()] * 2 + [any_spec] * 2 + [_vmem_spec()] * len(small)
                 + [any_spec] * N_BIG_WEIGHTS,
        out_specs=[any_spec] + [_vmem_spec()] * 2 + [any_spec] * (2 + N_BIG_WEIGHTS),
        out_shape=[
            jax.ShapeDtypeStruct((ns, 1, D_MODEL), F32),
            jax.ShapeDtypeStruct((n_states, ns), F32),
            jax.ShapeDtypeStruct((n_states, ns), F32),
            jax.ShapeDtypeStruct((POOL_BUF, ns, D_POOL), F32),
            jax.ShapeDtypeStruct((ns, CONV_W - 1, 2 * D_FF), F32),
        ] + [jax.ShapeDtypeStruct(w.shape, BF16) for w in big_f32],
        scratch_shapes=[pltpu.VMEM(w.shape, BF16) for w in big_f32]
        + [pltpu.VMEM(shape, F32) for shape in STAGE_SHAPES] + [
            pltpu.VMEM((ns, 1, D_MODEL), F32),
            pltpu.VMEM((ns, 1, D_MODEL), F32),
            pltpu.VMEM((POOL_BUF, ns, D_POOL), F32),
            pltpu.VMEM((CONV_W - 1, ns, 1, 2 * D_FF), F32),
            pltpu.VMEM((ns, 1, 2 * D_FF), F32),
            pltpu.VMEM((ns, D_POOL), F32),
            pltpu.SemaphoreType.DMA((sum(shape[0] for shape in STAGE_SHAPES),)),
            pltpu.SemaphoreType.DMA((3,)),
            pltpu.SemaphoreType.DMA((N_BIG_WEIGHTS,)),
            pltpu.SemaphoreType.DMA((5,)),
        ],
        compiler_params=pltpu.CompilerParams(**cparams),
        name="sample_layer",
    )(x_sample, state_major(state_s5_re[0]), state_major(state_s5_im[0]),
      pool_in, state_ffn_conv[0], *small, *big_f32)
    ys, s_re, s_im, s_pool, s_conv = sample_out[:5]
    win_b, wglu_b, wpool_b, wout_b, wup_b, wdown_b = sample_out[5:]
    weights = (g1, win_b, a_re, a_im, bbd, cbd, dskip, wglu_b, wpool_b, pscale, wout_b, g2,
               wup_b, cw, cb, wdown_b, gf)

    tc = PROMPT_TC
    rows = tc * SUBLANES
    const = lambda i: (0, 0)
    y_prompt, p_re, p_im, p_pool, p_conv = pl.pallas_call(
        functools.partial(_prompt_kernel, tc, seq // tc),
        grid=(seq // tc,),
        in_specs=[pl.BlockSpec(memory_space=pl.ANY)] + [_vmem_spec()] * len(weights),
        out_specs=[
            pl.BlockSpec(memory_space=pl.ANY),
            pl.BlockSpec((SUBLANES, n_states), const),
            pl.BlockSpec((SUBLANES, n_states), const),
            pl.BlockSpec((POOL_BUF * SUBLANES, D_POOL), const),
            pl.BlockSpec(((CONV_W - 1) * SUBLANES, 2 * D_FF), const),
        ],
        out_shape=[
            jax.ShapeDtypeStruct((nb, seq, D_MODEL), F32),
            jax.ShapeDtypeStruct((SUBLANES, n_states), F32),
            jax.ShapeDtypeStruct((SUBLANES, n_states), F32),
            jax.ShapeDtypeStruct((POOL_BUF * SUBLANES, D_POOL), F32),
            jax.ShapeDtypeStruct(((CONV_W - 1) * SUBLANES, 2 * D_FF), F32),
        ],
        scratch_shapes=[
            pltpu.VMEM((S5_SPLIT, rows, 2 * ST_PER_SPLIT), BF16),
            pltpu.VMEM((rows, D_FF), BF16),
            pltpu.VMEM((2, tc, SUBLANES, D_MODEL), F32),
            pltpu.VMEM((2, tc, SUBLANES, D_MODEL), F32),
            pltpu.SemaphoreType.DMA((2,)),
            pltpu.SemaphoreType.DMA((2,)),
        ],
        compiler_params=pltpu.CompilerParams(dimension_semantics=("arbitrary",), **cparams),
        name="prompt_layer",
    )(x_prompt, *weights)
    new_pool_p = jnp.transpose(p_pool.reshape(POOL_BUF, nb, D_POOL), (1, 0, 2))[None]
    new_conv_p = jnp.transpose(p_conv.reshape(CONV_W - 1, nb, 2 * D_FF), (1, 0, 2))[None]
    new_re_p = p_re.reshape(1, nb, S5_GROUPS, S5_STATE)
    new_im_p = p_im.reshape(1, nb, S5_GROUPS, S5_STATE)

    return (y_prompt, ys, new_re_p, new_im_p, new_pool_p, new_conv_p,
            sequence_major(s_re), sequence_major(s_im),
            jnp.transpose(s_pool, (1, 0, 2))[None], s_conv[None])
```

```python
import functools
import math

import numpy as np
import jax
import jax.numpy as jnp
from jax import lax
from jax.experimental import pallas as pl
from jax.experimental.pallas import tpu as pltpu

D_MODEL = 1024
D_S5 = 512
S5_CH = 16
S5_GROUPS = 32
S5_STATE = 64
D_POOL = 512
POOL_WINDOWS = (2, 4, 8, 16)
POOL_CH = 128
POOL_BUF = 15
D_FF = 2816
CONV_W = 3
EPS = 1e-6

SUBLANES = 8
S5_SPLIT = 2
GROUPS_PER_SPLIT = S5_GROUPS // S5_SPLIT
U_PER_SPLIT = GROUPS_PER_SPLIT * S5_CH
ST_PER_SPLIT = GROUPS_PER_SPLIT * S5_STATE
FF_CHUNK = 256
N_FF_CHUNKS = D_FF // FF_CHUNK
SQRT_HALF = float(np.sqrt(0.5).astype(np.float32))

BF16 = jnp.bfloat16
F32 = jnp.float32


def _dot(a, b):
    return jnp.dot(a, b, preferred_element_type=F32)


def _rms(x, g):
    ms = jnp.mean(x * x, axis=-1, keepdims=True)
    return x * lax.rsqrt(ms + EPS) * g


def _gelu(x):
    return 0.5 * x * (1.0 + lax.erf(x * SQRT_HALF))


def _s5_post(y_lin, u, dskip_ref, wglu_ref):
    y = _gelu(y_lin + dskip_ref[...] * u)
    return y * jax.nn.sigmoid(_dot(y.astype(BF16), wglu_ref[...]))


def _pool_project(pooled_cols, wpool_ref, pscale_ref):
    zero = jnp.zeros((POOL_CH, POOL_CH), BF16)
    outs = []
    for pair in range(len(POOL_WINDOWS) // 2):
        w_a = wpool_ref[(2 * pair) * POOL_CH:(2 * pair + 1) * POOL_CH, :]
        w_b = wpool_ref[(2 * pair + 1) * POOL_CH:(2 * pair + 2) * POOL_CH, :]
        w_pair = jnp.concatenate([jnp.concatenate([w_a, zero], axis=1),
                                  jnp.concatenate([zero, w_b], axis=1)], axis=0)
        lhs = jnp.concatenate([pooled_cols[2 * pair], pooled_cols[2 * pair + 1]], axis=1)
        outs.append(_dot(lhs.astype(BF16), w_pair))
    return jnp.concatenate(outs, axis=1) * pscale_ref[...]


def _block_copies(hbm_ref, buf_ref, sem_ref, block, slot, tc, to_hbm):
    copies = []
    for n in range(SUBLANES):
        hbm = hbm_ref.at[n, pl.ds(block * tc, tc), :]
        vmem = buf_ref.at[slot, :, n, :]
        src, dst = (vmem, hbm) if to_hbm else (hbm, vmem)
        copies.append(pltpu.make_async_copy(src, dst, sem_ref.at[slot]))
    return copies


def _prompt_kernel(tc, n_steps,
                   x_hbm, g1_ref, win_ref, are_ref, aim_ref, bbd_ref, cbd_ref, dskip_ref,
                   wglu_ref, wpool_ref, pscale_ref, wout_ref, g2_ref, wup_ref, cw_ref,
                   cb_ref, wdown_ref, gf_ref,
                   y_hbm, sre_ref, sim_ref, pool_ref, conv_ref,
                   h_ref, act_ref, xbuf, ybuf, sem_in, sem_out):
    rows = tc * SUBLANES
    step = pl.program_id(0)
    slot = step % 2

    @pl.when(step == 0)
    def _():
        sre_ref[...] = jnp.zeros_like(sre_ref)
        sim_ref[...] = jnp.zeros_like(sim_ref)
        pool_ref[...] = jnp.zeros_like(pool_ref)
        conv_ref[...] = jnp.zeros_like(conv_ref)
        for cp in _block_copies(x_hbm, xbuf, sem_in, 0, 0, tc, False):
            cp.start()

    @pl.when(step + 1 < n_steps)
    def _():
        for cp in _block_copies(x_hbm, xbuf, sem_in, step + 1, 1 - slot, tc, False):
            cp.start()

    @pl.when(step >= 2)
    def _():
        for cp in _block_copies(y_hbm, ybuf, sem_out, step - 2, slot, tc, True):
            cp.wait()

    for cp in _block_copies(x_hbm, xbuf, sem_in, step, slot, tc, False):
        cp.wait()

    x = xbuf[slot].reshape(rows, D_MODEL)
    hb = _rms(x, g1_ref[...]).astype(BF16)
    proj = _dot(hb, win_ref[...])
    u = proj[:, :D_S5]
    v = proj[:, D_S5:]
    ub = u.astype(BF16)

    bus = [_dot(ub[:, k * U_PER_SPLIT:(k + 1) * U_PER_SPLIT], bbd_ref[k]) for k in range(S5_SPLIT)]
    y_parts = []
    for k in range(S5_SPLIT):
        bu = bus[k]
        lanes = slice(k * ST_PER_SPLIT, (k + 1) * ST_PER_SPLIT)
        a_re = jnp.broadcast_to(are_ref[:, lanes], (SUBLANES, ST_PER_SPLIT))
        a_im = jnp.broadcast_to(aim_ref[:, lanes], (SUBLANES, ST_PER_SPLIT))
        h_re, h_im = sre_ref[:, lanes], sim_ref[:, lanes]
        for i in range(tc // 2):
            r0 = i * 2 * SUBLANES
            res, ims = [], []
            for s in range(2):
                rs = slice(r0 + s * SUBLANES, r0 + (s + 1) * SUBLANES)
                n_re = a_re * h_re - a_im * h_im + bu[rs, :ST_PER_SPLIT]
                n_im = a_re * h_im + a_im * h_re + bu[rs, ST_PER_SPLIT:]
                h_re, h_im = n_re, n_im
                res.append(h_re)
                ims.append(h_im)
            pair = slice(r0, r0 + 2 * SUBLANES)
            h_ref[k, pair, :ST_PER_SPLIT] = jnp.concatenate(res, axis=0).astype(BF16)
            h_ref[k, pair, ST_PER_SPLIT:] = jnp.concatenate(ims, axis=0).astype(BF16)
        sre_ref[:, lanes] = h_re
        sim_ref[:, lanes] = h_im
        half = rows // 2
        y_parts.append(jnp.concatenate(
            [_dot(h_ref[k, :half, :], cbd_ref[k]), _dot(h_ref[k, half:, :], cbd_ref[k])], axis=0))
    y_s5 = _s5_post(jnp.concatenate(y_parts, axis=1), u, dskip_ref, wglu_ref)

    halo = POOL_BUF * SUBLANES
    vfull = jnp.concatenate([pool_ref[...], v], axis=0)
    pool_ref[...] = vfull[rows:, :]
    t_idx = step * tc + (lax.broadcasted_iota(jnp.int32, (rows, POOL_CH), 0) >> 3)
    pooled = []
    for gi, w in enumerate(POOL_WINDOWS):
        s = vfull[:, gi * POOL_CH:(gi + 1) * POOL_CH]
        span = 1
        while span < w:
            sh = span * SUBLANES
            s = s[sh:, :] + s[:-sh, :]
            span *= 2
        first = (POOL_BUF - (w - 1)) * SUBLANES
        wsum = s[first:first + rows, :]
        cnt = jnp.minimum(t_idx + 1, w).astype(F32)
        pooled.append(wsum / cnt - v[:, gi * POOL_CH:(gi + 1) * POOL_CH])
    y_pool = _pool_project(pooled, wpool_ref, pscale_ref)

    x1 = x + _dot(y_s5.astype(BF16), wout_ref[:D_S5, :]) + _dot(y_pool.astype(BF16), wout_ref[D_S5:, :])

    h2b = _rms(x1, g2_ref[...]).astype(BF16)
    taps = (CONV_W - 1) * SUBLANES
    for j in range(N_FF_CHUNKS):
        convd = []
        for base in (0, D_FF):
            cols = slice(base + j * FF_CHUNK, base + (j + 1) * FF_CHUNK)
            hup = _dot(h2b, wup_ref[:, cols])
            full = jnp.concatenate([conv_ref[:, cols], hup], axis=0)
            conv_ref[:, cols] = hup[rows - taps:, :]
            c = cb_ref[:, cols]
            for kk in range(CONV_W):
                c = c + cw_ref[kk:kk + 1, cols] * full[kk * SUBLANES:kk * SUBLANES + rows, :]
            convd.append(c)
        act_ref[:, j * FF_CHUNK:(j + 1) * FF_CHUNK] = (_gelu(convd[0]) * convd[1]).astype(BF16)
    y = _rms(x1 + _dot(act_ref[...], wdown_ref[...]), gf_ref[...])

    ybuf[slot] = y.reshape(tc, SUBLANES, D_MODEL)
    for cp in _block_copies(y_hbm, ybuf, sem_out, step, slot, tc, True):
        cp.start()

    @pl.when(step == n_steps - 1)
    def _():
        if n_steps >= 2:
            for cp in _block_copies(y_hbm, ybuf, sem_out, step - 1, 1 - slot, tc, True):
                cp.wait()
        for cp in _block_copies(y_hbm, ybuf, sem_out, step, slot, tc, True):
            cp.wait()


N_BIG_WEIGHTS = 6
STAGE_SHAPES = ((5, 256, D_MODEL), (2, 256, D_S5), (2, 256, POOL_CH), (5, D_MODEL, FF_CHUNK))
STAGE_LOOKAHEAD = 4


def _sample_kernel(x_hbm, sre_in, sim_in, pool_hbm, conv_hbm,
                   g1_ref, are_ref, aim_ref, bbd_ref, cbd_ref, dskip_ref, pscale_ref, g2_ref,
                   cw_ref, cb_ref, gf_ref,
                   win_hbm, wglu_hbm, wpool_hbm, wout_hbm, wup_hbm, wdown_hbm,
                   y_hbm, sre_ref, sim_ref, pool_out, conv_out,
                   winb_out, wglub_out, wpoolb_out, woutb_out, wupb_out, wdownb_out,
                   win_ref, wglu_ref, wpool_ref, wout_ref, wup_ref, wdown_ref,
                   stage_model, stage_s5, stage_pool, stage_col,
                   xbuf, ybuf, poolbuf, convbuf, hupbuf, vbuf,
                   sem_stage, sem_state, sem_wout, sem_sout):
    def state_row(hbm_ref, r):
        return hbm_ref.at[:, pl.ds(r, 1), :]

    in_copies = [pltpu.make_async_copy(x_hbm, xbuf, sem_state.at[0]),
                 pltpu.make_async_copy(pool_hbm, poolbuf, sem_state.at[1])]
    in_copies += [pltpu.make_async_copy(state_row(conv_hbm, r), convbuf.at[r], sem_state.at[2])
                  for r in range(CONV_W - 1)]
    for cp in in_copies:
        cp.start()

    stages = (stage_model, stage_s5, stage_pool, stage_col)
    sem_base = [sum(s.shape[0] for s in stages[:i]) for i in range(len(stages))]
    ring_pos = [0] * len(stages)
    queue = []
    out_copies = []
    live = {}

    def enqueue(src, sid, dst, after=None):
        queue.append((src, sid, ring_pos[sid] % stages[sid].shape[0], dst, after))
        ring_pos[sid] += 1

    def start_out(src, dst, sem):
        cp = pltpu.make_async_copy(src, dst, sem)
        cp.start()
        out_copies.append(cp)

    def enqueue_rows(w_hbm, wb_ref, sid, after_last):
        rows = stages[sid].shape[1]
        n_chunks = w_hbm.shape[0] // rows
        for c in range(n_chunks):
            part = pl.ds(c * rows, rows)
            enqueue(w_hbm.at[part, :], sid, wb_ref.at[part, :], after_last if c == n_chunks - 1 else None)

    def mixer():
        for cp in in_copies:
            cp.wait()
        start_out(wout_ref, woutb_out, sem_wout.at[3])
        x = xbuf[:, 0, :]
        hb = _rms(x, g1_ref[...]).astype(BF16)
        proj = _dot(hb, win_ref[...])
        u = proj[:, :D_S5]
        v = proj[:, D_S5:]
        ub = u.astype(BF16)

        y_parts = []
        for k in range(S5_SPLIT):
            bu = _dot(ub[:, k * U_PER_SPLIT:(k + 1) * U_PER_SPLIT], bbd_ref[k])
            lanes = slice(k * ST_PER_SPLIT, (k + 1) * ST_PER_SPLIT)
            a_re = are_ref[:, lanes]
            a_im = aim_ref[:, lanes]
            h_re0 = sre_in[lanes, :].T
            h_im0 = sim_in[lanes, :].T
            h_re = a_re * h_re0 - a_im * h_im0 + bu[:, :ST_PER_SPLIT]
            h_im = a_re * h_im0 + a_im * h_re0 + bu[:, ST_PER_SPLIT:]
            sre_ref[lanes, :] = h_re.T
            sim_ref[lanes, :] = h_im.T
            hcat = jnp.concatenate([h_re, h_im], axis=1).astype(BF16)
            y_parts.append(_dot(hcat, cbd_ref[k]))
        y_s5 = _s5_post(jnp.concatenate(y_parts, axis=1), u, dskip_ref, wglu_ref)

        pooled = []
        for gi, w in enumerate(POOL_WINDOWS):
            lanes = slice(gi * POOL_CH, (gi + 1) * POOL_CH)
            vc = v[:, lanes]
            wsum = vc
            for back in range(1, w):
                wsum = wsum + poolbuf[POOL_BUF - back, :, lanes]
            pooled.append(wsum / float(w) - vc)
        y_pool = _pool_project(pooled, wpool_ref, pscale_ref)
        vbuf[...] = v
        start_out(poolbuf.at[pl.ds(1, POOL_BUF - 1)], pool_out.at[pl.ds(0, POOL_BUF - 1)], sem_sout.at[0])
        start_out(vbuf, pool_out.at[POOL_BUF - 1], sem_sout.at[1])
        start_out(convbuf.at[1], state_row(conv_out, 0), sem_sout.at[2])

        x1 = (x + _dot(y_s5.astype(BF16), wout_ref[:D_S5, :])
              + _dot(y_pool.astype(BF16), wout_ref[D_S5:, :]))
        live["x1"] = x1
        live["h2b"] = _rms(x1, g2_ref[...]).astype(BF16)
        live["acc"] = None

    def ffn_chunk(j):
        rows_j = pl.ds(j * FF_CHUNK, FF_CHUNK)
        convd = []
        for base in (0, D_FF):
            cols = slice(base + j * FF_CHUNK, base + (j + 1) * FF_CHUNK)
            start_out(wup_ref.at[:, cols], wupb_out.at[:, cols], sem_wout.at[4])
            hup = _dot(live["h2b"], wup_ref[:, cols])
            hupbuf[:, 0, cols] = hup
            convd.append(cb_ref[:, cols] + cw_ref[0:1, cols] * convbuf[0, :, 0, cols]
                         + cw_ref[1:2, cols] * convbuf[1, :, 0, cols] + cw_ref[2:3, cols] * hup)
        start_out(wdown_ref.at[rows_j, :], wdownb_out.at[rows_j, :], sem_wout.at[5])
        act = (_gelu(convd[0]) * convd[1]).astype(BF16)
        part = _dot(act, wdown_ref[j * FF_CHUNK:(j + 1) * FF_CHUNK, :])
        live["acc"] = part if live["acc"] is None else live["acc"] + part

    enqueue_rows(win_hbm, win_ref, 0, lambda: start_out(win_ref, winb_out, sem_wout.at[0]))
    enqueue_rows(wglu_hbm, wglu_ref, 1, lambda: start_out(wglu_ref, wglub_out, sem_wout.at[1]))
    enqueue_rows(wpool_hbm, wpool_ref, 2, lambda: start_out(wpool_ref, wpoolb_out, sem_wout.at[2]))
    enqueue_rows(wout_hbm, wout_ref, 0, mixer)
    for j in range(N_FF_CHUNKS):
        for base in (0, D_FF):
            cols = pl.ds(base + j * FF_CHUNK, FF_CHUNK)
            enqueue(wup_hbm.at[:, cols], 3, wup_ref.at[:, cols])
        rows_j = pl.ds(j * FF_CHUNK, FF_CHUNK)
        enqueue(wdown_hbm.at[rows_j, :], 0, wdown_ref.at[rows_j, :], functools.partial(ffn_chunk, j))

    def read(entry):
        src, sid, slot, _, _ = entry
        return pltpu.make_async_copy(src, stages[sid].at[slot], sem_stage.at[sem_base[sid] + slot])

    started = 0
    for i, entry in enumerate(queue):
        while started < min(len(queue), i + 1 + STAGE_LOOKAHEAD):
            read(queue[started]).start()
            started += 1
        read(entry).wait()
        _, sid, slot, dst, after = entry
        dst[...] = stages[sid][slot].astype(BF16)
        if after is not None:
            after()

    ybuf[:, 0, :] = _rms(live["x1"] + live["acc"], gf_ref[...])
    start_out(hupbuf, state_row(conv_out, 1), sem_sout.at[3])
    start_out(ybuf, y_hbm, sem_sout.at[4])
    for cp in out_copies:
        cp.wait()


def _s5_tables(a_re, a_im, log_dt, b_re, b_im, c_re, c_im):
    dt = jnp.exp(log_dt)[:, None]
    mag = jnp.exp(dt * a_re)
    abar_re = mag * jnp.cos(dt * a_im)
    abar_im = mag * jnp.sin(dt * a_im)
    nr, ni = abar_re - 1.0, abar_im
    den = a_re * a_re + a_im * a_im
    f_re = ((nr * a_re + ni * a_im) / den)[:, :, None]
    f_im = ((ni * a_re - nr * a_im) / den)[:, :, None]
    bbar_re = f_re * b_re - f_im * b_im
    bbar_im = f_re * b_im + f_im * b_re
    u_group = lax.broadcasted_iota(jnp.int32, (U_PER_SPLIT, ST_PER_SPLIT), 0) // S5_CH
    st_group = lax.broadcasted_iota(jnp.int32, (U_PER_SPLIT, ST_PER_SPLIT), 1) // S5_STATE
    diag = u_group == st_group

    def in_table(b):
        b = b.reshape(S5_SPLIT, GROUPS_PER_SPLIT, S5_STATE, S5_CH)
        row = jnp.transpose(b, (0, 3, 1, 2)).reshape(S5_SPLIT, 1, S5_CH, ST_PER_SPLIT)
        rep = jnp.broadcast_to(row, (S5_SPLIT, GROUPS_PER_SPLIT, S5_CH, ST_PER_SPLIT))
        return jnp.where(diag, rep.reshape(S5_SPLIT, U_PER_SPLIT, ST_PER_SPLIT), 0.0)

    def out_table(c):
        c = c.reshape(S5_SPLIT, GROUPS_PER_SPLIT, S5_CH, S5_STATE)
        col = jnp.transpose(c, (0, 1, 3, 2)).reshape(S5_SPLIT, ST_PER_SPLIT, S5_CH)
        rep = jnp.tile(col, (1, 1, GROUPS_PER_SPLIT))
        return jnp.where(diag.T, rep, 0.0)

    bbd = jnp.concatenate([in_table(bbar_re), in_table(bbar_im)], axis=2)
    cbd = jnp.concatenate([out_table(c_re), out_table(-c_im)], axis=1)
    return abar_re.reshape(1, -1), abar_im.reshape(1, -1), bbd.astype(BF16), cbd.astype(BF16)


def _vmem_spec():
    return pl.BlockSpec(memory_space=pltpu.VMEM)


PROMPT_TC = 64
VMEM_LIMIT_BYTES = 60 * 1024 * 1024


def kernel(x_prompt, x_sample, state_s5_re, state_s5_im, state_pool, state_ffn_conv, norm_mix_g, w_in, s5_a_re, s5_a_im, s5_log_dt, s5_b_re, s5_b_im, s5_c_re, s5_c_im, s5_d, s5_w_glu, pool_w, pool_scale, w_out, norm_ffn_g, ffn_w_up, ffn_conv_w, ffn_conv_b, ffn_w_down, norm_final_g):
    nb, seq, _ = x_prompt.shape
    ns = x_sample.shape[0]
    assert nb == SUBLANES and seq % PROMPT_TC == 0 and x_sample.shape[1] == 1
    assert norm_mix_g.shape[0] == 1, "single layer"

    a_re, a_im, bbd, cbd = _s5_tables(s5_a_re[0], s5_a_im[0], s5_log_dt[0], s5_b_re[0],
                                      s5_b_im[0], s5_c_re[0], s5_c_im[0])
    n_states = S5_GROUPS * S5_STATE
    cparams = dict(vmem_limit_bytes=VMEM_LIMIT_BYTES)
    g1 = norm_mix_g[0].reshape(1, D_MODEL)
    dskip = s5_d[0].reshape(1, D_S5)
    pscale = pool_scale[0].reshape(1, D_POOL)
    g2 = norm_ffn_g[0].reshape(1, D_MODEL)
    cw = ffn_conv_w[0]
    cb = ffn_conv_b[0].reshape(1, 2 * D_FF)
    gf = norm_final_g.reshape(1, D_MODEL)

    big_f32 = (w_in[0], s5_w_glu[0], pool_w[0].reshape(len(POOL_WINDOWS) * POOL_CH, POOL_CH),
               w_out[0], ffn_w_up[0], ffn_w_down[0])
    small = (g1, a_re, a_im, bbd, cbd, dskip, pscale, g2, cw, cb, gf)
    any_spec = pl.BlockSpec(memory_space=pl.ANY)
    pool_in = jnp.transpose(state_pool[0], (1, 0, 2))

    def state_major(s):
        return jnp.transpose(s, (1, 2, 0)).reshape(n_states, ns)

    def sequence_major(s):
        return jnp.transpose(s.reshape(S5_GROUPS, S5_STATE, ns), (2, 0, 1))[None]
    sample_out = pl.pallas_call(
        _sample_kernel,
        in_specs=[any_spec] + [_vmem_spec()] * 2 + [any_spec] * 2 + [_vmem_spec()] * len(small)
                 + [any_spec] * N_BIG_WEIGHTS,
        out_specs=[any_spec] + [_vmem_spec()] * 2 + [any_spec] * (2 + N_BIG_WEIGHTS),
        out_shape=[
            jax.ShapeDtypeStruct((ns, 1, D_MODEL), F32),
            jax.ShapeDtypeStruct((n_states, ns), F32),
            jax.ShapeDtypeStruct((n_states, ns), F32),
            jax.ShapeDtypeStruct((POOL_BUF, ns, D_POOL), F32),
            jax.ShapeDtypeStruct((ns, CONV_W - 1, 2 * D_FF), F32),
        ] + [jax.ShapeDtypeStruct(w.shape, BF16) for w in big_f32],
        scratch_shapes=[pltpu.VMEM(w.shape, BF16) for w in big_f32]
        + [pltpu.VMEM(shape, F32) for shape in STAGE_SHAPES] + [
            pltpu.VMEM((ns, 1, D_MODEL), F32),
            pltpu.VMEM((ns, 1, D_MODEL), F32),
            pltpu.VMEM((POOL_BUF, ns, D_POOL), F32),
            pltpu.VMEM((CONV_W - 1, ns, 1, 2 * D_FF), F32),
            pltpu.VMEM((ns, 1, 2 * D_FF), F32),
            pltpu.VMEM((ns, D_POOL), F32),
            pltpu.SemaphoreType.DMA((sum(shape[0] for shape in STAGE_SHAPES),)),
            pltpu.SemaphoreType.DMA((3,)),
            pltpu.SemaphoreType.DMA((N_BIG_WEIGHTS,)),
            pltpu.SemaphoreType.DMA((5,)),
        ],
        compiler_params=pltpu.CompilerParams(**cparams),
        name="sample_layer",
    )(x_sample, state_major(state_s5_re[0]), state_major(state_s5_im[0]),
      pool_in, state_ffn_conv[0], *small, *big_f32)
    ys, s_re, s_im, s_pool, s_conv = sample_out[:5]
    win_b, wglu_b, wpool_b, wout_b, wup_b, wdown_b = sample_out[5:]
    weights = (g1, win_b, a_re, a_im, bbd, cbd, dskip, wglu_b, wpool_b, pscale, wout_b, g2,
               wup_b, cw, cb, wdown_b, gf)

    tc = PROMPT_TC
    rows = tc * SUBLANES
    const = lambda i: (0, 0)
    y_prompt, p_re, p_im, p_pool, p_conv = pl.pallas_call(
        functools.partial(_prompt_kernel, tc, seq // tc),
        grid=(seq // tc,),
        in_specs=[pl.BlockSpec(memory_space=pl.ANY)] + [_vmem_spec()] * len(weights),
        out_specs=[
            pl.BlockSpec(memory_space=pl.ANY),
            pl.BlockSpec((SUBLANES, n_states), const),
            pl.BlockSpec((SUBLANES, n_states), const),
            pl.BlockSpec((POOL_BUF * SUBLANES, D_POOL), const),
            pl.BlockSpec(((CONV_W - 1) * SUBLANES, 2 * D_FF), const),
        ],
        out_shape=[
            jax.ShapeDtypeStruct((nb, seq, D_MODEL), F32),
            jax.ShapeDtypeStruct((SUBLANES, n_states), F32),
            jax.ShapeDtypeStruct((SUBLANES, n_states), F32),
            jax.ShapeDtypeStruct((POOL_BUF * SUBLANES, D_POOL), F32),
            jax.ShapeDtypeStruct(((CONV_W - 1) * SUBLANES, 2 * D_FF), F32),
        ],
        scratch_shapes=[
            pltpu.VMEM((S5_SPLIT, rows, 2 * ST_PER_SPLIT), BF16),
            pltpu.VMEM((rows, D_FF), BF16),
            pltpu.VMEM((2, tc, SUBLANES, D_MODEL), F32),
            pltpu.VMEM((2, tc, SUBLANES, D_MODEL), F32),
            pltpu.SemaphoreType.DMA((2,)),
            pltpu.SemaphoreType.DMA((2,)),
        ],
        compiler_params=pltpu.CompilerParams(dimension_semantics=("arbitrary",), **cparams),
        name="prompt_layer",
    )(x_prompt, *weights)
    new_pool_p = jnp.transpose(p_pool.reshape(POOL_BUF, nb, D_POOL), (1, 0, 2))[None]
    new_conv_p = jnp.transpose(p_conv.reshape(CONV_W - 1, nb, 2 * D_FF), (1, 0, 2))[None]
    new_re_p = p_re.reshape(1, nb, S5_GROUPS, S5_STATE)
    new_im_p = p_im.reshape(1, nb, S5_GROUPS, S5_STATE)

    return (y_prompt, ys, new_re_p, new_im_p, new_pool_p, new_conv_p,
            sequence_major(s_re), sequence_major(s_im),
            jnp.transpose(s_pool, (1, 0, 2))[None], s_conv[None])
```

```python
import functools
import math

import numpy as np
import jax
import jax.numpy as jnp
from jax import lax
from jax.experimental import pallas as pl
from jax.experimental.pallas import tpu as pltpu

D_MODEL = 1024
D_S5 = 512
S5_CH = 16
S5_GROUPS = 32
S5_STATE = 64
D_POOL = 512
POOL_WINDOWS = (2, 4, 8, 16)
POOL_CH = 128
POOL_BUF = 15
D_FF = 2816
CONV_W = 3
EPS = 1e-6

SUBLANES = 8
S5_SPLIT = 2
GROUPS_PER_SPLIT = S5_GROUPS // S5_SPLIT
U_PER_SPLIT = GROUPS_PER_SPLIT * S5_CH
ST_PER_SPLIT = GROUPS_PER_SPLIT * S5_STATE
FF_CHUNK = 256
N_FF_CHUNKS = D_FF // FF_CHUNK
SQRT_HALF = float(np.sqrt(0.5).astype(np.float32))

BF16 = jnp.bfloat16
F32 = jnp.float32


def _dot(a, b):
    return jnp.dot(a, b, preferred_element_type=F32)


def _rms(x, g):
    ms = jnp.mean(x * x, axis=-1, keepdims=True)
    return x * lax.rsqrt(ms + EPS) * g


def _gelu(x):
    return 0.5 * x * (1.0 + lax.erf(x * SQRT_HALF))


def _s5_post(y_lin, u, dskip_ref, wglu_ref):
    y = _gelu(y_lin + dskip_ref[...] * u)
    return y * jax.nn.sigmoid(_dot(y.astype(BF16), wglu_ref[...]))


def _pool_project(pooled_cols, wpool_ref, pscale_ref):
    zero = jnp.zeros((POOL_CH, POOL_CH), BF16)
    outs = []
    for pair in range(len(POOL_WINDOWS) // 2):
        w_a = wpool_ref[(2 * pair) * POOL_CH:(2 * pair + 1) * POOL_CH, :]
        w_b = wpool_ref[(2 * pair + 1) * POOL_CH:(2 * pair + 2) * POOL_CH, :]
        w_pair = jnp.concatenate([jnp.concatenate([w_a, zero], axis=1),
                                  jnp.concatenate([zero, w_b], axis=1)], axis=0)
        lhs = jnp.concatenate([pooled_cols[2 * pair], pooled_cols[2 * pair + 1]], axis=1)
        outs.append(_dot(lhs.astype(BF16), w_pair))
    return jnp.concatenate(outs, axis=1) * pscale_ref[...]


def _block_copies(hbm_ref, buf_ref, sem_ref, block, slot, tc, to_hbm):
    copies = []
    for n in range(SUBLANES):
        hbm = hbm_ref.at[n, pl.ds(block * tc, tc), :]
        vmem = buf_ref.at[slot, :, n, :]
        src, dst = (vmem, hbm) if to_hbm else (hbm, vmem)
        copies.append(pltpu.make_async_copy(src, dst, sem_ref.at[slot]))
    return copies


def _prompt_kernel(tc, n_steps,
                   x_hbm, g1_ref, win_ref, are_ref, aim_ref, bbd_ref, cbd_ref, dskip_ref,
                   wglu_ref, wpool_ref, pscale_ref, wout_ref, g2_ref, wup_ref, cw_ref,
                   cb_ref, wdown_ref, gf_ref,
                   y_hbm, sre_ref, sim_ref, pool_ref, conv_ref,
                   h_ref, act_ref, xbuf, ybuf, sem_in, sem_out):
    rows = tc * SUBLANES
    step = pl.program_id(0)
    slot = step % 2

    @pl.when(step == 0)
    def _():
        sre_ref[...] = jnp.zeros_like(sre_ref)
        sim_ref[...] = jnp.zeros_like(sim_ref)
        pool_ref[...] = jnp.zeros_like(pool_ref)
        conv_ref[...] = jnp.zeros_like(conv_ref)
        for cp in _block_copies(x_hbm, xbuf, sem_in, 0, 0, tc, False):
            cp.start()

    @pl.when(step + 1 < n_steps)
    def _():
        for cp in _block_copies(x_hbm, xbuf, sem_in, step + 1, 1 - slot, tc, False):
            cp.start()

    @pl.when(step >= 2)
    def _():
        for cp in _block_copies(y_hbm, ybuf, sem_out, step - 2, slot, tc, True):
            cp.wait()

    for cp in _block_copies(x_hbm, xbuf, sem_in, step, slot, tc, False):
        cp.wait()

    x = xbuf[slot].reshape(rows, D_MODEL)
    hb = _rms(x, g1_ref[...]).astype(BF16)
    proj = _dot(hb, win_ref[...])
    u = proj[:, :D_S5]
    v = proj[:, D_S5:]
    ub = u.astype(BF16)

    bus = [_dot(ub[:, k * U_PER_SPLIT:(k + 1) * U_PER_SPLIT], bbd_ref[k]) for k in range(S5_SPLIT)]
    y_parts = []
    for k in range(S5_SPLIT):
        bu = bus[k]
        lanes = slice(k * ST_PER_SPLIT, (k + 1) * ST_PER_SPLIT)
        a_re = jnp.broadcast_to(are_ref[:, lanes], (SUBLANES, ST_PER_SPLIT))
        a_im = jnp.broadcast_to(aim_ref[:, lanes], (SUBLANES, ST_PER_SPLIT))
        h_re, h_im = sre_ref[:, lanes], sim_ref[:, lanes]
        for i in range(tc // 2):
            r0 = i * 2 * SUBLANES
            res, ims = [], []
            for s in range(2):
                rs = slice(r0 + s * SUBLANES, r0 + (s + 1) * SUBLANES)
                n_re = a_re * h_re - a_im * h_im + bu[rs, :ST_PER_SPLIT]
                n_im = a_re * h_im + a_im * h_re + bu[rs, ST_PER_SPLIT:]
                h_re, h_im = n_re, n_im
                res.append(h_re)
                ims.append(h_im)
            pair = slice(r0, r0 + 2 * SUBLANES)
            h_ref[k, pair, :ST_PER_SPLIT] = jnp.concatenate(res, axis=0).astype(BF16)
            h_ref[k, pair, ST_PER_SPLIT:] = jnp.concatenate(ims, axis=0).astype(BF16)
        sre_ref[:, lanes] = h_re
        sim_ref[:, lanes] = h_im
        half = rows // 2
        y_parts.append(jnp.concatenate(
            [_dot(h_ref[k, :half, :], cbd_ref[k]), _dot(h_ref[k, half:, :], cbd_ref[k])], axis=0))
    y_s5 = _s5_post(jnp.concatenate(y_parts, axis=1), u, dskip_ref, wglu_ref)

    halo = POOL_BUF * SUBLANES
    vfull = jnp.concatenate([pool_ref[...], v], axis=0)
    pool_ref[...] = vfull[rows:, :]
    t_idx = step * tc + (lax.broadcasted_iota(jnp.int32, (rows, POOL_CH), 0) >> 3)
    pooled = []
    for gi, w in enumerate(POOL_WINDOWS):
        s = vfull[:, gi * POOL_CH:(gi + 1) * POOL_CH]
        span = 1
        while span < w:
            sh = span * SUBLANES
            s = s[sh:, :] + s[:-sh, :]
            span *= 2
        first = (POOL_BUF - (w - 1)) * SUBLANES
        wsum = s[first:first + rows, :]
        cnt = jnp.minimum(t_idx + 1, w).astype(F32)
        pooled.append(wsum / cnt - v[:, gi * POOL_CH:(gi + 1) * POOL_CH])
    y_pool = _pool_project(pooled, wpool_ref, pscale_ref)

    x1 = x + _dot(y_s5.astype(BF16), wout_ref[:D_S5, :]) + _dot(y_pool.astype(BF16), wout_ref[D_S5:, :])

    h2b = _rms(x1, g2_ref[...]).astype(BF16)
    taps = (CONV_W - 1) * SUBLANES
    for j in range(N_FF_CHUNKS):
        convd = []
        for base in (0, D_FF):
            cols = slice(base + j * FF_CHUNK, base + (j + 1) * FF_CHUNK)
            hup = _dot(h2b, wup_ref[:, cols])
            full = jnp.concatenate([conv_ref[:, cols], hup], axis=0)
            conv_ref[:, cols] = hup[rows - taps:, :]
            c = cb_ref[:, cols]
            for kk in range(CONV_W):
                c = c + cw_ref[kk:kk + 1, cols] * full[kk * SUBLANES:kk * SUBLANES + rows, :]
            convd.append(c)
        act_ref[:, j * FF_CHUNK:(j + 1) * FF_CHUNK] = (_gelu(convd[0]) * convd[1]).astype(BF16)
    y = _rms(x1 + _dot(act_ref[...], wdown_ref[...]), gf_ref[...])

    ybuf[slot] = y.reshape(tc, SUBLANES, D_MODEL)
    for cp in _block_copies(y_hbm, ybuf, sem_out, step, slot, tc, True):
        cp.start()

    @pl.when(step == n_steps - 1)
    def _():
        if n_steps >= 2:
            for cp in _block_copies(y_hbm, ybuf, sem_out, step - 1, 1 - slot, tc, True):
                cp.wait()
        for cp in _block_copies(y_hbm, ybuf, sem_out, step, slot, tc, True):
            cp.wait()


N_BIG_WEIGHTS = 6
STAGE_SHAPES = ((6, 256, D_MODEL), (2, 256, D_S5), (2, 256, POOL_CH), (6, D_MODEL, FF_CHUNK))
STAGE_LOOKAHEAD = 5


def _sample_kernel(x_hbm, sre_in, sim_in, pool_hbm, conv_hbm,
                   g1_ref, are_ref, aim_ref, bbd_ref, cbd_ref, dskip_ref, pscale_ref, g2_ref,
                   cw_ref, cb_ref, gf_ref,
                   win_hbm, wglu_hbm, wpool_hbm, wout_hbm, wup_hbm, wdown_hbm,
                   y_hbm, sre_ref, sim_ref, pool_out, conv_out,
                   winb_out, wglub_out, wpoolb_out, woutb_out, wupb_out, wdownb_out,
                   win_ref, wglu_ref, wpool_ref, wout_ref, wup_ref, wdown_ref,
                   stage_model, stage_s5, stage_pool, stage_col,
                   xbuf, ybuf, poolbuf, convbuf, vbuf,
                   sem_stage, sem_state, sem_wout, sem_sout):
    in_copies = [pltpu.make_async_copy(x_hbm, xbuf, sem_state.at[0]),
                 pltpu.make_async_copy(pool_hbm, poolbuf, sem_state.at[1]),
                 pltpu.make_async_copy(conv_hbm, convbuf, sem_state.at[2])]
    for cp in in_copies:
        cp.start()

    stages = (stage_model, stage_s5, stage_pool, stage_col)
    sem_base = [sum(s.shape[0] for s in stages[:i]) for i in range(len(stages))]
    ring_pos = [0] * len(stages)
    queue = []
    out_copies = []
    live = {}

    def enqueue(src, sid, dst, after=None):
        queue.append((src, sid, ring_pos[sid] % stages[sid].shape[0], dst, after))
        ring_pos[sid] += 1

    def start_out(src, dst, sem):
        cp = pltpu.make_async_copy(src, dst, sem)
        cp.start()
        out_copies.append(cp)

    def enqueue_rows(w_hbm, wb_ref, sid, after_last):
        rows = stages[sid].shape[1]
        n_chunks = w_hbm.shape[0] // rows
        for c in range(n_chunks):
            part = pl.ds(c * rows, rows)
            enqueue(w_hbm.at[part, :], sid, wb_ref.at[part, :], after_last if c == n_chunks - 1 else None)

    def mixer():
        for cp in in_copies:
            cp.wait()
        start_out(wout_ref, woutb_out, sem_wout.at[3])
        x = xbuf[:, 0, :]
        hb = _rms(x, g1_ref[...]).astype(BF16)
        proj = _dot(hb, win_ref[...])
        u = proj[:, :D_S5]
        v = proj[:, D_S5:]
        ub = u.astype(BF16)

        y_parts = []
        for k in range(S5_SPLIT):
            bu = _dot(ub[:, k * U_PER_SPLIT:(k + 1) * U_PER_SPLIT], bbd_ref[k])
            lanes = slice(k * ST_PER_SPLIT, (k + 1) * ST_PER_SPLIT)
            a_re = are_ref[:, lanes]
            a_im = aim_ref[:, lanes]
            h_re0 = sre_in[lanes, :].T
            h_im0 = sim_in[lanes, :].T
            h_re = a_re * h_re0 - a_im * h_im0 + bu[:, :ST_PER_SPLIT]
            h_im = a_re * h_im0 + a_im * h_re0 + bu[:, ST_PER_SPLIT:]
            sre_ref[lanes, :] = h_re.T
            sim_ref[lanes, :] = h_im.T
            hcat = jnp.concatenate([h_re, h_im], axis=1).astype(BF16)
            y_parts.append(_dot(hcat, cbd_ref[k]))
        y_s5 = _s5_post(jnp.concatenate(y_parts, axis=1), u, dskip_ref, wglu_ref)

        pooled = []
        for gi, w in enumerate(POOL_WINDOWS):
            lanes = slice(gi * POOL_CH, (gi + 1) * POOL_CH)
            vc = v[:, lanes]
            wsum = vc
            for back in range(1, w):
                wsum = wsum + poolbuf[POOL_BUF - back, :, lanes]
            pooled.append(wsum / float(w) - vc)
        y_pool = _pool_project(pooled, wpool_ref, pscale_ref)
        vbuf[...] = v
        start_out(poolbuf.at[pl.ds(1, POOL_BUF - 1)], pool_out.at[pl.ds(0, POOL_BUF - 1)], sem_sout.at[0])
        start_out(vbuf, pool_out.at[POOL_BUF - 1], sem_sout.at[1])

        x1 = (x + _dot(y_s5.astype(BF16), wout_ref[:D_S5, :])
              + _dot(y_pool.astype(BF16), wout_ref[D_S5:, :]))
        live["x1"] = x1
        live["h2b"] = _rms(x1, g2_ref[...]).astype(BF16)
        live["acc"] = None

    def ffn_chunk(j):
        rows_j = pl.ds(j * FF_CHUNK, FF_CHUNK)
        convd = []
        for base in (0, D_FF):
            cols = slice(base + j * FF_CHUNK, base + (j + 1) * FF_CHUNK)
            start_out(wup_ref.at[:, cols], wupb_out.at[:, cols], sem_wout.at[4])
            hup = _dot(live["h2b"], wup_ref[:, cols])
            older = convbuf[:, 0, cols]
            newer = convbuf[:, 1, cols]
            convbuf[:, 0, cols] = newer
            convbuf[:, 1, cols] = hup
            convd.append(cb_ref[:, cols] + cw_ref[0:1, cols] * older
                         + cw_ref[1:2, cols] * newer + cw_ref[2:3, cols] * hup)
        start_out(wdown_ref.at[rows_j, :], wdownb_out.at[rows_j, :], sem_wout.at[5])
        act = (_gelu(convd[0]) * convd[1]).astype(BF16)
        part = _dot(act, wdown_ref[j * FF_CHUNK:(j + 1) * FF_CHUNK, :])
        live["acc"] = part if live["acc"] is None else live["acc"] + part

    enqueue_rows(win_hbm, win_ref, 0, lambda: start_out(win_ref, winb_out, sem_wout.at[0]))
    enqueue_rows(wglu_hbm, wglu_ref, 1, lambda: start_out(wglu_ref, wglub_out, sem_wout.at[1]))
    enqueue_rows(wpool_hbm, wpool_ref, 2, lambda: start_out(wpool_ref, wpoolb_out, sem_wout.at[2]))
    enqueue_rows(wout_hbm, wout_ref, 0, mixer)
    for j in range(N_FF_CHUNKS):
        for base in (0, D_FF):
            cols = pl.ds(base + j * FF_CHUNK, FF_CHUNK)
            enqueue(wup_hbm.at[:, cols], 3, wup_ref.at[:, cols])
        rows_j = pl.ds(j * FF_CHUNK, FF_CHUNK)
        enqueue(wdown_hbm.at[rows_j, :], 0, wdown_ref.at[rows_j, :], functools.partial(ffn_chunk, j))

    def read(entry):
        src, sid, slot, _, _ = entry
        return pltpu.make_async_copy(src, stages[sid].at[slot], sem_stage.at[sem_base[sid] + slot])

    started = 0
    for i, entry in enumerate(queue):
        while started < min(len(queue), i + 1 + STAGE_LOOKAHEAD):
            read(queue[started]).start()
            started += 1
        read(entry).wait()
        _, sid, slot, dst, after = entry
        dst[...] = stages[sid][slot].astype(BF16)
        if after is not None:
            after()

    ybuf[:, 0, :] = _rms(live["x1"] + live["acc"], gf_ref[...])
    start_out(convbuf, conv_out, sem_sout.at[2])
    start_out(ybuf, y_hbm, sem_sout.at[3])
    for cp in out_copies:
        cp.wait()


def _s5_tables(a_re, a_im, log_dt, b_re, b_im, c_re, c_im):
    dt = jnp.exp(log_dt)[:, None]
    mag = jnp.exp(dt * a_re)
    abar_re = mag * jnp.cos(dt * a_im)
    abar_im = mag * jnp.sin(dt * a_im)
    nr, ni = abar_re - 1.0, abar_im
    den = a_re * a_re + a_im * a_im
    f_re = ((nr * a_re + ni * a_im) / den)[:, :, None]
    f_im = ((ni * a_re - nr * a_im) / den)[:, :, None]
    bbar_re = f_re * b_re - f_im * b_im
    bbar_im = f_re * b_im + f_im * b_re
    u_group = lax.broadcasted_iota(jnp.int32, (U_PER_SPLIT, ST_PER_SPLIT), 0) // S5_CH
    st_group = lax.broadcasted_iota(jnp.int32, (U_PER_SPLIT, ST_PER_SPLIT), 1) // S5_STATE
    diag = u_group == st_group

    def in_table(b):
        b = b.reshape(S5_SPLIT, GROUPS_PER_SPLIT, S5_STATE, S5_CH)
        row = jnp.transpose(b, (0, 3, 1, 2)).reshape(S5_SPLIT, 1, S5_CH, ST_PER_SPLIT)
        rep = jnp.broadcast_to(row, (S5_SPLIT, GROUPS_PER_SPLIT, S5_CH, ST_PER_SPLIT))
        return jnp.where(diag, rep.reshape(S5_SPLIT, U_PER_SPLIT, ST_PER_SPLIT), 0.0)

    def out_table(c):
        c = c.reshape(S5_SPLIT, GROUPS_PER_SPLIT, S5_CH, S5_STATE)
        col = jnp.transpose(c, (0, 1, 3, 2)).reshape(S5_SPLIT, ST_PER_SPLIT, S5_CH)
        rep = jnp.tile(col, (1, 1, GROUPS_PER_SPLIT))
        return jnp.where(diag.T, rep, 0.0)

    bbd = jnp.concatenate([in_table(bbar_re), in_table(bbar_im)], axis=2)
    cbd = jnp.concatenate([out_table(c_re), out_table(-c_im)], axis=1)
    return abar_re.reshape(1, -1), abar_im.reshape(1, -1), bbd.astype(BF16), cbd.astype(BF16)


def _vmem_spec():
    return pl.BlockSpec(memory_space=pltpu.VMEM)


PROMPT_TC = 64
VMEM_LIMIT_BYTES = 60 * 1024 * 1024


def kernel(x_prompt, x_sample, state_s5_re, state_s5_im, state_pool, state_ffn_conv, norm_mix_g, w_in, s5_a_re, s5_a_im, s5_log_dt, s5_b_re, s5_b_im, s5_c_re, s5_c_im, s5_d, s5_w_glu, pool_w, pool_scale, w_out, norm_ffn_g, ffn_w_up, ffn_conv_w, ffn_conv_b, ffn_w_down, norm_final_g):
    nb, seq, _ = x_prompt.shape
    ns = x_sample.shape[0]
    assert nb == SUBLANES and seq % PROMPT_TC == 0 and x_sample.shape[1] == 1
    assert norm_mix_g.shape[0] == 1, "single layer"

    a_re, a_im, bbd, cbd = _s5_tables(s5_a_re[0], s5_a_im[0], s5_log_dt[0], s5_b_re[0],
                                      s5_b_im[0], s5_c_re[0], s5_c_im[0])
    n_states = S5_GROUPS * S5_STATE
    cparams = dict(vmem_limit_bytes=VMEM_LIMIT_BYTES)
    g1 = norm_mix_g[0].reshape(1, D_MODEL)
    dskip = s5_d[0].reshape(1, D_S5)
    pscale = pool_scale[0].reshape(1, D_POOL)
    g2 = norm_ffn_g[0].reshape(1, D_MODEL)
    cw = ffn_conv_w[0]
    cb = ffn_conv_b[0].reshape(1, 2 * D_FF)
    gf = norm_final_g.reshape(1, D_MODEL)

    big_f32 = (w_in[0], s5_w_glu[0], pool_w[0].reshape(len(POOL_WINDOWS) * POOL_CH, POOL_CH),
               w_out[0], ffn_w_up[0], ffn_w_down[0])
    small = (g1, a_re, a_im, bbd, cbd, dskip, pscale, g2, cw, cb, gf)
    any_spec = pl.BlockSpec(memory_space=pl.ANY)
    pool_in = jnp.transpose(state_pool[0], (1, 0, 2))

    def state_major(s):
        return jnp.transpose(s, (1, 2, 0)).reshape(n_states, ns)

    def sequence_major(s):
        return jnp.transpose(s.reshape(S5_GROUPS, S5_STATE, ns), (2, 0, 1))[None]
    sample_out = pl.pallas_call(
        _sample_kernel,
        in_specs=[any_spec] + [_vmem_spec()] * 2 + [any_spec] * 2 + [_vmem_spec()] * len(small)
                 + [any_spec] * N_BIG_WEIGHTS,
        out_specs=[any_spec] + [_vmem_spec()] * 2 + [any_spec] * (2 + N_BIG_WEIGHTS),
        out_shape=[
            jax.ShapeDtypeStruct((ns, 1, D_MODEL), F32),
            jax.ShapeDtypeStruct((n_states, ns), F32),
            jax.ShapeDtypeStruct((n_states, ns), F32),
            jax.ShapeDtypeStruct((POOL_BUF, ns, D_POOL), F32),
            jax.ShapeDtypeStruct((ns, CONV_W - 1, 2 * D_FF), F32),
        ] + [jax.ShapeDtypeStruct(w.shape, BF16) for w in big_f32],
        scratch_shapes=[pltpu.VMEM(w.shape, BF16) for w in big_f32]
        + [pltpu.VMEM(shape, F32) for shape in STAGE_SHAPES] + [
            pltpu.VMEM((ns, 1, D_MODEL), F32),
            pltpu.VMEM((ns, 1, D_MODEL), F32),
            pltpu.VMEM((POOL_BUF, ns, D_POOL), F32),
            pltpu.VMEM((ns, CONV_W - 1, 2 * D_FF), F32),
            pltpu.VMEM((ns, D_POOL), F32),
            pltpu.SemaphoreType.DMA((sum(shape[0] for shape in STAGE_SHAPES),)),
            pltpu.SemaphoreType.DMA((3,)),
            pltpu.SemaphoreType.DMA((N_BIG_WEIGHTS,)),
            pltpu.SemaphoreType.DMA((4,)),
        ],
        compiler_params=pltpu.CompilerParams(**cparams),
        name="sample_layer",
    )(x_sample, state_major(state_s5_re[0]), state_major(state_s5_im[0]),
      pool_in, state_ffn_conv[0], *small, *big_f32)
    ys, s_re, s_im, s_pool, s_conv = sample_out[:5]
    win_b, wglu_b, wpool_b, wout_b, wup_b, wdown_b = sample_out[5:]
    weights = (g1, win_b, a_re, a_im, bbd, cbd, dskip, wglu_b, wpool_b, pscale, wout_b, g2,
               wup_b, cw, cb, wdown_b, gf)

    tc = PROMPT_TC
    rows = tc * SUBLANES
    const = lambda i: (0, 0)
    y_prompt, p_re, p_im, p_pool, p_conv = pl.pallas_call(
        functools.partial(_prompt_kernel, tc, seq // tc),
        grid=(seq // tc,),
        in_specs=[pl.BlockSpec(memory_space=pl.ANY)] + [_vmem_spec()] * len(weights),
        out_specs=[
            pl.BlockSpec(memory_space=pl.ANY),
            pl.BlockSpec((SUBLANES, n_states), const),
            pl.BlockSpec((SUBLANES, n_states), const),
            pl.BlockSpec((POOL_BUF * SUBLANES, D_POOL), const),
            pl.BlockSpec(((CONV_W - 1) * SUBLANES, 2 * D_FF), const),
        ],
        out_shape=[
            jax.ShapeDtypeStruct((nb, seq, D_MODEL), F32),
            jax.ShapeDtypeStruct((SUBLANES, n_states), F32),
            jax.ShapeDtypeStruct((SUBLANES, n_states), F32),
            jax.ShapeDtypeStruct((POOL_BUF * SUBLANES, D_POOL), F32),
            jax.ShapeDtypeStruct(((CONV_W - 1) * SUBLANES, 2 * D_FF), F32),
        ],
        scratch_shapes=[
            pltpu.VMEM((S5_SPLIT, rows, 2 * ST_PER_SPLIT), BF16),
            pltpu.VMEM((rows, D_FF), BF16),
            pltpu.VMEM((2, tc, SUBLANES, D_MODEL), F32),
            pltpu.VMEM((2, tc, SUBLANES, D_MODEL), F32),
            pltpu.SemaphoreType.DMA((2,)),
            pltpu.SemaphoreType.DMA((2,)),
        ],
        compiler_params=pltpu.CompilerParams(dimension_semantics=("arbitrary",), **cparams),
        name="prompt_layer",
    )(x_prompt, *weights)
    new_pool_p = jnp.transpose(p_pool.reshape(POOL_BUF, nb, D_POOL), (1, 0, 2))[None]
    new_conv_p = jnp.transpose(p_conv.reshape(CONV_W - 1, nb, 2 * D_FF), (1, 0, 2))[None]
    new_re_p = p_re.reshape(1, nb, S5_GROUPS, S5_STATE)
    new_im_p = p_im.reshape(1, nb, S5_GROUPS, S5_STATE)

    return (y_prompt, ys, new_re_p, new_im_p, new_pool_p, new_conv_p,
            sequence_major(s_re), sequence_major(s_im),
            jnp.transpose(s_pool, (1, 0, 2))[None], s_conv[None])
```

```python
import functools

import numpy as np
import jax
import jax.numpy as jnp
from jax import lax
from jax.experimental import pallas as pl
from jax.experimental.pallas import tpu as pltpu

D_MODEL = 1024
D_S5 = 512
S5_CH = 16
S5_GROUPS = 32
S5_STATE = 64
D_POOL = 512
POOL_WINDOWS = (2, 4, 8, 16)
POOL_CH = 128
POOL_BUF = 15
D_FF = 2816
CONV_W = 3
EPS = 1e-6

SUBLANES = 8
S5_SPLIT = 2
GROUPS_PER_SPLIT = S5_GROUPS // S5_SPLIT
U_PER_SPLIT = GROUPS_PER_SPLIT * S5_CH
ST_PER_SPLIT = GROUPS_PER_SPLIT * S5_STATE
FF_CHUNK = 256
N_FF_CHUNKS = D_FF // FF_CHUNK
SQRT_HALF = float(np.sqrt(0.5).astype(np.float32))

BF16 = jnp.bfloat16
F32 = jnp.float32


def _dot(a, b):
    return jnp.dot(a, b, preferred_element_type=F32)


def _rms(x, g):
    ms = jnp.mean(x * x, axis=-1, keepdims=True)
    return x * lax.rsqrt(ms + EPS) * g


def _gelu(x):
    return 0.5 * x * (1.0 + lax.erf(x * SQRT_HALF))


def _s5_post(y_lin, u, dskip_ref, wglu_ref):
    y = _gelu(y_lin + dskip_ref[...] * u)
    return y * jax.nn.sigmoid(_dot(y.astype(BF16), wglu_ref[...]))


def _pool_project(pooled_cols, wpool_ref, pscale_ref):
    zero = jnp.zeros((POOL_CH, POOL_CH), BF16)
    outs = []
    for pair in range(len(POOL_WINDOWS) // 2):
        w_a = wpool_ref[(2 * pair) * POOL_CH:(2 * pair + 1) * POOL_CH, :]
        w_b = wpool_ref[(2 * pair + 1) * POOL_CH:(2 * pair + 2) * POOL_CH, :]
        w_pair = jnp.concatenate([jnp.concatenate([w_a, zero], axis=1),
                                  jnp.concatenate([zero, w_b], axis=1)], axis=0)
        lhs = jnp.concatenate([pooled_cols[2 * pair], pooled_cols[2 * pair + 1]], axis=1)
        outs.append(_dot(lhs.astype(BF16), w_pair))
    return jnp.concatenate(outs, axis=1) * pscale_ref[...]


def _block_copies(hbm_ref, buf_ref, sem_ref, block, slot, tc, to_hbm):
    copies = []
    for n in range(SUBLANES):
        hbm = hbm_ref.at[n, pl.ds(block * tc, tc), :]
        vmem = buf_ref.at[slot, :, n, :]
        src, dst = (vmem, hbm) if to_hbm else (hbm, vmem)
        copies.append(pltpu.make_async_copy(src, dst, sem_ref.at[slot]))
    return copies


def _prompt_kernel(tc, n_steps,
                   x_hbm, g1_ref, win_ref, are_ref, aim_ref, bbd_ref, cbd_ref, dskip_ref,
                   wglu_ref, wpool_ref, pscale_ref, wout_ref, g2_ref, wup_ref, cw_ref,
                   cb_ref, wdown_ref, gf_ref,
                   y_hbm, sre_ref, sim_ref, pool_ref, conv_ref,
                   h_ref, act_ref, xbuf, ybuf, sem_in, sem_out):
    rows = tc * SUBLANES
    step = pl.program_id(0)
    slot = step % 2

    @pl.when(step == 0)
    def _():
        sre_ref[...] = jnp.zeros_like(sre_ref)
        sim_ref[...] = jnp.zeros_like(sim_ref)
        pool_ref[...] = jnp.zeros_like(pool_ref)
        conv_ref[...] = jnp.zeros_like(conv_ref)
        for cp in _block_copies(x_hbm, xbuf, sem_in, 0, 0, tc, False):
            cp.start()

    @pl.when(step + 1 < n_steps)
    def _():
        for cp in _block_copies(x_hbm, xbuf, sem_in, step + 1, 1 - slot, tc, False):
            cp.start()

    @pl.when(step >= 2)
    def _():
        for cp in _block_copies(y_hbm, ybuf, sem_out, step - 2, slot, tc, True):
            cp.wait()

    for cp in _block_copies(x_hbm, xbuf, sem_in, step, slot, tc, False):
        cp.wait()

    x = xbuf[slot].reshape(rows, D_MODEL)
    hb = _rms(x, g1_ref[...]).astype(BF16)
    proj = _dot(hb, win_ref[...])
    u = proj[:, :D_S5]
    v = proj[:, D_S5:]
    ub = u.astype(BF16)

    bus = [_dot(ub[:, k * U_PER_SPLIT:(k + 1) * U_PER_SPLIT], bbd_ref[k]) for k in range(S5_SPLIT)]
    y_parts = []
    for k in range(S5_SPLIT):
        bu = bus[k]
        lanes = slice(k * ST_PER_SPLIT, (k + 1) * ST_PER_SPLIT)
        a_re = jnp.broadcast_to(are_ref[:, lanes], (SUBLANES, ST_PER_SPLIT))
        a_im = jnp.broadcast_to(aim_ref[:, lanes], (SUBLANES, ST_PER_SPLIT))
        h_re, h_im = sre_ref[:, lanes], sim_ref[:, lanes]
        for i in range(tc // 2):
            r0 = i * 2 * SUBLANES
            res, ims = [], []
            for s in range(2):
                rs = slice(r0 + s * SUBLANES, r0 + (s + 1) * SUBLANES)
                n_re = a_re * h_re - a_im * h_im + bu[rs, :ST_PER_SPLIT]
                n_im = a_re * h_im + a_im * h_re + bu[rs, ST_PER_SPLIT:]
                h_re, h_im = n_re, n_im
                res.append(h_re)
                ims.append(h_im)
            pair = slice(r0, r0 + 2 * SUBLANES)
            h_ref[k, pair, :ST_PER_SPLIT] = jnp.concatenate(res, axis=0).astype(BF16)
            h_ref[k, pair, ST_PER_SPLIT:] = jnp.concatenate(ims, axis=0).astype(BF16)
        sre_ref[:, lanes] = h_re
        sim_ref[:, lanes] = h_im
        half = rows // 2
        y_parts.append(jnp.concatenate(
            [_dot(h_ref[k, :half, :], cbd_ref[k]), _dot(h_ref[k, half:, :], cbd_ref[k])], axis=0))
    y_s5 = _s5_post(jnp.concatenate(y_parts, axis=1), u, dskip_ref, wglu_ref)

    vfull = jnp.concatenate([pool_ref[...], v], axis=0)
    pool_ref[...] = vfull[rows:, :]
    t_idx = step * tc + (lax.broadcasted_iota(jnp.int32, (rows, POOL_CH), 0) >> 3)
    pooled = []
    for gi, w in enumerate(POOL_WINDOWS):
        s = vfull[:, gi * POOL_CH:(gi + 1) * POOL_CH]
        span = 1
        while span < w:
            sh = span * SUBLANES
            s = s[sh:, :] + s[:-sh, :]
            span *= 2
        first = (POOL_BUF - (w - 1)) * SUBLANES
        wsum = s[first:first + rows, :]
        cnt = jnp.minimum(t_idx + 1, w).astype(F32)
        pooled.append(wsum / cnt - v[:, gi * POOL_CH:(gi + 1) * POOL_CH])
    y_pool = _pool_project(pooled, wpool_ref, pscale_ref)

    x1 = x + _dot(y_s5.astype(BF16), wout_ref[:D_S5, :]) + _dot(y_pool.astype(BF16), wout_ref[D_S5:, :])

    h2b = _rms(x1, g2_ref[...]).astype(BF16)
    taps = (CONV_W - 1) * SUBLANES
    for j in range(N_FF_CHUNKS):
        convd = []
        for base in (0, D_FF):
            cols = slice(base + j * FF_CHUNK, base + (j + 1) * FF_CHUNK)
            hup = _dot(h2b, wup_ref[:, cols])
            full = jnp.concatenate([conv_ref[:, cols], hup], axis=0)
            conv_ref[:, cols] = hup[rows - taps:, :]
            c = cb_ref[:, cols]
            for kk in range(CONV_W):
                c = c + cw_ref[kk:kk + 1, cols] * full[kk * SUBLANES:kk * SUBLANES + rows, :]
            convd.append(c)
        act_ref[:, j * FF_CHUNK:(j + 1) * FF_CHUNK] = (_gelu(convd[0]) * convd[1]).astype(BF16)
    y = _rms(x1 + _dot(act_ref[...], wdown_ref[...]), gf_ref[...])

    ybuf[slot] = y.reshape(tc, SUBLANES, D_MODEL)
    for cp in _block_copies(y_hbm, ybuf, sem_out, step, slot, tc, True):
        cp.start()

    @pl.when(step == n_steps - 1)
    def _():
        if n_steps >= 2:
            for cp in _block_copies(y_hbm, ybuf, sem_out, step - 1, 1 - slot, tc, True):
                cp.wait()
        for cp in _block_copies(y_hbm, ybuf, sem_out, step, slot, tc, True):
            cp.wait()


N_BIG_WEIGHTS = 6
STAGE_SHAPES = ((6, 256, D_MODEL), (2, 256, D_S5), (2, 256, POOL_CH), (6, D_MODEL, FF_CHUNK))
STAGE_LOOKAHEAD = 5


def _sample_kernel(x_hbm, sre_in, sim_in, pool_hbm, conv_hbm,
                   g1_ref, are_ref, aim_ref, bbd_ref, cbd_ref, dskip_ref, pscale_ref, g2_ref,
                   cw_ref, cb_ref, gf_ref,
                   win_hbm, wglu_hbm, wpool_hbm, wout_hbm, wup_hbm, wdown_hbm,
                   y_hbm, sre_ref, sim_ref, pool_out, conv_out,
                   winb_out, wglub_out, wpoolb_out, woutb_out, wupb_out, wdownb_out,
                   win_ref, wglu_ref, wpool_ref, wout_ref, wup_ref, wdown_ref,
                   stage_model, stage_s5, stage_pool, stage_col,
                   xbuf, ybuf, poolbuf, convbuf, vbuf,
                   sem_stage, sem_state, sem_wout, sem_sout):
    in_copies = [pltpu.make_async_copy(x_hbm, xbuf, sem_state.at[0]),
                 pltpu.make_async_copy(pool_hbm, poolbuf, sem_state.at[1]),
                 pltpu.make_async_copy(conv_hbm, convbuf, sem_state.at[2])]
    for cp in in_copies:
        cp.start()

    stages = (stage_model, stage_s5, stage_pool, stage_col)
    sem_base = [sum(s.shape[0] for s in stages[:i]) for i in range(len(stages))]
    ring_pos = [0] * len(stages)
    queue = []
    out_copies = []
    live = {}

    def enqueue(src, sid, dst, after=None):
        queue.append((src, sid, ring_pos[sid] % stages[sid].shape[0], dst, after))
        ring_pos[sid] += 1

    def start_out(src, dst, sem):
        cp = pltpu.make_async_copy(src, dst, sem)
        cp.start()
        out_copies.append(cp)

    def enqueue_rows(w_hbm, wb_ref, sid, after_last):
        rows = stages[sid].shape[1]
        n_chunks = w_hbm.shape[0] // rows
        for c in range(n_chunks):
            part = pl.ds(c * rows, rows)
            enqueue(w_hbm.at[part, :], sid, wb_ref.at[part, :], after_last if c == n_chunks - 1 else None)

    def mixer():
        for cp in in_copies:
            cp.wait()
        start_out(wout_ref, woutb_out, sem_wout.at[3])
        x = xbuf[:, 0, :]
        hb = _rms(x, g1_ref[...]).astype(BF16)
        proj = _dot(hb, win_ref[...])
        u = proj[:, :D_S5]
        v = proj[:, D_S5:]
        ub = u.astype(BF16)

        y_parts = []
        for k in range(S5_SPLIT):
            bu = _dot(ub[:, k * U_PER_SPLIT:(k + 1) * U_PER_SPLIT], bbd_ref[k])
            lanes = slice(k * ST_PER_SPLIT, (k + 1) * ST_PER_SPLIT)
            a_re = are_ref[:, lanes]
            a_im = aim_ref[:, lanes]
            h_re0 = sre_in[lanes, :].T
            h_im0 = sim_in[lanes, :].T
            h_re = a_re * h_re0 - a_im * h_im0 + bu[:, :ST_PER_SPLIT]
            h_im = a_re * h_im0 + a_im * h_re0 + bu[:, ST_PER_SPLIT:]
            sre_ref[lanes, :] = h_re.T
            sim_ref[lanes, :] = h_im.T
            hcat = jnp.concatenate([h_re, h_im], axis=1).astype(BF16)
            y_parts.append(_dot(hcat, cbd_ref[k]))
        y_s5 = _s5_post(jnp.concatenate(y_parts, axis=1), u, dskip_ref, wglu_ref)

        pooled = []
        for gi, w in enumerate(POOL_WINDOWS):
            lanes = slice(gi * POOL_CH, (gi + 1) * POOL_CH)
            vc = v[:, lanes]
            wsum = vc
            for back in range(1, w):
                wsum = wsum + poolbuf[POOL_BUF - back, :, lanes]
            pooled.append(wsum / float(w) - vc)
        y_pool = _pool_project(pooled, wpool_ref, pscale_ref)
        vbuf[...] = v
        start_out(poolbuf.at[pl.ds(1, POOL_BUF - 1)], pool_out.at[pl.ds(0, POOL_BUF - 1)], sem_sout.at[0])
        start_out(vbuf, pool_out.at[POOL_BUF - 1], sem_sout.at[1])

        x1 = (x + _dot(y_s5.astype(BF16), wout_ref[:D_S5, :])
              + _dot(y_pool.astype(BF16), wout_ref[D_S5:, :]))
        live["x1"] = x1
        live["h2b"] = _rms(x1, g2_ref[...]).astype(BF16)
        live["acc"] = None

    def ffn_chunk(j):
        rows_j = pl.ds(j * FF_CHUNK, FF_CHUNK)
        convd = []
        for base in (0, D_FF):
            cols = slice(base + j * FF_CHUNK, base + (j + 1) * FF_CHUNK)
            start_out(wup_ref.at[:, cols], wupb_out.at[:, cols], sem_wout.at[4])
            hup = _dot(live["h2b"], wup_ref[:, cols])
            older = convbuf[:, 0, cols]
            newer = convbuf[:, 1, cols]
            convbuf[:, 0, cols] = newer
            convbuf[:, 1, cols] = hup
            convd.append(cb_ref[:, cols] + cw_ref[0:1, cols] * older
                         + cw_ref[1:2, cols] * newer + cw_ref[2:3, cols] * hup)
        start_out(wdown_ref.at[rows_j, :], wdownb_out.at[rows_j, :], sem_wout.at[5])
        act = (_gelu(convd[0]) * convd[1]).astype(BF16)
        part = _dot(act, wdown_ref[j * FF_CHUNK:(j + 1) * FF_CHUNK, :])
        live["acc"] = part if live["acc"] is None else live["acc"] + part

    enqueue_rows(win_hbm, win_ref, 0, lambda: start_out(win_ref, winb_out, sem_wout.at[0]))
    enqueue_rows(wglu_hbm, wglu_ref, 1, lambda: start_out(wglu_ref, wglub_out, sem_wout.at[1]))
    enqueue_rows(wpool_hbm, wpool_ref, 2, lambda: start_out(wpool_ref, wpoolb_out, sem_wout.at[2]))
    enqueue_rows(wout_hbm, wout_ref, 0, mixer)
    for j in range(N_FF_CHUNKS):
        for base in (0, D_FF):
            cols = pl.ds(base + j * FF_CHUNK, FF_CHUNK)
            enqueue(wup_hbm.at[:, cols], 3, wup_ref.at[:, cols])
        rows_j = pl.ds(j * FF_CHUNK, FF_CHUNK)
        enqueue(wdown_hbm.at[rows_j, :], 0, wdown_ref.at[rows_j, :], functools.partial(ffn_chunk, j))

    def read(entry):
        src, sid, slot, _, _ = entry
        return pltpu.make_async_copy(src, stages[sid].at[slot], sem_stage.at[sem_base[sid] + slot])

    started = 0
    for i, entry in enumerate(queue):
        while started < min(len(queue), i + 1 + STAGE_LOOKAHEAD):
            read(queue[started]).start()
            started += 1
        read(entry).wait()
        _, sid, slot, dst, after = entry
        dst[...] = stages[sid][slot].astype(BF16)
        if after is not None:
            after()

    ybuf[:, 0, :] = _rms(live["x1"] + live["acc"], gf_ref[...])
    start_out(convbuf, conv_out, sem_sout.at[2])
    start_out(ybuf, y_hbm, sem_sout.at[3])
    for cp in out_copies:
        cp.wait()


def _s5_tables(a_re, a_im, log_dt, b_re, b_im, c_re, c_im):
    dt = jnp.exp(log_dt)[:, None]
    mag = jnp.exp(dt * a_re)
    abar_re = mag * jnp.cos(dt * a_im)
    abar_im = mag * jnp.sin(dt * a_im)
    nr, ni = abar_re - 1.0, abar_im
    den = a_re * a_re + a_im * a_im
    f_re = ((nr * a_re + ni * a_im) / den)[:, :, None]
    f_im = ((ni * a_re - nr * a_im) / den)[:, :, None]
    bbar_re = f_re * b_re - f_im * b_im
    bbar_im = f_re * b_im + f_im * b_re
    diag = (np.arange(U_PER_SPLIT)[:, None] // S5_CH == np.arange(ST_PER_SPLIT)[None, :] // S5_STATE)
    diag2 = jnp.asarray(np.concatenate([diag, diag], axis=1))

    b5 = jnp.stack([bbar_re, bbar_im]).reshape(2, S5_SPLIT, GROUPS_PER_SPLIT, S5_STATE, S5_CH)
    row = jnp.transpose(b5, (1, 4, 0, 2, 3)).reshape(S5_SPLIT, 1, S5_CH, 2 * ST_PER_SPLIT)
    rep = jnp.broadcast_to(row, (S5_SPLIT, GROUPS_PER_SPLIT, S5_CH, 2 * ST_PER_SPLIT))
    bbd = jnp.where(diag2, rep.reshape(S5_SPLIT, U_PER_SPLIT, 2 * ST_PER_SPLIT), 0.0)

    c5 = jnp.stack([c_re, -c_im]).reshape(2, S5_SPLIT, GROUPS_PER_SPLIT, S5_CH, S5_STATE)
    col = jnp.transpose(c5, (1, 0, 2, 4, 3)).reshape(S5_SPLIT, 2 * ST_PER_SPLIT, S5_CH)
    cbd = jnp.where(diag2.T, jnp.tile(col, (1, 1, GROUPS_PER_SPLIT)), 0.0)
    return abar_re.reshape(1, -1), abar_im.reshape(1, -1), bbd.astype(BF16), cbd.astype(BF16)


def _vmem_spec():
    return pl.BlockSpec(memory_space=pltpu.VMEM)


PROMPT_TC = 64
VMEM_LIMIT_BYTES = 60 * 1024 * 1024


def kernel(x_prompt, x_sample, state_s5_re, state_s5_im, state_pool, state_ffn_conv, norm_mix_g, w_in, s5_a_re, s5_a_im, s5_log_dt, s5_b_re, s5_b_im, s5_c_re, s5_c_im, s5_d, s5_w_glu, pool_w, pool_scale, w_out, norm_ffn_g, ffn_w_up, ffn_conv_w, ffn_conv_b, ffn_w_down, norm_final_g):
    nb, seq, _ = x_prompt.shape
    ns = x_sample.shape[0]
    assert nb == SUBLANES and seq % PROMPT_TC == 0 and x_sample.shape[1] == 1
    assert norm_mix_g.shape[0] == 1, "single layer"

    a_re, a_im, bbd, cbd = _s5_tables(s5_a_re[0], s5_a_im[0], s5_log_dt[0], s5_b_re[0],
                                      s5_b_im[0], s5_c_re[0], s5_c_im[0])
    n_states = S5_GROUPS * S5_STATE
    cparams = dict(vmem_limit_bytes=VMEM_LIMIT_BYTES)
    g1 = norm_mix_g[0].reshape(1, D_MODEL)
    dskip = s5_d[0].reshape(1, D_S5)
    pscale = pool_scale[0].reshape(1, D_POOL)
    g2 = norm_ffn_g[0].reshape(1, D_MODEL)
    cw = ffn_conv_w[0]
    cb = ffn_conv_b[0].reshape(1, 2 * D_FF)
    gf = norm_final_g.reshape(1, D_MODEL)

    big_f32 = (w_in[0], s5_w_glu[0], pool_w[0].reshape(len(POOL_WINDOWS) * POOL_CH, POOL_CH),
               w_out[0], ffn_w_up[0], ffn_w_down[0])
    small = (g1, a_re, a_im, bbd, cbd, dskip, pscale, g2, cw, cb, gf)
    any_spec = pl.BlockSpec(memory_space=pl.ANY)
    pool_in = jnp.transpose(state_pool[0], (1, 0, 2))

    def state_major(s):
        return jnp.transpose(s, (1, 2, 0)).reshape(n_states, ns)

    def sequence_major(s):
        return jnp.transpose(s.reshape(S5_GROUPS, S5_STATE, ns), (2, 0, 1))[None]
    sample_out = pl.pallas_call(
        _sample_kernel,
        in_specs=[any_spec] + [_vmem_spec()] * 2 + [any_spec] * 2 + [_vmem_spec()] * len(small)
                 + [any_spec] * N_BIG_WEIGHTS,
        out_specs=[any_spec] + [_vmem_spec()] * 2 + [any_spec] * (2 + N_BIG_WEIGHTS),
        out_shape=[
            jax.ShapeDtypeStruct((ns, 1, D_MODEL), F32),
            jax.ShapeDtypeStruct((n_states, ns), F32),
            jax.ShapeDtypeStruct((n_states, ns), F32),
            jax.ShapeDtypeStruct((POOL_BUF, ns, D_POOL), F32),
            jax.ShapeDtypeStruct((ns, CONV_W - 1, 2 * D_FF), F32),
        ] + [jax.ShapeDtypeStruct(w.shape, BF16) for w in big_f32],
        scratch_shapes=[pltpu.VMEM(w.shape, BF16) for w in big_f32]
        + [pltpu.VMEM(shape, F32) for shape in STAGE_SHAPES] + [
            pltpu.VMEM((ns, 1, D_MODEL), F32),
            pltpu.VMEM((ns, 1, D_MODEL), F32),
            pltpu.VMEM((POOL_BUF, ns, D_POOL), F32),
            pltpu.VMEM((ns, CONV_W - 1, 2 * D_FF), F32),
            pltpu.VMEM((ns, D_POOL), F32),
            pltpu.SemaphoreType.DMA((sum(shape[0] for shape in STAGE_SHAPES),)),
            pltpu.SemaphoreType.DMA((3,)),
            pltpu.SemaphoreType.DMA((N_BIG_WEIGHTS,)),
            pltpu.SemaphoreType.DMA((4,)),
        ],
        compiler_params=pltpu.CompilerParams(**cparams),
        name="sample_layer",
    )(x_sample, state_major(state_s5_re[0]), state_major(state_s5_im[0]),
      pool_in, state_ffn_conv[0], *small, *big_f32)
    ys, s_re, s_im, s_pool, s_conv = sample_out[:5]
    win_b, wglu_b, wpool_b, wout_b, wup_b, wdown_b = sample_out[5:]
    weights = (g1, win_b, a_re, a_im, bbd, cbd, dskip, wglu_b, wpool_b, pscale, wout_b, g2,
               wup_b, cw, cb, wdown_b, gf)

    tc = PROMPT_TC
    rows = tc * SUBLANES
    const = lambda i: (0, 0)
    y_prompt, p_re, p_im, p_pool, p_conv = pl.pallas_call(
        functools.partial(_prompt_kernel, tc, seq // tc),
        grid=(seq // tc,),
        in_specs=[pl.BlockSpec(memory_space=pl.ANY)] + [_vmem_spec()] * len(weights),
        out_specs=[
            pl.BlockSpec(memory_space=pl.ANY),
            pl.BlockSpec((SUBLANES, n_states), const),
            pl.BlockSpec((SUBLANES, n_states), const),
            pl.BlockSpec((POOL_BUF * SUBLANES, D_POOL), const),
            pl.BlockSpec(((CONV_W - 1) * SUBLANES, 2 * D_FF), const),
        ],
        out_shape=[
            jax.ShapeDtypeStruct((nb, seq, D_MODEL), F32),
            jax.ShapeDtypeStruct((SUBLANES, n_states), F32),
            jax.ShapeDtypeStruct((SUBLANES, n_states), F32),
            jax.ShapeDtypeStruct((POOL_BUF * SUBLANES, D_POOL), F32),
            jax.ShapeDtypeStruct(((CONV_W - 1) * SUBLANES, 2 * D_FF), F32),
        ],
        scratch_shapes=[
            pltpu.VMEM((S5_SPLIT, rows, 2 * ST_PER_SPLIT), BF16),
            pltpu.VMEM((rows, D_FF), BF16),
            pltpu.VMEM((2, tc, SUBLANES, D_MODEL), F32),
            pltpu.VMEM((2, tc, SUBLANES, D_MODEL), F32),
            pltpu.SemaphoreType.DMA((2,)),
            pltpu.SemaphoreType.DMA((2,)),
        ],
        compiler_params=pltpu.CompilerParams(dimension_semantics=("arbitrary",), **cparams),
        name="prompt_layer",
    )(x_prompt, *weights)
    new_pool_p = jnp.transpose(p_pool.reshape(POOL_BUF, nb, D_POOL), (1, 0, 2))[None]
    new_conv_p = jnp.transpose(p_conv.reshape(CONV_W - 1, nb, 2 * D_FF), (1, 0, 2))[None]
    new_re_p = p_re.reshape(1, nb, S5_GROUPS, S5_STATE)
    new_im_p = p_im.reshape(1, nb, S5_GROUPS, S5_STATE)

    return (y_prompt, ys, new_re_p, new_im_p, new_pool_p, new_conv_p,
            sequence_major(s_re), sequence_major(s_im),
            jnp.transpose(s_pool, (1, 0, 2))[None], s_conv[None])
```

```python
import functools

import numpy as np
import jax
import jax.numpy as jnp
from jax import lax
from jax.experimental import pallas as pl
from jax.experimental.pallas import tpu as pltpu

D_MODEL = 1024
D_S5 = 512
S5_CH = 16
S5_GROUPS = 32
S5_STATE = 64
D_POOL = 512
POOL_WINDOWS = (2, 4, 8, 16)
POOL_CH = 128
POOL_BUF = 15
D_FF = 2816
CONV_W = 3
EPS = 1e-6

SUBLANES = 8
LANES = 128
S5_SPLIT = 2
GROUPS_PER_SPLIT = S5_GROUPS // S5_SPLIT
U_PER_SPLIT = GROUPS_PER_SPLIT * S5_CH
ST_PER_SPLIT = GROUPS_PER_SPLIT * S5_STATE
GROUPS_PER_TILE = LANES // S5_STATE
FF_CHUNK = 256
N_FF_CHUNKS = D_FF // FF_CHUNK
SQRT_HALF = float(np.sqrt(0.5).astype(np.float32))

BF16 = jnp.bfloat16
F32 = jnp.float32


def _dot(a, b):
    return jnp.dot(a, b, preferred_element_type=F32)


def _rms(x, g):
    ms = jnp.mean(x * x, axis=-1, keepdims=True)
    return x * lax.rsqrt(ms + EPS) * g


def _gelu(x):
    return 0.5 * x * (1.0 + lax.erf(x * SQRT_HALF))


def _s5_post(y_lin, u, dskip_ref, wglu_ref):
    y = _gelu(y_lin + dskip_ref[...] * u)
    return y * jax.nn.sigmoid(_dot(y.astype(BF16), wglu_ref[...]))


def _pool_project(pooled_cols, wpool_ref, pscale_ref):
    zero = jnp.zeros((POOL_CH, POOL_CH), BF16)
    outs = []
    for pair in range(len(POOL_WINDOWS) // 2):
        w_a = wpool_ref[(2 * pair) * POOL_CH:(2 * pair + 1) * POOL_CH, :]
        w_b = wpool_ref[(2 * pair + 1) * POOL_CH:(2 * pair + 2) * POOL_CH, :]
        w_pair = jnp.concatenate([jnp.concatenate([w_a, zero], axis=1),
                                  jnp.concatenate([zero, w_b], axis=1)], axis=0)
        lhs = jnp.concatenate([pooled_cols[2 * pair], pooled_cols[2 * pair + 1]], axis=1)
        outs.append(_dot(lhs.astype(BF16), w_pair))
    return jnp.concatenate(outs, axis=1) * pscale_ref[...]


def _block_copies(hbm_ref, buf_ref, sem_ref, block, slot, tc, to_hbm):
    copies = []
    for n in range(SUBLANES):
        hbm = hbm_ref.at[n, pl.ds(block * tc, tc), :]
        vmem = buf_ref.at[slot, :, n, :]
        src, dst = (vmem, hbm) if to_hbm else (hbm, vmem)
        copies.append(pltpu.make_async_copy(src, dst, sem_ref.at[slot]))
    return copies


def _prompt_kernel(tc, n_steps,
                   x_hbm, g1_ref, win_ref, are_ref, aim_ref, bbd_ref, cbd_ref, dskip_ref,
                   wglu_ref, wpool_ref, pscale_ref, wout_ref, g2_ref, wup_ref, cw_ref,
                   cb_ref, wdown_ref, gf_ref,
                   y_hbm, sre_ref, sim_ref, pool_ref, conv_ref,
                   h_ref, act_ref, xbuf, ybuf, sem_in, sem_out):
    rows = tc * SUBLANES
    step = pl.program_id(0)
    slot = step % 2

    @pl.when(step == 0)
    def _():
        sre_ref[...] = jnp.zeros_like(sre_ref)
        sim_ref[...] = jnp.zeros_like(sim_ref)
        pool_ref[...] = jnp.zeros_like(pool_ref)
        conv_ref[...] = jnp.zeros_like(conv_ref)
        for cp in _block_copies(x_hbm, xbuf, sem_in, 0, 0, tc, False):
            cp.start()

    @pl.when(step + 1 < n_steps)
    def _():
        for cp in _block_copies(x_hbm, xbuf, sem_in, step + 1, 1 - slot, tc, False):
            cp.start()

    @pl.when(step >= 2)
    def _():
        for cp in _block_copies(y_hbm, ybuf, sem_out, step - 2, slot, tc, True):
            cp.wait()

    for cp in _block_copies(x_hbm, xbuf, sem_in, step, slot, tc, False):
        cp.wait()

    x = xbuf[slot].reshape(rows, D_MODEL)
    hb = _rms(x, g1_ref[...]).astype(BF16)
    proj = _dot(hb, win_ref[...])
    u = proj[:, :D_S5]
    v = proj[:, D_S5:]
    ub = u.astype(BF16)

    y_parts = []
    for k in range(S5_SPLIT):
        ub_k = ub[:, k * U_PER_SPLIT:(k + 1) * U_PER_SPLIT]
        for j in range(ST_PER_SPLIT // LANES):
            re_cols = slice(2 * j * LANES, (2 * j + 1) * LANES)
            im_cols = slice((2 * j + 1) * LANES, (2 * j + 2) * LANES)
            bu = _dot(ub_k, bbd_ref[k, :, 2 * j * LANES:(2 * j + 2) * LANES])
            lanes = slice(k * ST_PER_SPLIT + j * LANES, k * ST_PER_SPLIT + (j + 1) * LANES)
            a_re = jnp.broadcast_to(are_ref[:, lanes], (SUBLANES, LANES))
            a_im = jnp.broadcast_to(aim_ref[:, lanes], (SUBLANES, LANES))
            h_re, h_im = sre_ref[:, lanes], sim_ref[:, lanes]
            for i in range(tc // 2):
                r0 = i * 2 * SUBLANES
                res, ims = [], []
                for s in range(2):
                    rs = slice(r0 + s * SUBLANES, r0 + (s + 1) * SUBLANES)
                    n_re = a_re * h_re - a_im * h_im + bu[rs, :LANES]
                    n_im = a_re * h_im + a_im * h_re + bu[rs, LANES:]
                    h_re, h_im = n_re, n_im
                    res.append(h_re)
                    ims.append(h_im)
                pair = slice(r0, r0 + 2 * SUBLANES)
                h_ref[k, pair, re_cols] = jnp.concatenate(res, axis=0).astype(BF16)
                h_ref[k, pair, im_cols] = jnp.concatenate(ims, axis=0).astype(BF16)
            sre_ref[:, lanes] = h_re
            sim_ref[:, lanes] = h_im
        half = rows // 2
        y_parts.append(jnp.concatenate(
            [_dot(h_ref[k, :half, :], cbd_ref[k]), _dot(h_ref[k, half:, :], cbd_ref[k])], axis=0))
    y_s5 = _s5_post(jnp.concatenate(y_parts, axis=1), u, dskip_ref, wglu_ref)

    vfull = jnp.concatenate([pool_ref[...], v], axis=0)
    pool_ref[...] = vfull[rows:, :]
    t_idx = step * tc + (lax.broadcasted_iota(jnp.int32, (rows, POOL_CH), 0) >> 3)
    pooled = []
    for gi, w in enumerate(POOL_WINDOWS):
        s = vfull[:, gi * POOL_CH:(gi + 1) * POOL_CH]
        span = 1
        while span < w:
            sh = span * SUBLANES
            s = s[sh:, :] + s[:-sh, :]
            span *= 2
        first = (POOL_BUF - (w - 1)) * SUBLANES
        wsum = s[first:first + rows, :]
        cnt = jnp.minimum(t_idx + 1, w).astype(F32)
        pooled.append(wsum / cnt - v[:, gi * POOL_CH:(gi + 1) * POOL_CH])
    y_pool = _pool_project(pooled, wpool_ref, pscale_ref)

    x1 = x + _dot(y_s5.astype(BF16), wout_ref[:D_S5, :]) + _dot(y_pool.astype(BF16), wout_ref[D_S5:, :])

    h2b = _rms(x1, g2_ref[...]).astype(BF16)
    taps = (CONV_W - 1) * SUBLANES
    for j in range(N_FF_CHUNKS):
        convd = []
        for base in (0, D_FF):
            cols = slice(base + j * FF_CHUNK, base + (j + 1) * FF_CHUNK)
            hup = _dot(h2b, wup_ref[:, cols])
            full = jnp.concatenate([conv_ref[:, cols], hup], axis=0)
            conv_ref[:, cols] = hup[rows - taps:, :]
            c = cb_ref[:, cols]
            for kk in range(CONV_W):
                c = c + cw_ref[kk:kk + 1, cols] * full[kk * SUBLANES:kk * SUBLANES + rows, :]
            convd.append(c)
        act_ref[:, j * FF_CHUNK:(j + 1) * FF_CHUNK] = (_gelu(convd[0]) * convd[1]).astype(BF16)
    y = _rms(x1 + _dot(act_ref[...], wdown_ref[...]), gf_ref[...])

    ybuf[slot] = y.reshape(tc, SUBLANES, D_MODEL)
    for cp in _block_copies(y_hbm, ybuf, sem_out, step, slot, tc, True):
        cp.start()

    @pl.when(step == n_steps - 1)
    def _():
        if n_steps >= 2:
            for cp in _block_copies(y_hbm, ybuf, sem_out, step - 1, 1 - slot, tc, True):
                cp.wait()
        for cp in _block_copies(y_hbm, ybuf, sem_out, step, slot, tc, True):
            cp.wait()


N_BIG_WEIGHTS = 6
STAGE_SHAPES = ((6, 256, D_MODEL), (2, 256, D_S5), (2, 256, POOL_CH), (6, D_MODEL, FF_CHUNK))
STAGE_LOOKAHEAD = 5


def _sample_kernel(x_hbm, sre_in, sim_in, pool_hbm, conv_hbm,
                   g1_ref, are_ref, aim_ref, bbd_ref, cbd_ref, dskip_ref, pscale_ref, g2_ref,
                   cw_ref, cb_ref, gf_ref,
                   win_hbm, wglu_hbm, wpool_hbm, wout_hbm, wup_hbm, wdown_hbm,
                   y_hbm, sre_ref, sim_ref, pool_out, conv_out,
                   winb_out, wglub_out, wpoolb_out, woutb_out, wupb_out, wdownb_out,
                   win_ref, wglu_ref, wpool_ref, wout_ref, wup_ref, wdown_ref,
                   stage_model, stage_s5, stage_pool, stage_col,
                   xbuf, ybuf, poolbuf, convbuf, vbuf,
                   sem_stage, sem_state, sem_wout, sem_sout):
    in_copies = [pltpu.make_async_copy(x_hbm, xbuf, sem_state.at[0]),
                 pltpu.make_async_copy(pool_hbm, poolbuf, sem_state.at[1]),
                 pltpu.make_async_copy(conv_hbm, convbuf, sem_state.at[2])]
    for cp in in_copies:
        cp.start()

    stages = (stage_model, stage_s5, stage_pool, stage_col)
    sem_base = [sum(s.shape[0] for s in stages[:i]) for i in range(len(stages))]
    ring_pos = [0] * len(stages)
    queue = []
    out_copies = []
    live = {}

    def enqueue(src, sid, dst, after=None):
        queue.append((src, sid, ring_pos[sid] % stages[sid].shape[0], dst, after))
        ring_pos[sid] += 1

    def start_out(src, dst, sem):
        cp = pltpu.make_async_copy(src, dst, sem)
        cp.start()
        out_copies.append(cp)

    def enqueue_rows(w_hbm, wb_ref, sid, after_last):
        rows = stages[sid].shape[1]
        n_chunks = w_hbm.shape[0] // rows
        for c in range(n_chunks):
            part = pl.ds(c * rows, rows)
            enqueue(w_hbm.at[part, :], sid, wb_ref.at[part, :], after_last if c == n_chunks - 1 else None)

    def mixer():
        for cp in in_copies:
            cp.wait()
        start_out(wout_ref, woutb_out, sem_wout.at[3])
        x = xbuf[:, 0, :]
        hb = _rms(x, g1_ref[...]).astype(BF16)
        proj = _dot(hb, win_ref[...])
        u = proj[:, :D_S5]
        v = proj[:, D_S5:]
        ub = u.astype(BF16)

        y_parts = []
        for k in range(S5_SPLIT):
            bu = _dot(ub[:, k * U_PER_SPLIT:(k + 1) * U_PER_SPLIT], bbd_ref[k])
            h_tiles = []
            for j in range(ST_PER_SPLIT // LANES):
                lanes = slice(k * ST_PER_SPLIT + j * LANES, k * ST_PER_SPLIT + (j + 1) * LANES)
                a_re = are_ref[:, lanes]
                a_im = aim_ref[:, lanes]
                h_re0 = sre_in[lanes, :].T
                h_im0 = sim_in[lanes, :].T
                h_re = a_re * h_re0 - a_im * h_im0 + bu[:, 2 * j * LANES:(2 * j + 1) * LANES]
                h_im = a_re * h_im0 + a_im * h_re0 + bu[:, (2 * j + 1) * LANES:(2 * j + 2) * LANES]
                sre_ref[lanes, :] = h_re.T
                sim_ref[lanes, :] = h_im.T
                h_tiles += [h_re, h_im]
            hcat = jnp.concatenate(h_tiles, axis=1).astype(BF16)
            y_parts.append(_dot(hcat, cbd_ref[k]))
        y_s5 = _s5_post(jnp.concatenate(y_parts, axis=1), u, dskip_ref, wglu_ref)

        pooled = []
        for gi, w in enumerate(POOL_WINDOWS):
            lanes = slice(gi * POOL_CH, (gi + 1) * POOL_CH)
            vc = v[:, lanes]
            wsum = vc
            for back in range(1, w):
                wsum = wsum + poolbuf[POOL_BUF - back, :, lanes]
            pooled.append(wsum / float(w) - vc)
        y_pool = _pool_project(pooled, wpool_ref, pscale_ref)
        vbuf[...] = v
        start_out(poolbuf.at[pl.ds(1, POOL_BUF - 1)], pool_out.at[pl.ds(0, POOL_BUF - 1)], sem_sout.at[0])
        start_out(vbuf, pool_out.at[POOL_BUF - 1], sem_sout.at[1])

        x1 = (x + _dot(y_s5.astype(BF16), wout_ref[:D_S5, :])
              + _dot(y_pool.astype(BF16), wout_ref[D_S5:, :]))
        live["x1"] = x1
        live["h2b"] = _rms(x1, g2_ref[...]).astype(BF16)
        live["acc"] = None

    def ffn_chunk(j):
        rows_j = pl.ds(j * FF_CHUNK, FF_CHUNK)
        convd = []
        for base in (0, D_FF):
            cols = slice(base + j * FF_CHUNK, base + (j + 1) * FF_CHUNK)
            start_out(wup_ref.at[:, cols], wupb_out.at[:, cols], sem_wout.at[4])
            hup = _dot(live["h2b"], wup_ref[:, cols])
            older = convbuf[:, 0, cols]
            newer = convbuf[:, 1, cols]
            convbuf[:, 0, cols] = newer
            convbuf[:, 1, cols] = hup
            convd.append(cb_ref[:, cols] + cw_ref[0:1, cols] * older
                         + cw_ref[1:2, cols] * newer + cw_ref[2:3, cols] * hup)
        start_out(wdown_ref.at[rows_j, :], wdownb_out.at[rows_j, :], sem_wout.at[5])
        act = (_gelu(convd[0]) * convd[1]).astype(BF16)
        part = _dot(act, wdown_ref[j * FF_CHUNK:(j + 1) * FF_CHUNK, :])
        live["acc"] = part if live["acc"] is None else live["acc"] + part

    enqueue_rows(win_hbm, win_ref, 0, lambda: start_out(win_ref, winb_out, sem_wout.at[0]))
    enqueue_rows(wglu_hbm, wglu_ref, 1, lambda: start_out(wglu_ref, wglub_out, sem_wout.at[1]))
    enqueue_rows(wpool_hbm, wpool_ref, 2, lambda: start_out(wpool_ref, wpoolb_out, sem_wout.at[2]))
    enqueue_rows(wout_hbm, wout_ref, 0, mixer)
    for j in range(N_FF_CHUNKS):
        for base in (0, D_FF):
            cols = pl.ds(base + j * FF_CHUNK, FF_CHUNK)
            enqueue(wup_hbm.at[:, cols], 3, wup_ref.at[:, cols])
        rows_j = pl.ds(j * FF_CHUNK, FF_CHUNK)
        enqueue(wdown_hbm.at[rows_j, :], 0, wdown_ref.at[rows_j, :], functools.partial(ffn_chunk, j))

    def read(entry):
        src, sid, slot, _, _ = entry
        return pltpu.make_async_copy(src, stages[sid].at[slot], sem_stage.at[sem_base[sid] + slot])

    started = 0
    for i, entry in enumerate(queue):
        while started < min(len(queue), i + 1 + STAGE_LOOKAHEAD):
            read(queue[started]).start()
            started += 1
        read(entry).wait()
        _, sid, slot, dst, after = entry
        dst[...] = stages[sid][slot].astype(BF16)
        if after is not None:
            after()

    ybuf[:, 0, :] = _rms(live["x1"] + live["acc"], gf_ref[...])
    start_out(convbuf, conv_out, sem_sout.at[2])
    start_out(ybuf, y_hbm, sem_sout.at[3])
    for cp in out_copies:
        cp.wait()


def _s5_tables(a_re, a_im, log_dt, b_re, b_im, c_re, c_im):
    dt = jnp.exp(log_dt)[:, None]
    mag = jnp.exp(dt * a_re)
    abar_re = mag * jnp.cos(dt * a_im)
    abar_im = mag * jnp.sin(dt * a_im)
    nr, ni = abar_re - 1.0, abar_im
    den = a_re * a_re + a_im * a_im
    f_re = ((nr * a_re + ni * a_im) / den)[:, :, None]
    f_im = ((ni * a_re - nr * a_im) / den)[:, :, None]
    bbar_re = f_re * b_re - f_im * b_im
    bbar_im = f_re * b_im + f_im * b_re
    n_tiles = GROUPS_PER_SPLIT // GROUPS_PER_TILE
    state_cols = np.arange(2 * ST_PER_SPLIT)
    col_group = (state_cols // (2 * LANES)) * GROUPS_PER_TILE + (state_cols % LANES) // S5_STATE
    diag = jnp.asarray(np.arange(U_PER_SPLIT)[:, None] // S5_CH == col_group[None, :])

    b6 = jnp.stack([bbar_re, bbar_im]).reshape(2, S5_SPLIT, n_tiles, GROUPS_PER_TILE, S5_STATE, S5_CH)
    row = jnp.transpose(b6, (1, 5, 2, 0, 3, 4)).reshape(S5_SPLIT, 1, S5_CH, 2 * ST_PER_SPLIT)
    rep = jnp.broadcast_to(row, (S5_SPLIT, GROUPS_PER_SPLIT, S5_CH, 2 * ST_PER_SPLIT))
    bbd = jnp.where(diag, rep.reshape(S5_SPLIT, U_PER_SPLIT, 2 * ST_PER_SPLIT), 0.0)

    c6 = jnp.stack([c_re, -c_im]).reshape(2, S5_SPLIT, n_tiles, GROUPS_PER_TILE, S5_CH, S5_STATE)
    col = jnp.transpose(c6, (1, 2, 0, 3, 5, 4)).reshape(S5_SPLIT, 2 * ST_PER_SPLIT, S5_CH)
    cbd = jnp.where(diag.T, jnp.tile(col, (1, 1, GROUPS_PER_SPLIT)), 0.0)
    return abar_re.reshape(1, -1), abar_im.reshape(1, -1), bbd.astype(BF16), cbd.astype(BF16)


def _vmem_spec():
    return pl.BlockSpec(memory_space=pltpu.VMEM)


PROMPT_TC = 64
VMEM_LIMIT_BYTES = 60 * 1024 * 1024


def kernel(x_prompt, x_sample, state_s5_re, state_s5_im, state_pool, state_ffn_conv, norm_mix_g, w_in, s5_a_re, s5_a_im, s5_log_dt, s5_b_re, s5_b_im, s5_c_re, s5_c_im, s5_d, s5_w_glu, pool_w, pool_scale, w_out, norm_ffn_g, ffn_w_up, ffn_conv_w, ffn_conv_b, ffn_w_down, norm_final_g):
    nb, seq, _ = x_prompt.shape
    ns = x_sample.shape[0]
    assert nb == SUBLANES and seq % PROMPT_TC == 0 and x_sample.shape[1] == 1
    assert norm_mix_g.shape[0] == 1, "single layer"

    a_re, a_im, bbd, cbd = _s5_tables(s5_a_re[0], s5_a_im[0], s5_log_dt[0], s5_b_re[0],
                                      s5_b_im[0], s5_c_re[0], s5_c_im[0])
    n_states = S5_GROUPS * S5_STATE
    cparams = dict(vmem_limit_bytes=VMEM_LIMIT_BYTES)
    g1 = norm_mix_g[0].reshape(1, D_MODEL)
    dskip = s5_d[0].reshape(1, D_S5)
    pscale = pool_scale[0].reshape(1, D_POOL)
    g2 = norm_ffn_g[0].reshape(1, D_MODEL)
    cw = ffn_conv_w[0]
    cb = ffn_conv_b[0].reshape(1, 2 * D_FF)
    gf = norm_final_g.reshape(1, D_MODEL)

    big_f32 = (w_in[0], s5_w_glu[0], pool_w[0].reshape(len(POOL_WINDOWS) * POOL_CH, POOL_CH),
               w_out[0], ffn_w_up[0], ffn_w_down[0])
    small = (g1, a_re, a_im, bbd, cbd, dskip, pscale, g2, cw, cb, gf)
    any_spec = pl.BlockSpec(memory_space=pl.ANY)
    pool_in = jnp.transpose(state_pool[0], (1, 0, 2))

    def state_major(s):
        return jnp.transpose(s, (1, 2, 0)).reshape(n_states, ns)

    def sequence_major(s):
        return jnp.transpose(s.reshape(S5_GROUPS, S5_STATE, ns), (2, 0, 1))[None]
    sample_out = pl.pallas_call(
        _sample_kernel,
        in_specs=[any_spec] + [_vmem_spec()] * 2 + [any_spec] * 2 + [_vmem_spec()] * len(small)
                 + [any_spec] * N_BIG_WEIGHTS,
        out_specs=[any_spec] + [_vmem_spec()] * 2 + [any_spec] * (2 + N_BIG_WEIGHTS),
        out_shape=[
            jax.ShapeDtypeStruct((ns, 1, D_MODEL), F32),
            jax.ShapeDtypeStruct((n_states, ns), F32),
            jax.ShapeDtypeStruct((n_states, ns), F32),
            jax.ShapeDtypeStruct((POOL_BUF, ns, D_POOL), F32),
            jax.ShapeDtypeStruct((ns, CONV_W - 1, 2 * D_FF), F32),
        ] + [jax.ShapeDtypeStruct(w.shape, BF16) for w in big_f32],
        scratch_shapes=[pltpu.VMEM(w.shape, BF16) for w in big_f32]
        + [pltpu.VMEM(shape, F32) for shape in STAGE_SHAPES] + [
            pltpu.VMEM((ns, 1, D_MODEL), F32),
            pltpu.VMEM((ns, 1, D_MODEL), F32),
            pltpu.VMEM((POOL_BUF, ns, D_POOL), F32),
            pltpu.VMEM((ns, CONV_W - 1, 2 * D_FF), F32),
            pltpu.VMEM((ns, D_POOL), F32),
            pltpu.SemaphoreType.DMA((sum(shape[0] for shape in STAGE_SHAPES),)),
            pltpu.SemaphoreType.DMA((3,)),
            pltpu.SemaphoreType.DMA((N_BIG_WEIGHTS,)),
            pltpu.SemaphoreType.DMA((4,)),
        ],
        compiler_params=pltpu.CompilerParams(**cparams),
        name="sample_layer",
    )(x_sample, state_major(state_s5_re[0]), state_major(state_s5_im[0]),
      pool_in, state_ffn_conv[0], *small, *big_f32)
    ys, s_re, s_im, s_pool, s_conv = sample_out[:5]
    win_b, wglu_b, wpool_b, wout_b, wup_b, wdown_b = sample_out[5:]
    weights = (g1, win_b, a_re, a_im, bbd, cbd, dskip, wglu_b, wpool_b, pscale, wout_b, g2,
               wup_b, cw, cb, wdown_b, gf)

    tc = PROMPT_TC
    rows = tc * SUBLANES
    const = lambda i: (0, 0)
    y_prompt, p_re, p_im, p_pool, p_conv = pl.pallas_call(
        functools.partial(_prompt_kernel, tc, seq // tc),
        grid=(seq // tc,),
        in_specs=[pl.BlockSpec(memory_space=pl.ANY)] + [_vmem_spec()] * len(weights),
        out_specs=[
            pl.BlockSpec(memory_space=pl.ANY),
            pl.BlockSpec((SUBLANES, n_states), const),
            pl.BlockSpec((SUBLANES, n_states), const),
            pl.BlockSpec((POOL_BUF * SUBLANES, D_POOL), const),
            pl.BlockSpec(((CONV_W - 1) * SUBLANES, 2 * D_FF), const),
        ],
        out_shape=[
            jax.ShapeDtypeStruct((nb, seq, D_MODEL), F32),
            jax.ShapeDtypeStruct((SUBLANES, n_states), F32),
            jax.ShapeDtypeStruct((SUBLANES, n_states), F32),
            jax.ShapeDtypeStruct((POOL_BUF * SUBLANES, D_POOL), F32),
            jax.ShapeDtypeStruct(((CONV_W - 1) * SUBLANES, 2 * D_FF), F32),
        ],
        scratch_shapes=[
            pltpu.VMEM((S5_SPLIT, rows, 2 * ST_PER_SPLIT), BF16),
            pltpu.VMEM((rows, D_FF), BF16),
            pltpu.VMEM((2, tc, SUBLANES, D_MODEL), F32),
            pltpu.VMEM((2, tc, SUBLANES, D_MODEL), F32),
            pltpu.SemaphoreType.DMA((2,)),
            pltpu.SemaphoreType.DMA((2,)),
        ],
        compiler_params=pltpu.CompilerParams(dimension_semantics=("arbitrary",), **cparams),
        name="prompt_layer",
    )(x_prompt, *weights)
    new_pool_p = jnp.transpose(p_pool.reshape(POOL_BUF, nb, D_POOL), (1, 0, 2))[None]
    new_conv_p = jnp.transpose(p_conv.reshape(CONV_W - 1, nb, 2 * D_FF), (1, 0, 2))[None]
    new_re_p = p_re.reshape(1, nb, S5_GROUPS, S5_STATE)
    new_im_p = p_im.reshape(1, nb, S5_GROUPS, S5_STATE)

    return (y_prompt, ys, new_re_p, new_im_p, new_pool_p, new_conv_p,
            sequence_major(s_re), sequence_major(s_im),
            jnp.transpose(s_pool, (1, 0, 2))[None], s_conv[None])
```

```python
import functools

import numpy as np
import jax
import jax.numpy as jnp
from jax import lax
from jax.experimental import pallas as pl
from jax.experimental.pallas import tpu as pltpu

D_MODEL = 1024
D_S5 = 512
S5_CH = 16
S5_GROUPS = 32
S5_STATE = 64
D_POOL = 512
POOL_WINDOWS = (2, 4, 8, 16)
POOL_CH = 128
POOL_BUF = 15
D_FF = 2816
CONV_W = 3
EPS = 1e-6

SUBLANES = 8
LANES = 128
S5_SPLIT = 2
GROUPS_PER_SPLIT = S5_GROUPS // S5_SPLIT
U_PER_SPLIT = GROUPS_PER_SPLIT * S5_CH
ST_PER_SPLIT = GROUPS_PER_SPLIT * S5_STATE
GROUPS_PER_TILE = LANES // S5_STATE
FF_CHUNK = 256
N_FF_CHUNKS = D_FF // FF_CHUNK
SQRT_HALF = float(np.sqrt(0.5).astype(np.float32))

BF16 = jnp.bfloat16
F32 = jnp.float32


def _dot(a, b):
    return jnp.dot(a, b, preferred_element_type=F32)


def _rms(x, g):
    ms = jnp.mean(x * x, axis=-1, keepdims=True)
    return x * lax.rsqrt(ms + EPS) * g


def _gelu(x):
    return 0.5 * x * (1.0 + lax.erf(x * SQRT_HALF))


def _s5_post(y_lin, u, dskip_ref, wglu_ref):
    y = _gelu(y_lin + dskip_ref[...] * u)
    return y * jax.nn.sigmoid(_dot(y.astype(BF16), wglu_ref[...]))


def _pool_project(pooled_cols, wpool_ref, pscale_ref):
    zero = jnp.zeros((POOL_CH, POOL_CH), BF16)
    outs = []
    for pair in range(len(POOL_WINDOWS) // 2):
        w_a = wpool_ref[(2 * pair) * POOL_CH:(2 * pair + 1) * POOL_CH, :]
        w_b = wpool_ref[(2 * pair + 1) * POOL_CH:(2 * pair + 2) * POOL_CH, :]
        w_pair = jnp.concatenate([jnp.concatenate([w_a, zero], axis=1),
                                  jnp.concatenate([zero, w_b], axis=1)], axis=0)
        lhs = jnp.concatenate([pooled_cols[2 * pair], pooled_cols[2 * pair + 1]], axis=1)
        outs.append(_dot(lhs.astype(BF16), w_pair))
    return jnp.concatenate(outs, axis=1) * pscale_ref[...]


def _block_copies(hbm_ref, buf_ref, sem_ref, block, slot, tc, to_hbm):
    copies = []
    for n in range(SUBLANES):
        hbm = hbm_ref.at[n, pl.ds(block * tc, tc), :]
        vmem = buf_ref.at[slot, :, n, :]
        src, dst = (vmem, hbm) if to_hbm else (hbm, vmem)
        copies.append(pltpu.make_async_copy(src, dst, sem_ref.at[slot]))
    return copies


def _prompt_kernel(tc, n_steps,
                   x_hbm, g1_ref, win_ref, are_ref, aim_ref, bbd_ref, cbd_ref, dskip_ref,
                   wglu_ref, wpool_ref, pscale_ref, wout_ref, g2_ref, wup_ref, cw_ref,
                   cb_ref, wdown_ref, gf_ref,
                   y_hbm, sre_ref, sim_ref, pool_ref, conv_ref,
                   act_ref, xbuf, ybuf, sem_in, sem_out):
    rows = tc * SUBLANES
    step = pl.program_id(0)
    slot = step % 2

    @pl.when(step == 0)
    def _():
        sre_ref[...] = jnp.zeros_like(sre_ref)
        sim_ref[...] = jnp.zeros_like(sim_ref)
        pool_ref[...] = jnp.zeros_like(pool_ref)
        conv_ref[...] = jnp.zeros_like(conv_ref)
        for cp in _block_copies(x_hbm, xbuf, sem_in, 0, 0, tc, False):
            cp.start()

    @pl.when(step + 1 < n_steps)
    def _():
        for cp in _block_copies(x_hbm, xbuf, sem_in, step + 1, 1 - slot, tc, False):
            cp.start()

    @pl.when(step >= 2)
    def _():
        for cp in _block_copies(y_hbm, ybuf, sem_out, step - 2, slot, tc, True):
            cp.wait()

    for cp in _block_copies(x_hbm, xbuf, sem_in, step, slot, tc, False):
        cp.wait()

    x = xbuf[slot].reshape(rows, D_MODEL)
    hb = _rms(x, g1_ref[...]).astype(BF16)
    proj = _dot(hb, win_ref[...])
    u = proj[:, :D_S5]
    v = proj[:, D_S5:]
    ub = u.astype(BF16)

    y_parts = []
    for k in range(S5_SPLIT):
        ub_k = ub[:, k * U_PER_SPLIT:(k + 1) * U_PER_SPLIT]
        for j in range(ST_PER_SPLIT // LANES):
            tile = slice(2 * j * LANES, (2 * j + 2) * LANES)
            bu = _dot(ub_k, bbd_ref[k, :, tile])
            lanes = slice(k * ST_PER_SPLIT + j * LANES, k * ST_PER_SPLIT + (j + 1) * LANES)
            a_re = jnp.broadcast_to(are_ref[:, lanes], (SUBLANES, LANES))
            a_im = jnp.broadcast_to(aim_ref[:, lanes], (SUBLANES, LANES))
            h_re, h_im = sre_ref[:, lanes], sim_ref[:, lanes]
            h_rows = []
            for i in range(tc // 2):
                r0 = i * 2 * SUBLANES
                res, ims = [], []
                for s in range(2):
                    rs = slice(r0 + s * SUBLANES, r0 + (s + 1) * SUBLANES)
                    n_re = a_re * h_re - a_im * h_im + bu[rs, :LANES]
                    n_im = a_re * h_im + a_im * h_re + bu[rs, LANES:]
                    h_re, h_im = n_re, n_im
                    res.append(h_re)
                    ims.append(h_im)
                h_rows.append(jnp.concatenate(
                    [jnp.concatenate(res, axis=0), jnp.concatenate(ims, axis=0)], axis=1).astype(BF16))
            sre_ref[:, lanes] = h_re
            sim_ref[:, lanes] = h_im
            part = _dot(jnp.concatenate(h_rows, axis=0), cbd_ref[k, tile, :])
            y_k = part if j == 0 else y_k + part
        y_parts.append(y_k)
    y_s5 = _s5_post(jnp.concatenate(y_parts, axis=1), u, dskip_ref, wglu_ref)

    vfull = jnp.concatenate([pool_ref[...], v], axis=0)
    pool_ref[...] = vfull[rows:, :]
    t_idx = step * tc + (lax.broadcasted_iota(jnp.int32, (rows, POOL_CH), 0) >> 3)
    pooled = []
    for gi, w in enumerate(POOL_WINDOWS):
        s = vfull[:, gi * POOL_CH:(gi + 1) * POOL_CH]
        span = 1
        while span < w:
            sh = span * SUBLANES
            s = s[sh:, :] + s[:-sh, :]
            span *= 2
        first = (POOL_BUF - (w - 1)) * SUBLANES
        wsum = s[first:first + rows, :]
        cnt = jnp.minimum(t_idx + 1, w).astype(F32)
        pooled.append(wsum / cnt - v[:, gi * POOL_CH:(gi + 1) * POOL_CH])
    y_pool = _pool_project(pooled, wpool_ref, pscale_ref)

    x1 = x + _dot(y_s5.astype(BF16), wout_ref[:D_S5, :]) + _dot(y_pool.astype(BF16), wout_ref[D_S5:, :])

    h2b = _rms(x1, g2_ref[...]).astype(BF16)
    taps = (CONV_W - 1) * SUBLANES
    for j in range(N_FF_CHUNKS):
        convd = []
        for base in (0, D_FF):
            cols = slice(base + j * FF_CHUNK, base + (j + 1) * FF_CHUNK)
            hup = _dot(h2b, wup_ref[:, cols])
            full = jnp.concatenate([conv_ref[:, cols], hup], axis=0)
            conv_ref[:, cols] = hup[rows - taps:, :]
            c = cb_ref[:, cols]
            for kk in range(CONV_W):
                c = c + cw_ref[kk:kk + 1, cols] * full[kk * SUBLANES:kk * SUBLANES + rows, :]
            convd.append(c)
        act_ref[:, j * FF_CHUNK:(j + 1) * FF_CHUNK] = (_gelu(convd[0]) * convd[1]).astype(BF16)
    y = _rms(x1 + _dot(act_ref[...], wdown_ref[...]), gf_ref[...])

    ybuf[slot] = y.reshape(tc, SUBLANES, D_MODEL)
    for cp in _block_copies(y_hbm, ybuf, sem_out, step, slot, tc, True):
        cp.start()

    @pl.when(step == n_steps - 1)
    def _():
        if n_steps >= 2:
            for cp in _block_copies(y_hbm, ybuf, sem_out, step - 1, 1 - slot, tc, True):
                cp.wait()
        for cp in _block_copies(y_hbm, ybuf, sem_out, step, slot, tc, True):
            cp.wait()


N_BIG_WEIGHTS = 6
STAGE_SHAPES = ((6, 256, D_MODEL), (2, 256, D_S5), (2, 256, POOL_CH), (6, D_MODEL, FF_CHUNK))
STAGE_LOOKAHEAD = 5


def _sample_kernel(x_hbm, sre_in, sim_in, pool_hbm, conv_hbm,
                   g1_ref, are_ref, aim_ref, bbd_ref, cbd_ref, dskip_ref, pscale_ref, g2_ref,
                   cw_ref, cb_ref, gf_ref,
                   win_hbm, wglu_hbm, wpool_hbm, wout_hbm, wup_hbm, wdown_hbm,
                   y_hbm, sre_ref, sim_ref, pool_out, conv_out,
                   winb_out, wglub_out, wpoolb_out, woutb_out, wupb_out, wdownb_out,
                   win_ref, wglu_ref, wpool_ref, wout_ref, wup_ref, wdown_ref,
                   stage_model, stage_s5, stage_pool, stage_col,
                   xbuf, ybuf, poolbuf, convbuf, vbuf,
                   sem_stage, sem_state, sem_wout, sem_sout):
    in_copies = [pltpu.make_async_copy(x_hbm, xbuf, sem_state.at[0]),
                 pltpu.make_async_copy(pool_hbm, poolbuf, sem_state.at[1]),
                 pltpu.make_async_copy(conv_hbm, convbuf, sem_state.at[2])]
    for cp in in_copies:
        cp.start()

    stages = (stage_model, stage_s5, stage_pool, stage_col)
    sem_base = [sum(s.shape[0] for s in stages[:i]) for i in range(len(stages))]
    ring_pos = [0] * len(stages)
    queue = []
    out_copies = []
    live = {}

    def enqueue(src, sid, dst, after=None):
        queue.append((src, sid, ring_pos[sid] % stages[sid].shape[0], dst, after))
        ring_pos[sid] += 1

    def start_out(src, dst, sem):
        cp = pltpu.make_async_copy(src, dst, sem)
        cp.start()
        out_copies.append(cp)

    def enqueue_rows(w_hbm, wb_ref, sid, after_last):
        rows = stages[sid].shape[1]
        n_chunks = w_hbm.shape[0] // rows
        for c in range(n_chunks):
            part = pl.ds(c * rows, rows)
            enqueue(w_hbm.at[part, :], sid, wb_ref.at[part, :], after_last if c == n_chunks - 1 else None)

    def mixer():
        for cp in in_copies:
            cp.wait()
        start_out(wout_ref, woutb_out, sem_wout.at[3])
        x = xbuf[:, 0, :]
        hb = _rms(x, g1_ref[...]).astype(BF16)
        proj = _dot(hb, win_ref[...])
        u = proj[:, :D_S5]
        v = proj[:, D_S5:]
        ub = u.astype(BF16)

        y_parts = []
        for k in range(S5_SPLIT):
            bu = _dot(ub[:, k * U_PER_SPLIT:(k + 1) * U_PER_SPLIT], bbd_ref[k])
            h_tiles = []
            for j in range(ST_PER_SPLIT // LANES):
                lanes = slice(k * ST_PER_SPLIT + j * LANES, k * ST_PER_SPLIT + (j + 1) * LANES)
                a_re = are_ref[:, lanes]
                a_im = aim_ref[:, lanes]
                h_re0 = sre_in[lanes, :].T
                h_im0 = sim_in[lanes, :].T
                h_re = a_re * h_re0 - a_im * h_im0 + bu[:, 2 * j * LANES:(2 * j + 1) * LANES]
                h_im = a_re * h_im0 + a_im * h_re0 + bu[:, (2 * j + 1) * LANES:(2 * j + 2) * LANES]
                sre_ref[lanes, :] = h_re.T
                sim_ref[lanes, :] = h_im.T
                h_tiles += [h_re, h_im]
            hcat = jnp.concatenate(h_tiles, axis=1).astype(BF16)
            y_parts.append(_dot(hcat, cbd_ref[k]))
        y_s5 = _s5_post(jnp.concatenate(y_parts, axis=1), u, dskip_ref, wglu_ref)

        pooled = []
        for gi, w in enumerate(POOL_WINDOWS):
            lanes = slice(gi * POOL_CH, (gi + 1) * POOL_CH)
            vc = v[:, lanes]
            wsum = vc
            for back in range(1, w):
                wsum = wsum + poolbuf[POOL_BUF - back, :, lanes]
            pooled.append(wsum / float(w) - vc)
        y_pool = _pool_project(pooled, wpool_ref, pscale_ref)
        vbuf[...] = v
        start_out(poolbuf.at[pl.ds(1, POOL_BUF - 1)], pool_out.at[pl.ds(0, POOL_BUF - 1)], sem_sout.at[0])
        start_out(vbuf, pool_out.at[POOL_BUF - 1], sem_sout.at[1])

        x1 = (x + _dot(y_s5.astype(BF16), wout_ref[:D_S5, :])
              + _dot(y_pool.astype(BF16), wout_ref[D_S5:, :]))
        live["x1"] = x1
        live["h2b"] = _rms(x1, g2_ref[...]).astype(BF16)
        live["acc"] = None

    def ffn_chunk(j):
        rows_j = pl.ds(j * FF_CHUNK, FF_CHUNK)
        convd = []
        for base in (0, D_FF):
            cols = slice(base + j * FF_CHUNK, base + (j + 1) * FF_CHUNK)
            start_out(wup_ref.at[:, cols], wupb_out.at[:, cols], sem_wout.at[4])
            hup = _dot(live["h2b"], wup_ref[:, cols])
            older = convbuf[:, 0, cols]
            newer = convbuf[:, 1, cols]
            convbuf[:, 0, cols] = newer
            convbuf[:, 1, cols] = hup
            convd.append(cb_ref[:, cols] + cw_ref[0:1, cols] * older
                         + cw_ref[1:2, cols] * newer + cw_ref[2:3, cols] * hup)
        start_out(wdown_ref.at[rows_j, :], wdownb_out.at[rows_j, :], sem_wout.at[5])
        act = (_gelu(convd[0]) * convd[1]).astype(BF16)
        part = _dot(act, wdown_ref[j * FF_CHUNK:(j + 1) * FF_CHUNK, :])
        live["acc"] = part if live["acc"] is None else live["acc"] + part

    enqueue_rows(win_hbm, win_ref, 0, lambda: start_out(win_ref, winb_out, sem_wout.at[0]))
    enqueue_rows(wglu_hbm, wglu_ref, 1, lambda: start_out(wglu_ref, wglub_out, sem_wout.at[1]))
    enqueue_rows(wpool_hbm, wpool_ref, 2, lambda: start_out(wpool_ref, wpoolb_out, sem_wout.at[2]))
    enqueue_rows(wout_hbm, wout_ref, 0, mixer)
    for j in range(N_FF_CHUNKS):
        for base in (0, D_FF):
            cols = pl.ds(base + j * FF_CHUNK, FF_CHUNK)
            enqueue(wup_hbm.at[:, cols], 3, wup_ref.at[:, cols])
        rows_j = pl.ds(j * FF_CHUNK, FF_CHUNK)
        enqueue(wdown_hbm.at[rows_j, :], 0, wdown_ref.at[rows_j, :], functools.partial(ffn_chunk, j))

    def read(entry):
        src, sid, slot, _, _ = entry
        return pltpu.make_async_copy(src, stages[sid].at[slot], sem_stage.at[sem_base[sid] + slot])

    started = 0
    for i, entry in enumerate(queue):
        while started < min(len(queue), i + 1 + STAGE_LOOKAHEAD):
            read(queue[started]).start()
            started += 1
        read(entry).wait()
        _, sid, slot, dst, after = entry
        dst[...] = stages[sid][slot].astype(BF16)
        if after is not None:
            after()

    ybuf[:, 0, :] = _rms(live["x1"] + live["acc"], gf_ref[...])
    start_out(convbuf, conv_out, sem_sout.at[2])
    start_out(ybuf, y_hbm, sem_sout.at[3])
    for cp in out_copies:
        cp.wait()


def _s5_tables(a_re, a_im, log_dt, b_re, b_im, c_re, c_im):
    dt = jnp.exp(log_dt)[:, None]
    mag = jnp.exp(dt * a_re)
    abar_re = mag * jnp.cos(dt * a_im)
    abar_im = mag * jnp.sin(dt * a_im)
    nr, ni = abar_re - 1.0, abar_im
    den = a_re * a_re + a_im * a_im
    f_re = ((nr * a_re + ni * a_im) / den)[:, :, None]
    f_im = ((ni * a_re - nr * a_im) / den)[:, :, None]
    bbar_re = f_re * b_re - f_im * b_im
    bbar_im = f_re * b_im + f_im * b_re
    n_tiles = GROUPS_PER_SPLIT // GROUPS_PER_TILE
    state_cols = np.arange(2 * ST_PER_SPLIT)
    col_group = (state_cols // (2 * LANES)) * GROUPS_PER_TILE + (state_cols % LANES) // S5_STATE
    diag = jnp.asarray(np.arange(U_PER_SPLIT)[:, None] // S5_CH == col_group[None, :])

    b6 = jnp.stack([bbar_re, bbar_im]).reshape(2, S5_SPLIT, n_tiles, GROUPS_PER_TILE, S5_STATE, S5_CH)
    row = jnp.transpose(b6, (1, 5, 2, 0, 3, 4)).reshape(S5_SPLIT, 1, S5_CH, 2 * ST_PER_SPLIT)
    rep = jnp.broadcast_to(row, (S5_SPLIT, GROUPS_PER_SPLIT, S5_CH, 2 * ST_PER_SPLIT))
    bbd = jnp.where(diag, rep.reshape(S5_SPLIT, U_PER_SPLIT, 2 * ST_PER_SPLIT), 0.0)

    c6 = jnp.stack([c_re, -c_im]).reshape(2, S5_SPLIT, n_tiles, GROUPS_PER_TILE, S5_CH, S5_STATE)
    col = jnp.transpose(c6, (1, 2, 0, 3, 5, 4)).reshape(S5_SPLIT, 2 * ST_PER_SPLIT, S5_CH)
    cbd = jnp.where(diag.T, jnp.tile(col, (1, 1, GROUPS_PER_SPLIT)), 0.0)
    return abar_re.reshape(1, -1), abar_im.reshape(1, -1), bbd.astype(BF16), cbd.astype(BF16)


def _vmem_spec():
    return pl.BlockSpec(memory_space=pltpu.VMEM)


PROMPT_TC = 64
VMEM_LIMIT_BYTES = 60 * 1024 * 1024


def kernel(x_prompt, x_sample, state_s5_re, state_s5_im, state_pool, state_ffn_conv, norm_mix_g, w_in, s5_a_re, s5_a_im, s5_log_dt, s5_b_re, s5_b_im, s5_c_re, s5_c_im, s5_d, s5_w_glu, pool_w, pool_scale, w_out, norm_ffn_g, ffn_w_up, ffn_conv_w, ffn_conv_b, ffn_w_down, norm_final_g):
    nb, seq, _ = x_prompt.shape
    ns = x_sample.shape[0]
    assert nb == SUBLANES and seq % PROMPT_TC == 0 and x_sample.shape[1] == 1
    assert norm_mix_g.shape[0] == 1, "single layer"

    a_re, a_im, bbd, cbd = _s5_tables(s5_a_re[0], s5_a_im[0], s5_log_dt[0], s5_b_re[0],
                                      s5_b_im[0], s5_c_re[0], s5_c_im[0])
    n_states = S5_GROUPS * S5_STATE
    cparams = dict(vmem_limit_bytes=VMEM_LIMIT_BYTES)
    g1 = norm_mix_g[0].reshape(1, D_MODEL)
    dskip = s5_d[0].reshape(1, D_S5)
    pscale = pool_scale[0].reshape(1, D_POOL)
    g2 = norm_ffn_g[0].reshape(1, D_MODEL)
    cw = ffn_conv_w[0]
    cb = ffn_conv_b[0].reshape(1, 2 * D_FF)
    gf = norm_final_g.reshape(1, D_MODEL)

    big_f32 = (w_in[0], s5_w_glu[0], pool_w[0].reshape(len(POOL_WINDOWS) * POOL_CH, POOL_CH),
               w_out[0], ffn_w_up[0], ffn_w_down[0])
    small = (g1, a_re, a_im, bbd, cbd, dskip, pscale, g2, cw, cb, gf)
    any_spec = pl.BlockSpec(memory_space=pl.ANY)
    pool_in = jnp.transpose(state_pool[0], (1, 0, 2))

    def state_major(s):
        return jnp.transpose(s, (1, 2, 0)).reshape(n_states, ns)

    def sequence_major(s):
        return jnp.transpose(s.reshape(S5_GROUPS, S5_STATE, ns), (2, 0, 1))[None]
    sample_out = pl.pallas_call(
        _sample_kernel,
        in_specs=[any_spec] + [_vmem_spec()] * 2 + [any_spec] * 2 + [_vmem_spec()] * len(small)
                 + [any_spec] * N_BIG_WEIGHTS,
        out_specs=[any_spec] + [_vmem_spec()] * 2 + [any_spec] * (2 + N_BIG_WEIGHTS),
        out_shape=[
            jax.ShapeDtypeStruct((ns, 1, D_MODEL), F32),
            jax.ShapeDtypeStruct((n_states, ns), F32),
            jax.ShapeDtypeStruct((n_states, ns), F32),
            jax.ShapeDtypeStruct((POOL_BUF, ns, D_POOL), F32),
            jax.ShapeDtypeStruct((ns, CONV_W - 1, 2 * D_FF), F32),
        ] + [jax.ShapeDtypeStruct(w.shape, BF16) for w in big_f32],
        scratch_shapes=[pltpu.VMEM(w.shape, BF16) for w in big_f32]
        + [pltpu.VMEM(shape, F32) for shape in STAGE_SHAPES] + [
            pltpu.VMEM((ns, 1, D_MODEL), F32),
            pltpu.VMEM((ns, 1, D_MODEL), F32),
            pltpu.VMEM((POOL_BUF, ns, D_POOL), F32),
            pltpu.VMEM((ns, CONV_W - 1, 2 * D_FF), F32),
            pltpu.VMEM((ns, D_POOL), F32),
            pltpu.SemaphoreType.DMA((sum(shape[0] for shape in STAGE_SHAPES),)),
            pltpu.SemaphoreType.DMA((3,)),
            pltpu.SemaphoreType.DMA((N_BIG_WEIGHTS,)),
            pltpu.SemaphoreType.DMA((4,)),
        ],
        compiler_params=pltpu.CompilerParams(**cparams),
        name="sample_layer",
    )(x_sample, state_major(state_s5_re[0]), state_major(state_s5_im[0]),
      pool_in, state_ffn_conv[0], *small, *big_f32)
    ys, s_re, s_im, s_pool, s_conv = sample_out[:5]
    win_b, wglu_b, wpool_b, wout_b, wup_b, wdown_b = sample_out[5:]
    weights = (g1, win_b, a_re, a_im, bbd, cbd, dskip, wglu_b, wpool_b, pscale, wout_b, g2,
               wup_b, cw, cb, wdown_b, gf)

    tc = PROMPT_TC
    rows = tc * SUBLANES
    const = lambda i: (0, 0)
    y_prompt, p_re, p_im, p_pool, p_conv = pl.pallas_call(
        functools.partial(_prompt_kernel, tc, seq // tc),
        grid=(seq // tc,),
        in_specs=[pl.BlockSpec(memory_space=pl.ANY)] + [_vmem_spec()] * len(weights),
        out_specs=[
            pl.BlockSpec(memory_space=pl.ANY),
            pl.BlockSpec((SUBLANES, n_states), const),
            pl.BlockSpec((SUBLANES, n_states), const),
            pl.BlockSpec((POOL_BUF * SUBLANES, D_POOL), const),
            pl.BlockSpec(((CONV_W - 1) * SUBLANES, 2 * D_FF), const),
        ],
        out_shape=[
            jax.ShapeDtypeStruct((nb, seq, D_MODEL), F32),
            jax.ShapeDtypeStruct((SUBLANES, n_states), F32),
            jax.ShapeDtypeStruct((SUBLANES, n_states), F32),
            jax.ShapeDtypeStruct((POOL_BUF * SUBLANES, D_POOL), F32),
            jax.ShapeDtypeStruct(((CONV_W - 1) * SUBLANES, 2 * D_FF), F32),
        ],
        scratch_shapes=[
            pltpu.VMEM((rows, D_FF), BF16),
            pltpu.VMEM((2, tc, SUBLANES, D_MODEL), F32),
            pltpu.VMEM((2, tc, SUBLANES, D_MODEL), F32),
            pltpu.SemaphoreType.DMA((2,)),
            pltpu.SemaphoreType.DMA((2,)),
        ],
        compiler_params=pltpu.CompilerParams(dimension_semantics=("arbitrary",), **cparams),
        name="prompt_layer",
    )(x_prompt, *weights)
    new_pool_p = jnp.transpose(p_pool.reshape(POOL_BUF, nb, D_POOL), (1, 0, 2))[None]
    new_conv_p = jnp.transpose(p_conv.reshape(CONV_W - 1, nb, 2 * D_FF), (1, 0, 2))[None]
    new_re_p = p_re.reshape(1, nb, S5_GROUPS, S5_STATE)
    new_im_p = p_im.reshape(1, nb, S5_GROUPS, S5_STATE)

    return (y_prompt, ys, new_re_p, new_im_p, new_pool_p, new_conv_p,
            sequence_major(s_re), sequence_major(s_im),
            jnp.transpose(s_pool, (1, 0, 2))[None], s_conv[None])
```

```python
import functools

import numpy as np
import jax
import jax.numpy as jnp
from jax import lax
from jax.experimental import pallas as pl
from jax.experimental.pallas import tpu as pltpu

D_MODEL = 1024
D_S5 = 512
S5_CH = 16
S5_GROUPS = 32
S5_STATE = 64
D_POOL = 512
POOL_WINDOWS = (2, 4, 8, 16)
POOL_CH = 128
POOL_BUF = 15
D_FF = 2816
CONV_W = 3
EPS = 1e-6

SUBLANES = 8
LANES = 128
S5_SPLIT = 2
GROUPS_PER_SPLIT = S5_GROUPS // S5_SPLIT
U_PER_SPLIT = GROUPS_PER_SPLIT * S5_CH
ST_PER_SPLIT = GROUPS_PER_SPLIT * S5_STATE
GROUPS_PER_TILE = LANES // S5_STATE
FF_CHUNK = 256
N_FF_CHUNKS = D_FF // FF_CHUNK
SQRT_HALF = float(np.sqrt(0.5).astype(np.float32))

BF16 = jnp.bfloat16
F32 = jnp.float32


def _dot(a, b):
    return jnp.dot(a, b, preferred_element_type=F32)


def _rms(x, g):
    ms = jnp.mean(x * x, axis=-1, keepdims=True)
    return x * lax.rsqrt(ms + EPS) * g


def _gelu(x):
    return 0.5 * x * (1.0 + lax.erf(x * SQRT_HALF))


def _s5_post(y_lin, u, dskip_ref, wglu_ref):
    y = _gelu(y_lin + dskip_ref[...] * u)
    return y * jax.nn.sigmoid(_dot(y.astype(BF16), wglu_ref[...]))


def _pool_project(pooled_cols, wpool_ref, pscale_ref):
    zero = jnp.zeros((POOL_CH, POOL_CH), BF16)
    outs = []
    for pair in range(len(POOL_WINDOWS) // 2):
        w_a = wpool_ref[(2 * pair) * POOL_CH:(2 * pair + 1) * POOL_CH, :]
        w_b = wpool_ref[(2 * pair + 1) * POOL_CH:(2 * pair + 2) * POOL_CH, :]
        w_pair = jnp.concatenate([jnp.concatenate([w_a, zero], axis=1),
                                  jnp.concatenate([zero, w_b], axis=1)], axis=0)
        lhs = jnp.concatenate([pooled_cols[2 * pair], pooled_cols[2 * pair + 1]], axis=1)
        outs.append(_dot(lhs.astype(BF16), w_pair))
    return jnp.concatenate(outs, axis=1) * pscale_ref[...]


def _block_copies(hbm_ref, buf_ref, sem_ref, block, slot, tc, to_hbm):
    copies = []
    for n in range(SUBLANES):
        hbm = hbm_ref.at[n, pl.ds(block * tc, tc), :]
        vmem = buf_ref.at[slot, :, n, :]
        src, dst = (vmem, hbm) if to_hbm else (hbm, vmem)
        copies.append(pltpu.make_async_copy(src, dst, sem_ref.at[slot]))
    return copies


def _prompt_kernel(tc, n_steps,
                   x_hbm, g1_ref, win_ref, are_ref, aim_ref, bbd_ref, cbd_ref, dskip_ref,
                   wglu_ref, wpool_ref, pscale_ref, wout_ref, g2_ref, wup_ref, cw_ref,
                   cb_ref, wdown_ref, gf_ref,
                   y_hbm, sre_ref, sim_ref, pool_ref, conv_ref,
                   act_ref, xbuf, ybuf, sem_in, sem_out):
    rows = tc * SUBLANES
    step = pl.program_id(0)
    slot = step % 2

    @pl.when(step == 0)
    def _():
        sre_ref[...] = jnp.zeros_like(sre_ref)
        sim_ref[...] = jnp.zeros_like(sim_ref)
        pool_ref[...] = jnp.zeros_like(pool_ref)
        conv_ref[...] = jnp.zeros_like(conv_ref)
        for cp in _block_copies(x_hbm, xbuf, sem_in, 0, 0, tc, False):
            cp.start()

    @pl.when(step + 1 < n_steps)
    def _():
        for cp in _block_copies(x_hbm, xbuf, sem_in, step + 1, 1 - slot, tc, False):
            cp.start()

    @pl.when(step >= 2)
    def _():
        for cp in _block_copies(y_hbm, ybuf, sem_out, step - 2, slot, tc, True):
            cp.wait()

    for cp in _block_copies(x_hbm, xbuf, sem_in, step, slot, tc, False):
        cp.wait()

    x = xbuf[slot].reshape(rows, D_MODEL)
    hb = _rms(x, g1_ref[...]).astype(BF16)
    proj = _dot(hb, win_ref[...])
    u = proj[:, :D_S5]
    v = proj[:, D_S5:]
    ub = u.astype(BF16)

    y_parts = []
    for k in range(S5_SPLIT):
        ub_k = ub[:, k * U_PER_SPLIT:(k + 1) * U_PER_SPLIT]
        for j in range(ST_PER_SPLIT // LANES):
            tile = slice(2 * j * LANES, (2 * j + 2) * LANES)
            bu = _dot(ub_k, bbd_ref[k, :, tile])
            lanes = slice(k * ST_PER_SPLIT + j * LANES, k * ST_PER_SPLIT + (j + 1) * LANES)
            a_re = jnp.broadcast_to(are_ref[:, lanes], (SUBLANES, LANES))
            a_im = jnp.broadcast_to(aim_ref[:, lanes], (SUBLANES, LANES))
            h_re, h_im = sre_ref[:, lanes], sim_ref[:, lanes]
            h_rows = []
            for i in range(tc // 2):
                r0 = i * 2 * SUBLANES
                res, ims = [], []
                for s in range(2):
                    rs = slice(r0 + s * SUBLANES, r0 + (s + 1) * SUBLANES)
                    n_re = a_re * h_re - a_im * h_im + bu[rs, :LANES]
                    n_im = a_re * h_im + a_im * h_re + bu[rs, LANES:]
                    h_re, h_im = n_re, n_im
                    res.append(h_re)
                    ims.append(h_im)
                h_rows.append(jnp.concatenate(
                    [jnp.concatenate(res, axis=0), jnp.concatenate(ims, axis=0)], axis=1).astype(BF16))
            sre_ref[:, lanes] = h_re
            sim_ref[:, lanes] = h_im
            part = _dot(jnp.concatenate(h_rows, axis=0), cbd_ref[k, tile, :])
            y_k = part if j == 0 else y_k + part
        y_parts.append(y_k)
    y_s5 = _s5_post(jnp.concatenate(y_parts, axis=1), u, dskip_ref, wglu_ref)

    vfull = jnp.concatenate([pool_ref[...], v], axis=0)
    pool_ref[...] = vfull[rows:, :]
    t_idx = step * tc + (lax.broadcasted_iota(jnp.int32, (rows, POOL_CH), 0) >> 3)
    pooled = []
    for gi, w in enumerate(POOL_WINDOWS):
        s = vfull[:, gi * POOL_CH:(gi + 1) * POOL_CH]
        span = 1
        while span < w:
            sh = span * SUBLANES
            s = s[sh:, :] + s[:-sh, :]
            span *= 2
        first = (POOL_BUF - (w - 1)) * SUBLANES
        wsum = s[first:first + rows, :]
        cnt = jnp.minimum(t_idx + 1, w).astype(F32)
        pooled.append(wsum / cnt - v[:, gi * POOL_CH:(gi + 1) * POOL_CH])
    y_pool = _pool_project(pooled, wpool_ref, pscale_ref)

    x1 = x + _dot(y_s5.astype(BF16), wout_ref[:D_S5, :]) + _dot(y_pool.astype(BF16), wout_ref[D_S5:, :])

    h2b = _rms(x1, g2_ref[...]).astype(BF16)
    taps = (CONV_W - 1) * SUBLANES
    for j in range(N_FF_CHUNKS):
        convd = []
        for base in (0, D_FF):
            cols = slice(base + j * FF_CHUNK, base + (j + 1) * FF_CHUNK)
            hup = _dot(h2b, wup_ref[:, cols])
            full = jnp.concatenate([conv_ref[:, cols], hup], axis=0)
            conv_ref[:, cols] = hup[rows - taps:, :]
            c = cb_ref[:, cols]
            for kk in range(CONV_W):
                c = c + cw_ref[kk:kk + 1, cols] * full[kk * SUBLANES:kk * SUBLANES + rows, :]
            convd.append(c)
        act_ref[:, j * FF_CHUNK:(j + 1) * FF_CHUNK] = (_gelu(convd[0]) * convd[1]).astype(BF16)
    y = _rms(x1 + _dot(act_ref[...], wdown_ref[...]), gf_ref[...])

    ybuf[slot] = y.reshape(tc, SUBLANES, D_MODEL)
    for cp in _block_copies(y_hbm, ybuf, sem_out, step, slot, tc, True):
        cp.start()

    @pl.when(step == n_steps - 1)
    def _():
        if n_steps >= 2:
            for cp in _block_copies(y_hbm, ybuf, sem_out, step - 1, 1 - slot, tc, True):
                cp.wait()
        for cp in _block_copies(y_hbm, ybuf, sem_out, step, slot, tc, True):
            cp.wait()


N_BIG_WEIGHTS = 6
STAGE_SHAPES = ((6, 256, D_MODEL), (2, 256, D_S5), (2, 256, POOL_CH), (6, D_MODEL, FF_CHUNK))
STAGE_LOOKAHEAD = 5


def _sample_kernel(x_hbm, sre_in, sim_in, pool_hbm, conv_hbm,
                   g1_ref, are_ref, aim_ref, bbd_ref, cbd_ref, dskip_ref, pscale_ref, g2_ref,
                   cw_ref, cb_ref, gf_ref,
                   win_hbm, wglu_hbm, wpool_hbm, wout_hbm, wup_hbm, wdown_hbm,
                   y_hbm, sre_ref, sim_ref, pool_out, conv_out,
                   winb_out, wglub_out, wpoolb_out, woutb_out, wupb_out, wdownb_out,
                   win_ref, wglu_ref, wpool_ref, wout_ref, wup_ref, wdown_ref,
                   stage_model, stage_s5, stage_pool, stage_col,
                   xbuf, ybuf, poolbuf, convbuf, vbuf,
                   sem_stage, sem_state, sem_wout, sem_sout):
    in_copies = [pltpu.make_async_copy(x_hbm, xbuf, sem_state.at[0]),
                 pltpu.make_async_copy(pool_hbm, poolbuf, sem_state.at[1]),
                 pltpu.make_async_copy(conv_hbm, convbuf, sem_state.at[2])]
    for cp in in_copies:
        cp.start()

    stages = (stage_model, stage_s5, stage_pool, stage_col)
    sem_base = [sum(s.shape[0] for s in stages[:i]) for i in range(len(stages))]
    ring_pos = [0] * len(stages)
    queue = []
    out_copies = []
    live = {}

    def enqueue(src, sid, dst, after=None):
        queue.append((src, sid, ring_pos[sid] % stages[sid].shape[0], dst, after))
        ring_pos[sid] += 1

    def start_out(src, dst, sem):
        cp = pltpu.make_async_copy(src, dst, sem)
        cp.start()
        out_copies.append(cp)

    def enqueue_rows(w_hbm, wb_ref, sid, after_last):
        rows = stages[sid].shape[1]
        n_chunks = w_hbm.shape[0] // rows
        for c in range(n_chunks):
            part = pl.ds(c * rows, rows)
            enqueue(w_hbm.at[part, :], sid, wb_ref.at[part, :], after_last if c == n_chunks - 1 else None)

    def mixer():
        for cp in in_copies:
            cp.wait()
        start_out(wout_ref, woutb_out, sem_wout.at[3])
        x = xbuf[:, 0, :]
        hb = _rms(x, g1_ref[...]).astype(BF16)
        proj = _dot(hb, win_ref[...])
        u = proj[:, :D_S5]
        v = proj[:, D_S5:]
        ub = u.astype(BF16)

        y_parts = []
        for k in range(S5_SPLIT):
            bu = _dot(ub[:, k * U_PER_SPLIT:(k + 1) * U_PER_SPLIT], bbd_ref[k])
            h_tiles = []
            for j in range(ST_PER_SPLIT // LANES):
                lanes = slice(k * ST_PER_SPLIT + j * LANES, k * ST_PER_SPLIT + (j + 1) * LANES)
                a_re = are_ref[:, lanes]
                a_im = aim_ref[:, lanes]
                h_re0 = sre_in[lanes, :].T
                h_im0 = sim_in[lanes, :].T
                h_re = a_re * h_re0 - a_im * h_im0 + bu[:, 2 * j * LANES:(2 * j + 1) * LANES]
                h_im = a_re * h_im0 + a_im * h_re0 + bu[:, (2 * j + 1) * LANES:(2 * j + 2) * LANES]
                sre_ref[lanes, :] = h_re.T
                sim_ref[lanes, :] = h_im.T
                h_tiles += [h_re, h_im]
            hcat = jnp.concatenate(h_tiles, axis=1).astype(BF16)
            y_parts.append(_dot(hcat, cbd_ref[k]))
        y_s5 = _s5_post(jnp.concatenate(y_parts, axis=1), u, dskip_ref, wglu_ref)

        pooled = []
        for gi, w in enumerate(POOL_WINDOWS):
            lanes = slice(gi * POOL_CH, (gi + 1) * POOL_CH)
            vc = v[:, lanes]
            wsum = vc
            for back in range(1, w):
                wsum = wsum + poolbuf[POOL_BUF - back, :, lanes]
            pooled.append(wsum / float(w) - vc)
        y_pool = _pool_project(pooled, wpool_ref, pscale_ref)
        vbuf[...] = v
        start_out(poolbuf.at[pl.ds(1, POOL_BUF - 1)], pool_out.at[pl.ds(0, POOL_BUF - 1)], sem_sout.at[0])
        start_out(vbuf, pool_out.at[POOL_BUF - 1], sem_sout.at[1])

        x1 = (x + _dot(y_s5.astype(BF16), wout_ref[:D_S5, :])
              + _dot(y_pool.astype(BF16), wout_ref[D_S5:, :]))
        live["x1"] = x1
        live["h2b"] = _rms(x1, g2_ref[...]).astype(BF16)
        live["acc"] = None

    def ffn_chunk(j):
        rows_j = pl.ds(j * FF_CHUNK, FF_CHUNK)
        convd = []
        for base in (0, D_FF):
            cols = slice(base + j * FF_CHUNK, base + (j + 1) * FF_CHUNK)
            start_out(wup_ref.at[:, cols], wupb_out.at[:, cols], sem_wout.at[4])
            hup = _dot(live["h2b"], wup_ref[:, cols])
            older = convbuf[:, 0, cols]
            newer = convbuf[:, 1, cols]
            convbuf[:, 0, cols] = newer
            convbuf[:, 1, cols] = hup
            convd.append(cb_ref[:, cols] + cw_ref[0:1, cols] * older
                         + cw_ref[1:2, cols] * newer + cw_ref[2:3, cols] * hup)
        start_out(wdown_ref.at[rows_j, :], wdownb_out.at[rows_j, :], sem_wout.at[5])
        act = (_gelu(convd[0]) * convd[1]).astype(BF16)
        part = _dot(act, wdown_ref[j * FF_CHUNK:(j + 1) * FF_CHUNK, :])
        live["acc"] = part if live["acc"] is None else live["acc"] + part

    enqueue_rows(win_hbm, win_ref, 0, lambda: start_out(win_ref, winb_out, sem_wout.at[0]))
    enqueue_rows(wglu_hbm, wglu_ref, 1, lambda: start_out(wglu_ref, wglub_out, sem_wout.at[1]))
    enqueue_rows(wpool_hbm, wpool_ref, 2, lambda: start_out(wpool_ref, wpoolb_out, sem_wout.at[2]))
    enqueue_rows(wout_hbm, wout_ref, 0, mixer)
    for j in range(N_FF_CHUNKS):
        for base in (0, D_FF):
            cols = pl.ds(base + j * FF_CHUNK, FF_CHUNK)
            enqueue(wup_hbm.at[:, cols], 3, wup_ref.at[:, cols])
        rows_j = pl.ds(j * FF_CHUNK, FF_CHUNK)
        enqueue(wdown_hbm.at[rows_j, :], 0, wdown_ref.at[rows_j, :], functools.partial(ffn_chunk, j))

    def read(entry):
        src, sid, slot, _, _ = entry
        return pltpu.make_async_copy(src, stages[sid].at[slot], sem_stage.at[sem_base[sid] + slot])

    started = 0
    for i, entry in enumerate(queue):
        while started < min(len(queue), i + 1 + STAGE_LOOKAHEAD):
            read(queue[started]).start()
            started += 1
        read(entry).wait()
        _, sid, slot, dst, after = entry
        dst[...] = stages[sid][slot].astype(BF16)
        if after is not None:
            after()

    ybuf[:, 0, :] = _rms(live["x1"] + live["acc"], gf_ref[...])
    start_out(convbuf, conv_out, sem_sout.at[2])
    start_out(ybuf, y_hbm, sem_sout.at[3])
    for cp in out_copies:
        cp.wait()


def _s5_tables(a_re, a_im, log_dt, b_re, b_im, c_re, c_im):
    dt = jnp.exp(log_dt)[:, None]
    mag = jnp.exp(dt * a_re)
    abar_re = mag * jnp.cos(dt * a_im)
    abar_im = mag * jnp.sin(dt * a_im)
    nr, ni = abar_re - 1.0, abar_im
    den = a_re * a_re + a_im * a_im
    f_re = ((nr * a_re + ni * a_im) / den)[:, :, None]
    f_im = ((ni * a_re - nr * a_im) / den)[:, :, None]
    bbar_re = f_re * b_re - f_im * b_im
    bbar_im = f_re * b_im + f_im * b_re
    n_tiles = GROUPS_PER_SPLIT // GROUPS_PER_TILE
    state_cols = np.arange(2 * ST_PER_SPLIT)
    col_group = (state_cols // (2 * LANES)) * GROUPS_PER_TILE + (state_cols % LANES) // S5_STATE
    diag = jnp.asarray(np.arange(U_PER_SPLIT)[:, None] // S5_CH == col_group[None, :])

    b6 = jnp.stack([bbar_re, bbar_im]).reshape(2, S5_SPLIT, n_tiles, GROUPS_PER_TILE, S5_STATE, S5_CH)
    row = jnp.transpose(b6, (1, 5, 2, 0, 3, 4)).reshape(S5_SPLIT, 1, S5_CH, 2 * ST_PER_SPLIT)
    rep = jnp.broadcast_to(row, (S5_SPLIT, GROUPS_PER_SPLIT, S5_CH, 2 * ST_PER_SPLIT))
    bbd = jnp.where(diag, rep.reshape(S5_SPLIT, U_PER_SPLIT, 2 * ST_PER_SPLIT), 0.0)

    c6 = jnp.stack([c_re, -c_im]).reshape(2, S5_SPLIT, n_tiles, GROUPS_PER_TILE, S5_CH, S5_STATE)
    col = jnp.transpose(c6, (1, 2, 0, 3, 5, 4)).reshape(S5_SPLIT, 2 * ST_PER_SPLIT, S5_CH)
    cbd = jnp.where(diag.T, jnp.tile(col, (1, 1, GROUPS_PER_SPLIT)), 0.0)
    return abar_re.reshape(1, -1), abar_im.reshape(1, -1), bbd.astype(BF16), cbd.astype(BF16)


def _vmem_spec():
    return pl.BlockSpec(memory_space=pltpu.VMEM)


PROMPT_TC = 128
VMEM_LIMIT_BYTES = 60 * 1024 * 1024


def kernel(x_prompt, x_sample, state_s5_re, state_s5_im, state_pool, state_ffn_conv, norm_mix_g, w_in, s5_a_re, s5_a_im, s5_log_dt, s5_b_re, s5_b_im, s5_c_re, s5_c_im, s5_d, s5_w_glu, pool_w, pool_scale, w_out, norm_ffn_g, ffn_w_up, ffn_conv_w, ffn_conv_b, ffn_w_down, norm_final_g):
    nb, seq, _ = x_prompt.shape
    ns = x_sample.shape[0]
    assert nb == SUBLANES and seq % PROMPT_TC == 0 and x_sample.shape[1] == 1
    assert norm_mix_g.shape[0] == 1, "single layer"

    a_re, a_im, bbd, cbd = _s5_tables(s5_a_re[0], s5_a_im[0], s5_log_dt[0], s5_b_re[0],
                                      s5_b_im[0], s5_c_re[0], s5_c_im[0])
    n_states = S5_GROUPS * S5_STATE
    cparams = dict(vmem_limit_bytes=VMEM_LIMIT_BYTES)
    g1 = norm_mix_g[0].reshape(1, D_MODEL)
    dskip = s5_d[0].reshape(1, D_S5)
    pscale = pool_scale[0].reshape(1, D_POOL)
    g2 = norm_ffn_g[0].reshape(1, D_MODEL)
    cw = ffn_conv_w[0]
    cb = ffn_conv_b[0].reshape(1, 2 * D_FF)
    gf = norm_final_g.reshape(1, D_MODEL)

    big_f32 = (w_in[0], s5_w_glu[0], pool_w[0].reshape(len(POOL_WINDOWS) * POOL_CH, POOL_CH),
               w_out[0], ffn_w_up[0], ffn_w_down[0])
    small = (g1, a_re, a_im, bbd, cbd, dskip, pscale, g2, cw, cb, gf)
    any_spec = pl.BlockSpec(memory_space=pl.ANY)
    pool_in = jnp.transpose(state_pool[0], (1, 0, 2))

    def state_major(s):
        return jnp.transpose(s, (1, 2, 0)).reshape(n_states, ns)

    def sequence_major(s):
        return jnp.transpose(s.reshape(S5_GROUPS, S5_STATE, ns), (2, 0, 1))[None]
    sample_out = pl.pallas_call(
        _sample_kernel,
        in_specs=[any_spec] + [_vmem_spec()] * 2 + [any_spec] * 2 + [_vmem_spec()] * len(small)
                 + [any_spec] * N_BIG_WEIGHTS,
        out_specs=[any_spec] + [_vmem_spec()] * 2 + [any_spec] * (2 + N_BIG_WEIGHTS),
        out_shape=[
            jax.ShapeDtypeStruct((ns, 1, D_MODEL), F32),
            jax.ShapeDtypeStruct((n_states, ns), F32),
            jax.ShapeDtypeStruct((n_states, ns), F32),
            jax.ShapeDtypeStruct((POOL_BUF, ns, D_POOL), F32),
            jax.ShapeDtypeStruct((ns, CONV_W - 1, 2 * D_FF), F32),
        ] + [jax.ShapeDtypeStruct(w.shape, BF16) for w in big_f32],
        scratch_shapes=[pltpu.VMEM(w.shape, BF16) for w in big_f32]
        + [pltpu.VMEM(shape, F32) for shape in STAGE_SHAPES] + [
            pltpu.VMEM((ns, 1, D_MODEL), F32),
            pltpu.VMEM((ns, 1, D_MODEL), F32),
            pltpu.VMEM((POOL_BUF, ns, D_POOL), F32),
            pltpu.VMEM((ns, CONV_W - 1, 2 * D_FF), F32),
            pltpu.VMEM((ns, D_POOL), F32),
            pltpu.SemaphoreType.DMA((sum(shape[0] for shape in STAGE_SHAPES),)),
            pltpu.SemaphoreType.DMA((3,)),
            pltpu.SemaphoreType.DMA((N_BIG_WEIGHTS,)),
            pltpu.SemaphoreType.DMA((4,)),
        ],
        compiler_params=pltpu.CompilerParams(**cparams),
        name="sample_layer",
    )(x_sample, state_major(state_s5_re[0]), state_major(state_s5_im[0]),
      pool_in, state_ffn_conv[0], *small, *big_f32)
    ys, s_re, s_im, s_pool, s_conv = sample_out[:5]
    win_b, wglu_b, wpool_b, wout_b, wup_b, wdown_b = sample_out[5:]
    weights = (g1, win_b, a_re, a_im, bbd, cbd, dskip, wglu_b, wpool_b, pscale, wout_b, g2,
               wup_b, cw, cb, wdown_b, gf)

    tc = PROMPT_TC
    rows = tc * SUBLANES
    const = lambda i: (0, 0)
    y_prompt, p_re, p_im, p_pool, p_conv = pl.pallas_call(
        functools.partial(_prompt_kernel, tc, seq // tc),
        grid=(seq // tc,),
        in_specs=[pl.BlockSpec(memory_space=pl.ANY)] + [_vmem_spec()] * len(weights),
        out_specs=[
            pl.BlockSpec(memory_space=pl.ANY),
            pl.BlockSpec((SUBLANES, n_states), const),
            pl.BlockSpec((SUBLANES, n_states), const),
            pl.BlockSpec((POOL_BUF * SUBLANES, D_POOL), const),
            pl.BlockSpec(((CONV_W - 1) * SUBLANES, 2 * D_FF), const),
        ],
        out_shape=[
            jax.ShapeDtypeStruct((nb, seq, D_MODEL), F32),
            jax.ShapeDtypeStruct((SUBLANES, n_states), F32),
            jax.ShapeDtypeStruct((SUBLANES, n_states), F32),
            jax.ShapeDtypeStruct((POOL_BUF * SUBLANES, D_POOL), F32),
            jax.ShapeDtypeStruct(((CONV_W - 1) * SUBLANES, 2 * D_FF), F32),
        ],
        scratch_shapes=[
            pltpu.VMEM((rows, D_FF), BF16),
            pltpu.VMEM((2, tc, SUBLANES, D_MODEL), F32),
            pltpu.VMEM((2, tc, SUBLANES, D_MODEL), F32),
            pltpu.SemaphoreType.DMA((2,)),
            pltpu.SemaphoreType.DMA((2,)),
        ],
        compiler_params=pltpu.CompilerParams(dimension_semantics=("arbitrary",), **cparams),
        name="prompt_layer",
    )(x_prompt, *weights)
    new_pool_p = jnp.transpose(p_pool.reshape(POOL_BUF, nb, D_POOL), (1, 0, 2))[None]
    new_conv_p = jnp.transpose(p_conv.reshape(CONV_W - 1, nb, 2 * D_FF), (1, 0, 2))[None]
    new_re_p = p_re.reshape(1, nb, S5_GROUPS, S5_STATE)
    new_im_p = p_im.reshape(1, nb, S5_GROUPS, S5_STATE)

    return (y_prompt, ys, new_re_p, new_im_p, new_pool_p, new_conv_p,
            sequence_major(s_re), sequence_major(s_im),
            jnp.transpose(s_pool, (1, 0, 2))[None], s_conv[None])
```

```python
import functools

import numpy as np
import jax
import jax.numpy as jnp
from jax import lax
from jax.experimental import pallas as pl
from jax.experimental.pallas import tpu as pltpu

D_MODEL = 1024
D_S5 = 512
S5_CH = 16
S5_GROUPS = 32
S5_STATE = 64
D_POOL = 512
POOL_WINDOWS = (2, 4, 8, 16)
POOL_CH = 128
POOL_BUF = 15
D_FF = 2816
CONV_W = 3
EPS = 1e-6

SUBLANES = 8
LANES = 128
S5_SPLIT = 2
GROUPS_PER_SPLIT = S5_GROUPS // S5_SPLIT
U_PER_SPLIT = GROUPS_PER_SPLIT * S5_CH
ST_PER_SPLIT = GROUPS_PER_SPLIT * S5_STATE
GROUPS_PER_TILE = LANES // S5_STATE
FF_CHUNK = 256
N_FF_CHUNKS = D_FF // FF_CHUNK
SQRT_HALF = float(np.sqrt(0.5).astype(np.float32))

BF16 = jnp.bfloat16
F32 = jnp.float32


def _dot(a, b):
    return jnp.dot(a, b, preferred_element_type=F32)


def _rms(x, g):
    ms = jnp.mean(x * x, axis=-1, keepdims=True)
    return x * lax.rsqrt(ms + EPS) * g


def _gelu(x):
    return 0.5 * x * (1.0 + lax.erf(x * SQRT_HALF))


def _s5_post(y_lin, u, dskip_ref, wglu_ref):
    y = _gelu(y_lin + dskip_ref[...] * u)
    return y * jax.nn.sigmoid(_dot(y.astype(BF16), wglu_ref[...]))


def _pool_project(pooled_cols, wpool_ref, pscale_ref):
    zero = jnp.zeros((POOL_CH, POOL_CH), BF16)
    outs = []
    for pair in range(len(POOL_WINDOWS) // 2):
        w_a = wpool_ref[(2 * pair) * POOL_CH:(2 * pair + 1) * POOL_CH, :]
        w_b = wpool_ref[(2 * pair + 1) * POOL_CH:(2 * pair + 2) * POOL_CH, :]
        w_pair = jnp.concatenate([jnp.concatenate([w_a, zero], axis=1),
                                  jnp.concatenate([zero, w_b], axis=1)], axis=0)
        lhs = jnp.concatenate([pooled_cols[2 * pair], pooled_cols[2 * pair + 1]], axis=1)
        outs.append(_dot(lhs.astype(BF16), w_pair))
    return jnp.concatenate(outs, axis=1) * pscale_ref[...]


def _block_copies(hbm_ref, buf_ref, sem_ref, block, slot, tc, to_hbm):
    copies = []
    for n in range(SUBLANES):
        hbm = hbm_ref.at[n, pl.ds(block * tc, tc), :]
        vmem = buf_ref.at[slot, :, n, :]
        src, dst = (vmem, hbm) if to_hbm else (hbm, vmem)
        copies.append(pltpu.make_async_copy(src, dst, sem_ref.at[slot]))
    return copies


def _prompt_kernel(tc, n_steps,
                   x_hbm, g1_ref, win_ref, are_ref, aim_ref, bbd_ref, cbd_ref, dskip_ref,
                   wglu_ref, wpool_ref, pscale_ref, wout_ref, g2_ref, wup_ref, cw_ref,
                   cb_ref, wdown_ref, gf_ref,
                   y_hbm, sre_ref, sim_ref, pool_ref, conv_ref,
                   act_ref, xbuf, ybuf, sem_in, sem_out):
    rows = tc * SUBLANES
    step = pl.program_id(0)
    slot = step % 2

    @pl.when(step == 0)
    def _():
        sre_ref[...] = jnp.zeros_like(sre_ref)
        sim_ref[...] = jnp.zeros_like(sim_ref)
        pool_ref[...] = jnp.zeros_like(pool_ref)
        conv_ref[...] = jnp.zeros_like(conv_ref)
        for cp in _block_copies(x_hbm, xbuf, sem_in, 0, 0, tc, False):
            cp.start()

    @pl.when(step + 1 < n_steps)
    def _():
        for cp in _block_copies(x_hbm, xbuf, sem_in, step + 1, 1 - slot, tc, False):
            cp.start()

    @pl.when(step >= 2)
    def _():
        for cp in _block_copies(y_hbm, ybuf, sem_out, step - 2, slot, tc, True):
            cp.wait()

    for cp in _block_copies(x_hbm, xbuf, sem_in, step, slot, tc, False):
        cp.wait()

    x = xbuf[slot].reshape(rows, D_MODEL)
    half = rows // 2
    proj = jnp.concatenate(
        [_dot(_rms(x[r, :], g1_ref[...]).astype(BF16), win_ref[...])
         for r in (slice(0, half), slice(half, rows))], axis=0)
    u = proj[:, :D_S5]
    v = proj[:, D_S5:]
    ub = u.astype(BF16)

    y_parts = []
    for k in range(S5_SPLIT):
        ub_k = ub[:, k * U_PER_SPLIT:(k + 1) * U_PER_SPLIT]
        for j in range(ST_PER_SPLIT // LANES):
            tile = slice(2 * j * LANES, (2 * j + 2) * LANES)
            bu = _dot(ub_k, bbd_ref[k, :, tile])
            lanes = slice(k * ST_PER_SPLIT + j * LANES, k * ST_PER_SPLIT + (j + 1) * LANES)
            a_re = jnp.broadcast_to(are_ref[:, lanes], (SUBLANES, LANES))
            a_im = jnp.broadcast_to(aim_ref[:, lanes], (SUBLANES, LANES))
            h_re, h_im = sre_ref[:, lanes], sim_ref[:, lanes]
            h_rows = []
            for i in range(tc // 2):
                r0 = i * 2 * SUBLANES
                res, ims = [], []
                for s in range(2):
                    rs = slice(r0 + s * SUBLANES, r0 + (s + 1) * SUBLANES)
                    n_re = a_re * h_re - a_im * h_im + bu[rs, :LANES]
                    n_im = a_re * h_im + a_im * h_re + bu[rs, LANES:]
                    h_re, h_im = n_re, n_im
                    res.append(h_re)
                    ims.append(h_im)
                h_rows.append(jnp.concatenate(
                    [jnp.concatenate(res, axis=0), jnp.concatenate(ims, axis=0)], axis=1).astype(BF16))
            sre_ref[:, lanes] = h_re
            sim_ref[:, lanes] = h_im
            part = _dot(jnp.concatenate(h_rows, axis=0), cbd_ref[k, tile, :])
            y_k = part if j == 0 else y_k + part
        y_parts.append(y_k)
    y_s5 = _s5_post(jnp.concatenate(y_parts, axis=1), u, dskip_ref, wglu_ref)

    vfull = jnp.concatenate([pool_ref[...], v], axis=0)
    pool_ref[...] = vfull[rows:, :]
    t_idx = step * tc + (lax.broadcasted_iota(jnp.int32, (rows, POOL_CH), 0) >> 3)
    pooled = []
    for gi, w in enumerate(POOL_WINDOWS):
        s = vfull[:, gi * POOL_CH:(gi + 1) * POOL_CH]
        span = 1
        while span < w:
            sh = span * SUBLANES
            s = s[sh:, :] + s[:-sh, :]
            span *= 2
        first = (POOL_BUF - (w - 1)) * SUBLANES
        wsum = s[first:first + rows, :]
        cnt = jnp.minimum(t_idx + 1, w).astype(F32)
        pooled.append(wsum / cnt - v[:, gi * POOL_CH:(gi + 1) * POOL_CH])
    y_pool = _pool_project(pooled, wpool_ref, pscale_ref)

    mixed = jnp.concatenate([y_s5.astype(BF16), y_pool.astype(BF16)], axis=1)
    half = rows // 2
    x1_halves = [x[r, :] + _dot(mixed[r, :], wout_ref[...]) for r in (slice(0, half), slice(half, rows))]
    h2b = jnp.concatenate([_rms(xh, g2_ref[...]).astype(BF16) for xh in x1_halves], axis=0)
    x1 = jnp.concatenate(x1_halves, axis=0)

    taps = (CONV_W - 1) * SUBLANES
    for j in range(N_FF_CHUNKS):
        convd = []
        for base in (0, D_FF):
            cols = slice(base + j * FF_CHUNK, base + (j + 1) * FF_CHUNK)
            hup = _dot(h2b, wup_ref[:, cols])
            full = jnp.concatenate([conv_ref[:, cols], hup], axis=0)
            conv_ref[:, cols] = hup[rows - taps:, :]
            c = cb_ref[:, cols]
            for kk in range(CONV_W):
                c = c + cw_ref[kk:kk + 1, cols] * full[kk * SUBLANES:kk * SUBLANES + rows, :]
            convd.append(c)
        act_ref[:, j * FF_CHUNK:(j + 1) * FF_CHUNK] = (_gelu(convd[0]) * convd[1]).astype(BF16)
    y = jnp.concatenate(
        [_rms(x1_halves[i] + _dot(act_ref[r, :], wdown_ref[...]), gf_ref[...])
         for i, r in enumerate((slice(0, half), slice(half, rows)))], axis=0)

    ybuf[slot] = y.reshape(tc, SUBLANES, D_MODEL)
    for cp in _block_copies(y_hbm, ybuf, sem_out, step, slot, tc, True):
        cp.start()

    @pl.when(step == n_steps - 1)
    def _():
        if n_steps >= 2:
            for cp in _block_copies(y_hbm, ybuf, sem_out, step - 1, 1 - slot, tc, True):
                cp.wait()
        for cp in _block_copies(y_hbm, ybuf, sem_out, step, slot, tc, True):
            cp.wait()


N_BIG_WEIGHTS = 6
STAGE_SHAPES = ((6, 256, D_MODEL), (2, 256, D_S5), (2, 256, POOL_CH), (6, D_MODEL, FF_CHUNK))
STAGE_LOOKAHEAD = 5


def _sample_kernel(x_hbm, sre_in, sim_in, pool_hbm, conv_hbm,
                   g1_ref, are_ref, aim_ref, bbd_ref, cbd_ref, dskip_ref, pscale_ref, g2_ref,
                   cw_ref, cb_ref, gf_ref,
                   win_hbm, wglu_hbm, wpool_hbm, wout_hbm, wup_hbm, wdown_hbm,
                   y_hbm, sre_ref, sim_ref, pool_out, conv_out,
                   winb_out, wglub_out, wpoolb_out, woutb_out, wupb_out, wdownb_out,
                   win_ref, wglu_ref, wpool_ref, wout_ref, wup_ref, wdown_ref,
                   stage_model, stage_s5, stage_pool, stage_col,
                   xbuf, ybuf, poolbuf, convbuf, vbuf,
                   sem_stage, sem_state, sem_wout, sem_sout):
    in_copies = [pltpu.make_async_copy(x_hbm, xbuf, sem_state.at[0]),
                 pltpu.make_async_copy(pool_hbm, poolbuf, sem_state.at[1]),
                 pltpu.make_async_copy(conv_hbm, convbuf, sem_state.at[2])]
    for cp in in_copies:
        cp.start()

    stages = (stage_model, stage_s5, stage_pool, stage_col)
    sem_base = [sum(s.shape[0] for s in stages[:i]) for i in range(len(stages))]
    ring_pos = [0] * len(stages)
    queue = []
    out_copies = []
    live = {}

    def enqueue(src, sid, dst, after=None):
        queue.append((src, sid, ring_pos[sid] % stages[sid].shape[0], dst, after))
        ring_pos[sid] += 1

    def start_out(src, dst, sem):
        cp = pltpu.make_async_copy(src, dst, sem)
        cp.start()
        out_copies.append(cp)

    def enqueue_rows(w_hbm, wb_ref, sid, after_last):
        rows = stages[sid].shape[1]
        n_chunks = w_hbm.shape[0] // rows
        for c in range(n_chunks):
            part = pl.ds(c * rows, rows)
            enqueue(w_hbm.at[part, :], sid, wb_ref.at[part, :], after_last if c == n_chunks - 1 else None)

    def mixer():
        for cp in in_copies:
            cp.wait()
        start_out(wout_ref, woutb_out, sem_wout.at[3])
        x = xbuf[:, 0, :]
        hb = _rms(x, g1_ref[...]).astype(BF16)
        proj = _dot(hb, win_ref[...])
        u = proj[:, :D_S5]
        v = proj[:, D_S5:]
        ub = u.astype(BF16)

        y_parts = []
        for k in range(S5_SPLIT):
            bu = _dot(ub[:, k * U_PER_SPLIT:(k + 1) * U_PER_SPLIT], bbd_ref[k])
            h_tiles = []
            for j in range(ST_PER_SPLIT // LANES):
                lanes = slice(k * ST_PER_SPLIT + j * LANES, k * ST_PER_SPLIT + (j + 1) * LANES)
                a_re = are_ref[:, lanes]
                a_im = aim_ref[:, lanes]
                h_re0 = sre_in[lanes, :].T
                h_im0 = sim_in[lanes, :].T
                h_re = a_re * h_re0 - a_im * h_im0 + bu[:, 2 * j * LANES:(2 * j + 1) * LANES]
                h_im = a_re * h_im0 + a_im * h_re0 + bu[:, (2 * j + 1) * LANES:(2 * j + 2) * LANES]
                sre_ref[lanes, :] = h_re.T
                sim_ref[lanes, :] = h_im.T
                h_tiles += [h_re, h_im]
            hcat = jnp.concatenate(h_tiles, axis=1).astype(BF16)
            y_parts.append(_dot(hcat, cbd_ref[k]))
        y_s5 = _s5_post(jnp.concatenate(y_parts, axis=1), u, dskip_ref, wglu_ref)

        pooled = []
        for gi, w in enumerate(POOL_WINDOWS):
            lanes = slice(gi * POOL_CH, (gi + 1) * POOL_CH)
            vc = v[:, lanes]
            wsum = vc
            for back in range(1, w):
                wsum = wsum + poolbuf[POOL_BUF - back, :, lanes]
            pooled.append(wsum / float(w) - vc)
        y_pool = _pool_project(pooled, wpool_ref, pscale_ref)
        vbuf[...] = v
        start_out(poolbuf.at[pl.ds(1, POOL_BUF - 1)], pool_out.at[pl.ds(0, POOL_BUF - 1)], sem_sout.at[0])
        start_out(vbuf, pool_out.at[POOL_BUF - 1], sem_sout.at[1])

        x1 = (x + _dot(y_s5.astype(BF16), wout_ref[:D_S5, :])
              + _dot(y_pool.astype(BF16), wout_ref[D_S5:, :]))
        live["x1"] = x1
        live["h2b"] = _rms(x1, g2_ref[...]).astype(BF16)
        live["acc"] = None

    def ffn_chunk(j):
        rows_j = pl.ds(j * FF_CHUNK, FF_CHUNK)
        convd = []
        for base in (0, D_FF):
            cols = slice(base + j * FF_CHUNK, base + (j + 1) * FF_CHUNK)
            start_out(wup_ref.at[:, cols], wupb_out.at[:, cols], sem_wout.at[4])
            hup = _dot(live["h2b"], wup_ref[:, cols])
            older = convbuf[:, 0, cols]
            newer = convbuf[:, 1, cols]
            convbuf[:, 0, cols] = newer
            convbuf[:, 1, cols] = hup
            convd.append(cb_ref[:, cols] + cw_ref[0:1, cols] * older
                         + cw_ref[1:2, cols] * newer + cw_ref[2:3, cols] * hup)
        start_out(wdown_ref.at[rows_j, :], wdownb_out.at[rows_j, :], sem_wout.at[5])
        act = (_gelu(convd[0]) * convd[1]).astype(BF16)
        part = _dot(act, wdown_ref[j * FF_CHUNK:(j + 1) * FF_CHUNK, :])
        live["acc"] = part if live["acc"] is None else live["acc"] + part

    enqueue_rows(win_hbm, win_ref, 0, lambda: start_out(win_ref, winb_out, sem_wout.at[0]))
    enqueue_rows(wglu_hbm, wglu_ref, 1, lambda: start_out(wglu_ref, wglub_out, sem_wout.at[1]))
    enqueue_rows(wpool_hbm, wpool_ref, 2, lambda: start_out(wpool_ref, wpoolb_out, sem_wout.at[2]))
    enqueue_rows(wout_hbm, wout_ref, 0, mixer)
    for j in range(N_FF_CHUNKS):
        for base in (0, D_FF):
            cols = pl.ds(base + j * FF_CHUNK, FF_CHUNK)
            enqueue(wup_hbm.at[:, cols], 3, wup_ref.at[:, cols])
        rows_j = pl.ds(j * FF_CHUNK, FF_CHUNK)
        enqueue(wdown_hbm.at[rows_j, :], 0, wdown_ref.at[rows_j, :], functools.partial(ffn_chunk, j))

    def read(entry):
        src, sid, slot, _, _ = entry
        return pltpu.make_async_copy(src, stages[sid].at[slot], sem_stage.at[sem_base[sid] + slot])

    started = 0
    for i, entry in enumerate(queue):
        while started < min(len(queue), i + 1 + STAGE_LOOKAHEAD):
            read(queue[started]).start()
            started += 1
        read(entry).wait()
        _, sid, slot, dst, after = entry
        dst[...] = stages[sid][slot].astype(BF16)
        if after is not None:
            after()

    ybuf[:, 0, :] = _rms(live["x1"] + live["acc"], gf_ref[...])
    start_out(convbuf, conv_out, sem_sout.at[2])
    start_out(ybuf, y_hbm, sem_sout.at[3])
    for cp in out_copies:
        cp.wait()


def _s5_tables(a_re, a_im, log_dt, b_re, b_im, c_re, c_im):
    dt = jnp.exp(log_dt)[:, None]
    mag = jnp.exp(dt * a_re)
    abar_re = mag * jnp.cos(dt * a_im)
    abar_im = mag * jnp.sin(dt * a_im)
    nr, ni = abar_re - 1.0, abar_im
    den = a_re * a_re + a_im * a_im
    f_re = ((nr * a_re + ni * a_im) / den)[:, :, None]
    f_im = ((ni * a_re - nr * a_im) / den)[:, :, None]
    bbar_re = f_re * b_re - f_im * b_im
    bbar_im = f_re * b_im + f_im * b_re
    n_tiles = GROUPS_PER_SPLIT // GROUPS_PER_TILE
    state_cols = np.arange(2 * ST_PER_SPLIT)
    col_group = (state_cols // (2 * LANES)) * GROUPS_PER_TILE + (state_cols % LANES) // S5_STATE
    diag = jnp.asarray(np.arange(U_PER_SPLIT)[:, None] // S5_CH == col_group[None, :])

    b6 = jnp.stack([bbar_re, bbar_im]).reshape(2, S5_SPLIT, n_tiles, GROUPS_PER_TILE, S5_STATE, S5_CH)
    row = jnp.transpose(b6, (1, 5, 2, 0, 3, 4)).reshape(S5_SPLIT, 1, S5_CH, 2 * ST_PER_SPLIT)
    rep = jnp.broadcast_to(row, (S5_SPLIT, GROUPS_PER_SPLIT, S5_CH, 2 * ST_PER_SPLIT))
    bbd = jnp.where(diag, rep.reshape(S5_SPLIT, U_PER_SPLIT, 2 * ST_PER_SPLIT), 0.0)

    c6 = jnp.stack([c_re, -c_im]).reshape(2, S5_SPLIT, n_tiles, GROUPS_PER_TILE, S5_CH, S5_STATE)
    col = jnp.transpose(c6, (1, 2, 0, 3, 5, 4)).reshape(S5_SPLIT, 2 * ST_PER_SPLIT, S5_CH)
    cbd = jnp.where(diag.T, jnp.tile(col, (1, 1, GROUPS_PER_SPLIT)), 0.0)
    return abar_re.reshape(1, -1), abar_im.reshape(1, -1), bbd.astype(BF16), cbd.astype(BF16)


def _vmem_spec():
    return pl.BlockSpec(memory_space=pltpu.VMEM)


PROMPT_TC = 64
VMEM_LIMIT_BYTES = 60 * 1024 * 1024


def kernel(x_prompt, x_sample, state_s5_re, state_s5_im, state_pool, state_ffn_conv, norm_mix_g, w_in, s5_a_re, s5_a_im, s5_log_dt, s5_b_re, s5_b_im, s5_c_re, s5_c_im, s5_d, s5_w_glu, pool_w, pool_scale, w_out, norm_ffn_g, ffn_w_up, ffn_conv_w, ffn_conv_b, ffn_w_down, norm_final_g):
    nb, seq, _ = x_prompt.shape
    ns = x_sample.shape[0]
    assert nb == SUBLANES and seq % PROMPT_TC == 0 and x_sample.shape[1] == 1
    assert norm_mix_g.shape[0] == 1, "single layer"

    a_re, a_im, bbd, cbd = _s5_tables(s5_a_re[0], s5_a_im[0], s5_log_dt[0], s5_b_re[0],
                                      s5_b_im[0], s5_c_re[0], s5_c_im[0])
    n_states = S5_GROUPS * S5_STATE
    cparams = dict(vmem_limit_bytes=VMEM_LIMIT_BYTES)
    g1 = norm_mix_g[0].reshape(1, D_MODEL)
    dskip = s5_d[0].reshape(1, D_S5)
    pscale = pool_scale[0].reshape(1, D_POOL)
    g2 = norm_ffn_g[0].reshape(1, D_MODEL)
    cw = ffn_conv_w[0]
    cb = ffn_conv_b[0].reshape(1, 2 * D_FF)
    gf = norm_final_g.reshape(1, D_MODEL)

    big_f32 = (w_in[0], s5_w_glu[0], pool_w[0].reshape(len(POOL_WINDOWS) * POOL_CH, POOL_CH),
               w_out[0], ffn_w_up[0], ffn_w_down[0])
    small = (g1, a_re, a_im, bbd, cbd, dskip, pscale, g2, cw, cb, gf)
    any_spec = pl.BlockSpec(memory_space=pl.ANY)
    pool_in = jnp.transpose(state_pool[0], (1, 0, 2))

    def state_major(s):
        return jnp.transpose(s, (1, 2, 0)).reshape(n_states, ns)

    def sequence_major(s):
        return jnp.transpose(s.reshape(S5_GROUPS, S5_STATE, ns), (2, 0, 1))[None]
    sample_out = pl.pallas_call(
        _sample_kernel,
        in_specs=[any_spec] + [_vmem_spec()] * 2 + [any_spec] * 2 + [_vmem_spec()] * len(small)
                 + [any_spec] * N_BIG_WEIGHTS,
        out_specs=[any_spec] + [_vmem_spec()] * 2 + [any_spec] * (2 + N_BIG_WEIGHTS),
        out_shape=[
            jax.ShapeDtypeStruct((ns, 1, D_MODEL), F32),
            jax.ShapeDtypeStruct((n_states, ns), F32),
            jax.ShapeDtypeStruct((n_states, ns), F32),
            jax.ShapeDtypeStruct((POOL_BUF, ns, D_POOL), F32),
            jax.ShapeDtypeStruct((ns, CONV_W - 1, 2 * D_FF), F32),
        ] + [jax.ShapeDtypeStruct(w.shape, BF16) for w in big_f32],
        scratch_shapes=[pltpu.VMEM(w.shape, BF16) for w in big_f32]
        + [pltpu.VMEM(shape, F32) for shape in STAGE_SHAPES] + [
            pltpu.VMEM((ns, 1, D_MODEL), F32),
            pltpu.VMEM((ns, 1, D_MODEL), F32),
            pltpu.VMEM((POOL_BUF, ns, D_POOL), F32),
            pltpu.VMEM((ns, CONV_W - 1, 2 * D_FF), F32),
            pltpu.VMEM((ns, D_POOL), F32),
            pltpu.SemaphoreType.DMA((sum(shape[0] for shape in STAGE_SHAPES),)),
            pltpu.SemaphoreType.DMA((3,)),
            pltpu.SemaphoreType.DMA((N_BIG_WEIGHTS,)),
            pltpu.SemaphoreType.DMA((4,)),
        ],
        compiler_params=pltpu.CompilerParams(**cparams),
        name="sample_layer",
    )(x_sample, state_major(state_s5_re[0]), state_major(state_s5_im[0]),
      pool_in, state_ffn_conv[0], *small, *big_f32)
    ys, s_re, s_im, s_pool, s_conv = sample_out[:5]
    win_b, wglu_b, wpool_b, wout_b, wup_b, wdown_b = sample_out[5:]
    weights = (g1, win_b, a_re, a_im, bbd, cbd, dskip, wglu_b, wpool_b, pscale, wout_b, g2,
               wup_b, cw, cb, wdown_b, gf)

    tc = PROMPT_TC
    rows = tc * SUBLANES
    const = lambda i: (0, 0)
    y_prompt, p_re, p_im, p_pool, p_conv = pl.pallas_call(
        functools.partial(_prompt_kernel, tc, seq // tc),
        grid=(seq // tc,),
        in_specs=[pl.BlockSpec(memory_space=pl.ANY)] + [_vmem_spec()] * len(weights),
        out_specs=[
            pl.BlockSpec(memory_space=pl.ANY),
            pl.BlockSpec((SUBLANES, n_states), const),
            pl.BlockSpec((SUBLANES, n_states), const),
            pl.BlockSpec((POOL_BUF * SUBLANES, D_POOL), const),
            pl.BlockSpec(((CONV_W - 1) * SUBLANES, 2 * D_FF), const),
        ],
        out_shape=[
            jax.ShapeDtypeStruct((nb, seq, D_MODEL), F32),
            jax.ShapeDtypeStruct((SUBLANES, n_states), F32),
            jax.ShapeDtypeStruct((SUBLANES, n_states), F32),
            jax.ShapeDtypeStruct((POOL_BUF * SUBLANES, D_POOL), F32),
            jax.ShapeDtypeStruct(((CONV_W - 1) * SUBLANES, 2 * D_FF), F32),
        ],
        scratch_shapes=[
            pltpu.VMEM((rows, D_FF), BF16),
            pltpu.VMEM((2, tc, SUBLANES, D_MODEL), F32),
            pltpu.VMEM((2, tc, SUBLANES, D_MODEL), F32),
            pltpu.SemaphoreType.DMA((2,)),
            pltpu.SemaphoreType.DMA((2,)),
        ],
        compiler_params=pltpu.CompilerParams(dimension_semantics=("arbitrary",), **cparams),
        name="prompt_layer",
    )(x_prompt, *weights)
    new_pool_p = jnp.transpose(p_pool.reshape(POOL_BUF, nb, D_POOL), (1, 0, 2))[None]
    new_conv_p = jnp.transpose(p_conv.reshape(CONV_W - 1, nb, 2 * D_FF), (1, 0, 2))[None]
    new_re_p = p_re.reshape(1, nb, S5_GROUPS, S5_STATE)
    new_im_p = p_im.reshape(1, nb, S5_GROUPS, S5_STATE)

    return (y_prompt, ys, new_re_p, new_im_p, new_pool_p, new_conv_p,
            sequence_major(s_re), sequence_major(s_im),
            jnp.transpose(s_pool, (1, 0, 2))[None], s_conv[None])
```

```python
import functools

import numpy as np
import jax
import jax.numpy as jnp
from jax import lax
from jax.experimental import pallas as pl
from jax.experimental.pallas import tpu as pltpu

D_MODEL = 1024
D_S5 = 512
S5_CH = 16
S5_GROUPS = 32
S5_STATE = 64
D_POOL = 512
POOL_WINDOWS = (2, 4, 8, 16)
POOL_CH = 128
POOL_BUF = 15
D_FF = 2816
CONV_W = 3
EPS = 1e-6

SUBLANES = 8
LANES = 128
S5_SPLIT = 2
GROUPS_PER_SPLIT = S5_GROUPS // S5_SPLIT
U_PER_SPLIT = GROUPS_PER_SPLIT * S5_CH
ST_PER_SPLIT = GROUPS_PER_SPLIT * S5_STATE
GROUPS_PER_TILE = LANES // S5_STATE
FF_CHUNK = 256
N_FF_CHUNKS = D_FF // FF_CHUNK
ROW_PARTS = 2
SQRT_HALF = float(np.sqrt(0.5).astype(np.float32))

BF16 = jnp.bfloat16
F32 = jnp.float32


def _dot(a, b):
    return jnp.dot(a, b, preferred_element_type=F32)


def _rms(x, g):
    ms = jnp.mean(x * x, axis=-1, keepdims=True)
    return x * lax.rsqrt(ms + EPS) * g


def _gelu(x):
    return 0.5 * x * (1.0 + lax.erf(x * SQRT_HALF))


def _s5_post(y_lin, u, dskip_ref, wglu_ref):
    y = _gelu(y_lin + dskip_ref[...] * u)
    return y * jax.nn.sigmoid(_dot(y.astype(BF16), wglu_ref[...]))


def _pool_project(pooled_cols, wpool_ref, pscale_ref):
    zero = jnp.zeros((POOL_CH, POOL_CH), BF16)
    outs = []
    for pair in range(len(POOL_WINDOWS) // 2):
        w_a = wpool_ref[(2 * pair) * POOL_CH:(2 * pair + 1) * POOL_CH, :]
        w_b = wpool_ref[(2 * pair + 1) * POOL_CH:(2 * pair + 2) * POOL_CH, :]
        w_pair = jnp.concatenate([jnp.concatenate([w_a, zero], axis=1),
                                  jnp.concatenate([zero, w_b], axis=1)], axis=0)
        lhs = jnp.concatenate([pooled_cols[2 * pair], pooled_cols[2 * pair + 1]], axis=1)
        outs.append(_dot(lhs.astype(BF16), w_pair))
    return jnp.concatenate(outs, axis=1) * pscale_ref[...]


def _block_copies(hbm_ref, buf_ref, sem_ref, block, slot, tc, to_hbm):
    copies = []
    for n in range(SUBLANES):
        hbm = hbm_ref.at[n, pl.ds(block * tc, tc), :]
        vmem = buf_ref.at[slot, :, n, :]
        src, dst = (vmem, hbm) if to_hbm else (hbm, vmem)
        copies.append(pltpu.make_async_copy(src, dst, sem_ref.at[slot]))
    return copies


def _prompt_kernel(tc, n_steps,
                   x_hbm, g1_ref, win_ref, are_ref, aim_ref, bbd_ref, cbd_ref, dskip_ref,
                   wglu_ref, wpool_ref, pscale_ref, wout_ref, g2_ref, wup_ref, cw_ref,
                   cb_ref, wdown_ref, gf_ref,
                   y_hbm, sre_out, sim_out, pool_ref, conv_out,
                   sre_ref, sim_ref, conv_ref, act_ref, xbuf, ybuf, sem_in, sem_out):
    rows = tc * SUBLANES
    step = pl.program_id(0)
    slot = step % 2

    @pl.when(step == 0)
    def _():
        sre_ref[...] = jnp.zeros_like(sre_ref)
        sim_ref[...] = jnp.zeros_like(sim_ref)
        pool_ref[...] = jnp.zeros_like(pool_ref)
        conv_ref[...] = jnp.zeros_like(conv_ref)
        for cp in _block_copies(x_hbm, xbuf, sem_in, 0, 0, tc, False):
            cp.start()

    @pl.when(step + 1 < n_steps)
    def _():
        for cp in _block_copies(x_hbm, xbuf, sem_in, step + 1, 1 - slot, tc, False):
            cp.start()

    @pl.when(step >= 2)
    def _():
        for cp in _block_copies(y_hbm, ybuf, sem_out, step - 2, slot, tc, True):
            cp.wait()

    for cp in _block_copies(x_hbm, xbuf, sem_in, step, slot, tc, False):
        cp.wait()

    x = xbuf[slot].reshape(rows, D_MODEL)
    part_rows = rows // ROW_PARTS
    row_parts = [slice(i * part_rows, (i + 1) * part_rows) for i in range(ROW_PARTS)]
    proj = jnp.concatenate(
        [_dot(_rms(x[r, :], g1_ref[...]).astype(BF16), win_ref[...]) for r in row_parts], axis=0)
    u = proj[:, :D_S5]
    v = proj[:, D_S5:]
    ub = u.astype(BF16)

    y_parts = []
    for k in range(S5_SPLIT):
        ub_k = ub[:, k * U_PER_SPLIT:(k + 1) * U_PER_SPLIT]
        for j in range(ST_PER_SPLIT // LANES):
            tile = slice(2 * j * LANES, (2 * j + 2) * LANES)
            bu = _dot(ub_k, bbd_ref[k, :, tile])
            lanes = slice(k * ST_PER_SPLIT + j * LANES, k * ST_PER_SPLIT + (j + 1) * LANES)
            a_re = jnp.broadcast_to(are_ref[:, lanes], (SUBLANES, LANES))
            a_im = jnp.broadcast_to(aim_ref[:, lanes], (SUBLANES, LANES))
            h_re, h_im = sre_ref[:, lanes], sim_ref[:, lanes]
            h_rows = []
            for i in range(tc // 2):
                r0 = i * 2 * SUBLANES
                res, ims = [], []
                for s in range(2):
                    rs = slice(r0 + s * SUBLANES, r0 + (s + 1) * SUBLANES)
                    n_re = a_re * h_re - a_im * h_im + bu[rs, :LANES]
                    n_im = a_re * h_im + a_im * h_re + bu[rs, LANES:]
                    h_re, h_im = n_re, n_im
                    res.append(h_re)
                    ims.append(h_im)
                h_rows.append(jnp.concatenate(
                    [jnp.concatenate(res, axis=0), jnp.concatenate(ims, axis=0)], axis=1).astype(BF16))
            sre_ref[:, lanes] = h_re
            sim_ref[:, lanes] = h_im
            part = _dot(jnp.concatenate(h_rows, axis=0), cbd_ref[k, tile, :])
            y_k = part if j == 0 else y_k + part
        y_parts.append(y_k)
    y_s5 = _s5_post(jnp.concatenate(y_parts, axis=1), u, dskip_ref, wglu_ref)

    vfull = jnp.concatenate([pool_ref[...], v], axis=0)
    pool_ref[...] = vfull[rows:, :]
    t_idx = step * tc + (lax.broadcasted_iota(jnp.int32, (rows, POOL_CH), 0) >> 3)
    pooled = []
    for gi, w in enumerate(POOL_WINDOWS):
        s = vfull[:, gi * POOL_CH:(gi + 1) * POOL_CH]
        span = 1
        while span < w:
            sh = span * SUBLANES
            s = s[sh:, :] + s[:-sh, :]
            span *= 2
        first = (POOL_BUF - (w - 1)) * SUBLANES
        wsum = s[first:first + rows, :]
        cnt = jnp.minimum(t_idx + 1, w).astype(F32)
        pooled.append(wsum / cnt - v[:, gi * POOL_CH:(gi + 1) * POOL_CH])
    y_pool = _pool_project(pooled, wpool_ref, pscale_ref)

    mixed = jnp.concatenate([y_s5.astype(BF16), y_pool.astype(BF16)], axis=1)
    x1_parts = [x[r, :] + _dot(mixed[r, :], wout_ref[...]) for r in row_parts]
    h2b = jnp.concatenate([_rms(xp, g2_ref[...]).astype(BF16) for xp in x1_parts], axis=0)

    taps = (CONV_W - 1) * SUBLANES
    for j in range(N_FF_CHUNKS):
        convd = []
        for base in (0, D_FF):
            cols = slice(base + j * FF_CHUNK, base + (j + 1) * FF_CHUNK)
            hup = _dot(h2b, wup_ref[:, cols])
            full = jnp.concatenate([conv_ref[:, cols], hup], axis=0)
            conv_ref[:, cols] = hup[rows - taps:, :]
            c = cb_ref[:, cols]
            for kk in range(CONV_W):
                c = c + cw_ref[kk:kk + 1, cols] * full[kk * SUBLANES:kk * SUBLANES + rows, :]
            convd.append(c)
        act_ref[:, j * FF_CHUNK:(j + 1) * FF_CHUNK] = (_gelu(convd[0]) * convd[1]).astype(BF16)
    y = jnp.concatenate(
        [_rms(xp + _dot(act_ref[r, :], wdown_ref[...]), gf_ref[...])
         for xp, r in zip(x1_parts, row_parts)], axis=0)

    ybuf[slot] = y.reshape(tc, SUBLANES, D_MODEL)
    for cp in _block_copies(y_hbm, ybuf, sem_out, step, slot, tc, True):
        cp.start()

    @pl.when(step == n_steps - 1)
    def _():
        for g in range(S5_GROUPS):
            sre_out[:, g, :] = sre_ref[:, g * S5_STATE:(g + 1) * S5_STATE]
            sim_out[:, g, :] = sim_ref[:, g * S5_STATE:(g + 1) * S5_STATE]
        for r in range(CONV_W - 1):
            conv_out[:, r, :] = conv_ref[r * SUBLANES:(r + 1) * SUBLANES, :]
        if n_steps >= 2:
            for cp in _block_copies(y_hbm, ybuf, sem_out, step - 1, 1 - slot, tc, True):
                cp.wait()
        for cp in _block_copies(y_hbm, ybuf, sem_out, step, slot, tc, True):
            cp.wait()


N_BIG_WEIGHTS = 6
STAGE_SHAPES = ((6, 256, D_MODEL), (2, 256, D_S5), (2, 256, POOL_CH), (6, D_MODEL, FF_CHUNK))
STAGE_LOOKAHEAD = 5


def _sample_kernel(x_hbm, sre_in, sim_in, pool_hbm, conv_hbm,
                   g1_ref, are_ref, aim_ref, bbd_ref, cbd_ref, dskip_ref, pscale_ref, g2_ref,
                   cw_ref, cb_ref, gf_ref,
                   win_hbm, wglu_hbm, wpool_hbm, wout_hbm, wup_hbm, wdown_hbm,
                   y_hbm, sre_ref, sim_ref, pool_out, conv_out,
                   winb_out, wglub_out, wpoolb_out, woutb_out, wupb_out, wdownb_out,
                   win_ref, wglu_ref, wpool_ref, wout_ref, wup_ref, wdown_ref,
                   stage_model, stage_s5, stage_pool, stage_col,
                   xbuf, ybuf, poolbuf, convbuf, vbuf,
                   sem_stage, sem_state, sem_wout, sem_sout):
    in_copies = [pltpu.make_async_copy(x_hbm, xbuf, sem_state.at[0]),
                 pltpu.make_async_copy(pool_hbm, poolbuf, sem_state.at[1]),
                 pltpu.make_async_copy(conv_hbm, convbuf, sem_state.at[2])]
    for cp in in_copies:
        cp.start()

    stages = (stage_model, stage_s5, stage_pool, stage_col)
    sem_base = [sum(s.shape[0] for s in stages[:i]) for i in range(len(stages))]
    ring_pos = [0] * len(stages)
    queue = []
    out_copies = []
    live = {}

    def enqueue(src, sid, dst, after=None):
        queue.append((src, sid, ring_pos[sid] % stages[sid].shape[0], dst, after))
        ring_pos[sid] += 1

    def start_out(src, dst, sem):
        cp = pltpu.make_async_copy(src, dst, sem)
        cp.start()
        out_copies.append(cp)

    def enqueue_rows(w_hbm, wb_ref, sid, after_last):
        rows = stages[sid].shape[1]
        n_chunks = w_hbm.shape[0] // rows
        for c in range(n_chunks):
            part = pl.ds(c * rows, rows)
            enqueue(w_hbm.at[part, :], sid, wb_ref.at[part, :], after_last if c == n_chunks - 1 else None)

    def mixer():
        for cp in in_copies:
            cp.wait()
        start_out(wout_ref, woutb_out, sem_wout.at[3])
        x = xbuf[:, 0, :]
        hb = _rms(x, g1_ref[...]).astype(BF16)
        proj = _dot(hb, win_ref[...])
        u = proj[:, :D_S5]
        v = proj[:, D_S5:]
        ub = u.astype(BF16)

        y_parts = []
        for k in range(S5_SPLIT):
            bu = _dot(ub[:, k * U_PER_SPLIT:(k + 1) * U_PER_SPLIT], bbd_ref[k])
            h_tiles = []
            for j in range(ST_PER_SPLIT // LANES):
                lanes = slice(k * ST_PER_SPLIT + j * LANES, k * ST_PER_SPLIT + (j + 1) * LANES)
                a_re = are_ref[:, lanes]
                a_im = aim_ref[:, lanes]
                h_re0 = sre_in[lanes, :].T
                h_im0 = sim_in[lanes, :].T
                h_re = a_re * h_re0 - a_im * h_im0 + bu[:, 2 * j * LANES:(2 * j + 1) * LANES]
                h_im = a_re * h_im0 + a_im * h_re0 + bu[:, (2 * j + 1) * LANES:(2 * j + 2) * LANES]
                sre_ref[lanes, :] = h_re.T
                sim_ref[lanes, :] = h_im.T
                h_tiles += [h_re, h_im]
            hcat = jnp.concatenate(h_tiles, axis=1).astype(BF16)
            y_parts.append(_dot(hcat, cbd_ref[k]))
        y_s5 = _s5_post(jnp.concatenate(y_parts, axis=1), u, dskip_ref, wglu_ref)

        pooled = []
        for gi, w in enumerate(POOL_WINDOWS):
            lanes = slice(gi * POOL_CH, (gi + 1) * POOL_CH)
            vc = v[:, lanes]
            wsum = vc
            for back in range(1, w):
                wsum = wsum + poolbuf[POOL_BUF - back, :, lanes]
            pooled.append(wsum / float(w) - vc)
        y_pool = _pool_project(pooled, wpool_ref, pscale_ref)
        vbuf[...] = v
        start_out(poolbuf.at[pl.ds(1, POOL_BUF - 1)], pool_out.at[pl.ds(0, POOL_BUF - 1)], sem_sout.at[0])
        start_out(vbuf, pool_out.at[POOL_BUF - 1], sem_sout.at[1])

        x1 = (x + _dot(y_s5.astype(BF16), wout_ref[:D_S5, :])
              + _dot(y_pool.astype(BF16), wout_ref[D_S5:, :]))
        live["x1"] = x1
        live["h2b"] = _rms(x1, g2_ref[...]).astype(BF16)
        live["acc"] = None

    def ffn_chunk(j):
        rows_j = pl.ds(j * FF_CHUNK, FF_CHUNK)
        convd = []
        for base in (0, D_FF):
            cols = slice(base + j * FF_CHUNK, base + (j + 1) * FF_CHUNK)
            start_out(wup_ref.at[:, cols], wupb_out.at[:, cols], sem_wout.at[4])
            hup = _dot(live["h2b"], wup_ref[:, cols])
            older = convbuf[:, 0, cols]
            newer = convbuf[:, 1, cols]
            convbuf[:, 0, cols] = newer
            convbuf[:, 1, cols] = hup
            convd.append(cb_ref[:, cols] + cw_ref[0:1, cols] * older
                         + cw_ref[1:2, cols] * newer + cw_ref[2:3, cols] * hup)
        start_out(wdown_ref.at[rows_j, :], wdownb_out.at[rows_j, :], sem_wout.at[5])
        act = (_gelu(convd[0]) * convd[1]).astype(BF16)
        part = _dot(act, wdown_ref[j * FF_CHUNK:(j + 1) * FF_CHUNK, :])
        live["acc"] = part if live["acc"] is None else live["acc"] + part

    enqueue_rows(win_hbm, win_ref, 0, lambda: start_out(win_ref, winb_out, sem_wout.at[0]))
    enqueue_rows(wglu_hbm, wglu_ref, 1, lambda: start_out(wglu_ref, wglub_out, sem_wout.at[1]))
    enqueue_rows(wpool_hbm, wpool_ref, 2, lambda: start_out(wpool_ref, wpoolb_out, sem_wout.at[2]))
    enqueue_rows(wout_hbm, wout_ref, 0, mixer)
    for j in range(N_FF_CHUNKS):
        for base in (0, D_FF):
            cols = pl.ds(base + j * FF_CHUNK, FF_CHUNK)
            enqueue(wup_hbm.at[:, cols], 3, wup_ref.at[:, cols])
        rows_j = pl.ds(j * FF_CHUNK, FF_CHUNK)
        enqueue(wdown_hbm.at[rows_j, :], 0, wdown_ref.at[rows_j, :], functools.partial(ffn_chunk, j))

    def read(entry):
        src, sid, slot, _, _ = entry
        return pltpu.make_async_copy(src, stages[sid].at[slot], sem_stage.at[sem_base[sid] + slot])

    started = 0
    for i, entry in enumerate(queue):
        while started < min(len(queue), i + 1 + STAGE_LOOKAHEAD):
            read(queue[started]).start()
            started += 1
        read(entry).wait()
        _, sid, slot, dst, after = entry
        dst[...] = stages[sid][slot].astype(BF16)
        if after is not None:
            after()

    ybuf[:, 0, :] = _rms(live["x1"] + live["acc"], gf_ref[...])
    start_out(convbuf, conv_out, sem_sout.at[2])
    start_out(ybuf, y_hbm, sem_sout.at[3])
    for cp in out_copies:
        cp.wait()


def _s5_tables(a_re, a_im, log_dt, b_re, b_im, c_re, c_im):
    dt = jnp.exp(log_dt)[:, None]
    mag = jnp.exp(dt * a_re)
    abar_re = mag * jnp.cos(dt * a_im)
    abar_im = mag * jnp.sin(dt * a_im)
    nr, ni = abar_re - 1.0, abar_im
    den = a_re * a_re + a_im * a_im
    f_re = ((nr * a_re + ni * a_im) / den)[:, :, None]
    f_im = ((ni * a_re - nr * a_im) / den)[:, :, None]
    bbar_re = f_re * b_re - f_im * b_im
    bbar_im = f_re * b_im + f_im * b_re
    n_tiles = GROUPS_PER_SPLIT // GROUPS_PER_TILE
    state_cols = np.arange(2 * ST_PER_SPLIT)
    col_group = (state_cols // (2 * LANES)) * GROUPS_PER_TILE + (state_cols % LANES) // S5_STATE
    diag = jnp.asarray(np.arange(U_PER_SPLIT)[:, None] // S5_CH == col_group[None, :])

    b6 = jnp.stack([bbar_re, bbar_im]).reshape(2, S5_SPLIT, n_tiles, GROUPS_PER_TILE, S5_STATE, S5_CH)
    row = jnp.transpose(b6, (1, 5, 2, 0, 3, 4)).reshape(S5_SPLIT, 1, S5_CH, 2 * ST_PER_SPLIT)
    rep = jnp.broadcast_to(row, (S5_SPLIT, GROUPS_PER_SPLIT, S5_CH, 2 * ST_PER_SPLIT))
    bbd = jnp.where(diag, rep.reshape(S5_SPLIT, U_PER_SPLIT, 2 * ST_PER_SPLIT), 0.0)

    c6 = jnp.stack([c_re, -c_im]).reshape(2, S5_SPLIT, n_tiles, GROUPS_PER_TILE, S5_CH, S5_STATE)
    col = jnp.transpose(c6, (1, 2, 0, 3, 5, 4)).reshape(S5_SPLIT, 2 * ST_PER_SPLIT, S5_CH)
    cbd = jnp.where(diag.T, jnp.tile(col, (1, 1, GROUPS_PER_SPLIT)), 0.0)
    return abar_re.reshape(1, -1), abar_im.reshape(1, -1), bbd.astype(BF16), cbd.astype(BF16)


def _vmem_spec():
    return pl.BlockSpec(memory_space=pltpu.VMEM)


PROMPT_TC = 64
VMEM_LIMIT_BYTES = 60 * 1024 * 1024


def kernel(x_prompt, x_sample, state_s5_re, state_s5_im, state_pool, state_ffn_conv, norm_mix_g, w_in, s5_a_re, s5_a_im, s5_log_dt, s5_b_re, s5_b_im, s5_c_re, s5_c_im, s5_d, s5_w_glu, pool_w, pool_scale, w_out, norm_ffn_g, ffn_w_up, ffn_conv_w, ffn_conv_b, ffn_w_down, norm_final_g):
    nb, seq, _ = x_prompt.shape
    ns = x_sample.shape[0]
    assert nb == SUBLANES and seq % PROMPT_TC == 0 and x_sample.shape[1] == 1
    assert norm_mix_g.shape[0] == 1, "single layer"

    a_re, a_im, bbd, cbd = _s5_tables(s5_a_re[0], s5_a_im[0], s5_log_dt[0], s5_b_re[0],
                                      s5_b_im[0], s5_c_re[0], s5_c_im[0])
    n_states = S5_GROUPS * S5_STATE
    cparams = dict(vmem_limit_bytes=VMEM_LIMIT_BYTES)
    g1 = norm_mix_g[0].reshape(1, D_MODEL)
    dskip = s5_d[0].reshape(1, D_S5)
    pscale = pool_scale[0].reshape(1, D_POOL)
    g2 = norm_ffn_g[0].reshape(1, D_MODEL)
    cw = ffn_conv_w[0]
    cb = ffn_conv_b[0].reshape(1, 2 * D_FF)
    gf = norm_final_g.reshape(1, D_MODEL)

    big_f32 = (w_in[0], s5_w_glu[0], pool_w[0].reshape(len(POOL_WINDOWS) * POOL_CH, POOL_CH),
               w_out[0], ffn_w_up[0], ffn_w_down[0])
    small = (g1, a_re, a_im, bbd, cbd, dskip, pscale, g2, cw, cb, gf)
    any_spec = pl.BlockSpec(memory_space=pl.ANY)
    pool_in = jnp.transpose(state_pool[0], (1, 0, 2))

    def state_major(s):
        return jnp.transpose(s, (1, 2, 0)).reshape(n_states, ns)

    def sequence_major(s):
        return jnp.transpose(s.reshape(S5_GROUPS, S5_STATE, ns), (2, 0, 1))[None]
    sample_out = pl.pallas_call(
        _sample_kernel,
        in_specs=[any_spec] + [_vmem_spec()] * 2 + [any_spec] * 2 + [_vmem_spec()] * len(small)
                 + [any_spec] * N_BIG_WEIGHTS,
        out_specs=[any_spec] + [_vmem_spec()] * 2 + [any_spec] * (2 + N_BIG_WEIGHTS),
        out_shape=[
            jax.ShapeDtypeStruct((ns, 1, D_MODEL), F32),
            jax.ShapeDtypeStruct((n_states, ns), F32),
            jax.ShapeDtypeStruct((n_states, ns), F32),
            jax.ShapeDtypeStruct((POOL_BUF, ns, D_POOL), F32),
            jax.ShapeDtypeStruct((ns, CONV_W - 1, 2 * D_FF), F32),
        ] + [jax.ShapeDtypeStruct(w.shape, BF16) for w in big_f32],
        scratch_shapes=[pltpu.VMEM(w.shape, BF16) for w in big_f32]
        + [pltpu.VMEM(shape, F32) for shape in STAGE_SHAPES] + [
            pltpu.VMEM((ns, 1, D_MODEL), F32),
            pltpu.VMEM((ns, 1, D_MODEL), F32),
            pltpu.VMEM((POOL_BUF, ns, D_POOL), F32),
            pltpu.VMEM((ns, CONV_W - 1, 2 * D_FF), F32),
            pltpu.VMEM((ns, D_POOL), F32),
            pltpu.SemaphoreType.DMA((sum(shape[0] for shape in STAGE_SHAPES),)),
            pltpu.SemaphoreType.DMA((3,)),
            pltpu.SemaphoreType.DMA((N_BIG_WEIGHTS,)),
            pltpu.SemaphoreType.DMA((4,)),
        ],
        compiler_params=pltpu.CompilerParams(**cparams),
        name="sample_layer",
    )(x_sample, state_major(state_s5_re[0]), state_major(state_s5_im[0]),
      pool_in, state_ffn_conv[0], *small, *big_f32)
    ys, s_re, s_im, s_pool, s_conv = sample_out[:5]
    win_b, wglu_b, wpool_b, wout_b, wup_b, wdown_b = sample_out[5:]
    weights = (g1, win_b, a_re, a_im, bbd, cbd, dskip, wglu_b, wpool_b, pscale, wout_b, g2,
               wup_b, cw, cb, wdown_b, gf)

    tc = PROMPT_TC
    rows = tc * SUBLANES
    const = lambda i: (0, 0)
    const3 = lambda i: (0, 0, 0)
    y_prompt, p_re, p_im, p_pool, p_conv = pl.pallas_call(
        functools.partial(_prompt_kernel, tc, seq // tc),
        grid=(seq // tc,),
        in_specs=[pl.BlockSpec(memory_space=pl.ANY)] + [_vmem_spec()] * len(weights),
        out_specs=[
            pl.BlockSpec(memory_space=pl.ANY),
            pl.BlockSpec((nb, S5_GROUPS, S5_STATE), const3),
            pl.BlockSpec((nb, S5_GROUPS, S5_STATE), const3),
            pl.BlockSpec((POOL_BUF * SUBLANES, D_POOL), const),
            pl.BlockSpec((nb, CONV_W - 1, 2 * D_FF), const3),
        ],
        out_shape=[
            jax.ShapeDtypeStruct((nb, seq, D_MODEL), F32),
            jax.ShapeDtypeStruct((nb, S5_GROUPS, S5_STATE), F32),
            jax.ShapeDtypeStruct((nb, S5_GROUPS, S5_STATE), F32),
            jax.ShapeDtypeStruct((POOL_BUF * SUBLANES, D_POOL), F32),
            jax.ShapeDtypeStruct((nb, CONV_W - 1, 2 * D_FF), F32),
        ],
        scratch_shapes=[
            pltpu.VMEM((SUBLANES, n_states), F32),
            pltpu.VMEM((SUBLANES, n_states), F32),
            pltpu.VMEM(((CONV_W - 1) * SUBLANES, 2 * D_FF), F32),
            pltpu.VMEM((rows, D_FF), BF16),
            pltpu.VMEM((2, tc, SUBLANES, D_MODEL), F32),
            pltpu.VMEM((2, tc, SUBLANES, D_MODEL), F32),
            pltpu.SemaphoreType.DMA((2,)),
            pltpu.SemaphoreType.DMA((2,)),
        ],
        compiler_params=pltpu.CompilerParams(dimension_semantics=("arbitrary",), **cparams),
        name="prompt_layer",
    )(x_prompt, *weights)
    new_pool_p = jnp.transpose(p_pool.reshape(POOL_BUF, nb, D_POOL), (1, 0, 2))[None]

    return (y_prompt, ys, p_re[None], p_im[None], new_pool_p, p_conv[None],
            sequence_major(s_re), sequence_major(s_im),
            jnp.transpose(s_pool, (1, 0, 2))[None], s_conv[None])
```
